```python
import math
import jax
import jax.numpy as jnp
from jax import lax
import numpy as np

D_MODEL = 4096
BATCH = 32
SEQ = 256
DEPTH = 2
DEC_BATCH = 4
DEC_SEQ = 1024
PAST_LEN = 256

GRID_W = 64
QB = 128
WINDOW = 128
ROPE_BASE = 10000.0
EPS = 1e-6
NEG_INF = -1e30
N_BRANCH = 4
MIX_W = D_MODEL // 4
A_HEAD_DIM = 128
A_HEADS = MIX_W // A_HEAD_DIM
A_KV_HEADS = A_HEADS // 4
A_GROUPS = A_HEADS // A_KV_HEADS
HY_W = MIX_W
HY_EMB = 33
HY_HID = 64
HY_N_INNER = 2
HY_TARGET = 1e-2
HY_FAST = 0.3
HY_SLOW = 1.5
S5_CH = 16
S5_GROUPS = MIX_W // S5_CH
S5_STATE = 64
D_HEADS = 8
D_HEAD_DIM = MIX_W // D_HEADS // 2
D_FF = ((8 * D_MODEL // 3 + 255) // 256) * 256
N_MOD = 9
IN_SPLITS = (A_HEADS * A_HEAD_DIM, A_KV_HEADS * A_HEAD_DIM, A_KV_HEADS * A_HEAD_DIM,
             3 * HY_W, MIX_W,
             2 * D_HEADS * D_HEAD_DIM, 2 * D_HEADS * D_HEAD_DIM, 2 * D_HEADS * D_HEAD_DIM,
             N_BRANCH * D_MODEL)
N_IN = sum(IN_SPLITS)

kernel_name = 'hybrid_diffusion_prefix_trunk'

F32 = jnp.float32


def rms(x, g):
    xf = x.astype(F32)
    y = xf * lax.rsqrt(jnp.mean(xf * xf, axis=-1, keepdims=True) + EPS)
    return (y * g.astype(F32)).astype(x.dtype)


def swiglu(x, w_gu, w_d):
    g, u = jnp.split(x @ w_gu, 2, axis=-1)
    return (jax.nn.silu(g) * u) @ w_d


def rope2d(x):
    L, n = x.shape[1], x.shape[-1]
    n_rows = L // GRID_W
    rows = jnp.repeat(jnp.arange(n_rows), GRID_W).astype(F32)
    cols = jnp.tile(jnp.arange(GRID_W), n_rows).astype(F32)
    quarter = n // 4
    freqs = ROPE_BASE ** (-jnp.arange(quarter, dtype=F32) / quarter)
    bshape = (L,) + (1,) * (x.ndim - 3) + (quarter,)

    def rot(xh, pos):
        ang = (pos[:, None] * freqs[None, :]).reshape(bshape)
        cos, sin = jnp.cos(ang).astype(x.dtype), jnp.sin(ang).astype(x.dtype)
        a, b = xh[..., :quarter], xh[..., quarter:]
        return jnp.concatenate([a * cos - b * sin, a * sin + b * cos], axis=-1)

    return jnp.concatenate([rot(x[..., :2 * quarter], rows), rot(x[..., 2 * quarter:], cols)], axis=-1)


def to_blocks(x):
    b, s = x.shape[:2]
    return jnp.moveaxis(x.reshape((b, s // QB, QB) + x.shape[2:]), 1, 0)


def from_blocks(x):
    x = jnp.moveaxis(x, 0, 1)
    return x.reshape((x.shape[0], x.shape[1] * x.shape[2]) + x.shape[3:])


def sink_attention(q, k, v, sink, mask):
    s = jnp.einsum('bqkgd,bskd->bkgqs', q, k).astype(F32) * (A_HEAD_DIM ** -0.5)
    if mask is not None:
        s = jnp.where(mask, s, NEG_INF)
    sink_col = jnp.broadcast_to(sink.astype(F32)[None, :, :, None, None], s.shape[:-1] + (1,))
    p = jax.nn.softmax(jnp.concatenate([s, sink_col], axis=-1), axis=-1)[..., :-1]
    return jnp.einsum('bkgqs,bskd->bqkgd', p.astype(v.dtype), v)


def mixer_a(q, k, v, qn, kn, sink, ctx):
    B, L = q.shape[:2]
    q = rms(q.reshape(B, L, A_KV_HEADS, A_GROUPS, A_HEAD_DIM), qn)
    k = rms(k.reshape(B, L, A_KV_HEADS, A_HEAD_DIM), kn)
    v = v.reshape(B, L, A_KV_HEADS, A_HEAD_DIM)
    sink = sink.reshape(A_KV_HEADS, A_GROUPS)
    if ctx is None:
        o = lax.map(lambda qb: sink_attention(qb, k, v, sink, None), to_blocks(q))
        return from_blocks(o).reshape(B, L, MIX_W), (k, v)
    k_ctx, v_ctx = ctx
    q, k = rope2d(q), rope2d(k)
    nb = L // QB

    def band(t):
        tb = jnp.pad(t, ((0, 0), (QB, QB), (0, 0), (0, 0))).reshape(B, nb + 2, QB, A_KV_HEADS, A_HEAD_DIM)
        return jnp.moveaxis(jnp.concatenate([tb[:, :-2], tb[:, 1:-1], tb[:, 2:]], axis=2), 1, 0)

    qpos = jnp.arange(nb)[:, None, None] * QB + jnp.arange(QB)[None, :, None]
    kpos = jnp.arange(nb)[:, None, None] * QB - QB + jnp.arange(3 * QB)[None, None, :]
    band_mask = (kpos >= 0) & (kpos < L) & (jnp.abs(qpos - kpos) <= WINDOW)
    ctx_mask = jnp.ones((QB, k_ctx.shape[1]), dtype=bool)

    def blk(args):
        qb, kb, vb, mb = args
        keys = jnp.concatenate([kb, k_ctx], axis=1)
        vals = jnp.concatenate([vb, v_ctx], axis=1)
        return sink_attention(qb, keys, vals, sink, jnp.concatenate([mb, ctx_mask], axis=1))

    o = lax.map(blk, (to_blocks(q), band(k), band(v), band_mask))
    return from_blocks(o).reshape(B, L, MIX_W), None


def short_conv3(x, w, b):
    xp = jnp.pad(x, ((0, 0), (1, 1), (0, 0)))
    return xp[:, :-2] * w[0] + xp[:, 1:-1] * w[1] + xp[:, 2:] * w[2] + b


def hyena_filters(L, w_in, b_in, w_hid, b_hid, w_out, freq):
    t = jnp.linspace(0.0, 1.0, L, dtype=F32)[:, None]
    bands = (HY_EMB - 1) // 2
    w = 2.0 * math.pi * jnp.arange(L, dtype=F32)[:, None] / L
    f = jnp.linspace(1e-4, bands - 1, bands, dtype=F32)[None, :]
    z = jnp.concatenate([t, jnp.cos(f * w), -jnp.sin(f * w)], axis=-1)
    fr = freq.astype(F32)
    h = jnp.sin(fr * (z @ w_in.astype(F32) + b_in.astype(F32)))
    for i in range(w_hid.shape[0]):
        h = jnp.sin(fr * (h @ w_hid[i].astype(F32) + b_hid[i].astype(F32)))
    h = h @ w_out.astype(F32)
    deltas = jnp.abs(jnp.linspace(math.log(HY_TARGET) / HY_FAST, math.log(HY_TARGET) / HY_SLOW, HY_W, dtype=F32))
    decay = jnp.exp(-t * deltas[None, :])
    return h[:, :HY_W] * decay, h[:, HY_W:] * decay


def two_sided_fftconv(u, h_fwd, h_bwd):
    L = u.shape[1]
    taps = jnp.concatenate([h_fwd, jnp.zeros_like(h_fwd[:1]), h_bwd[:0:-1]], axis=0)
    k_f = jnp.fft.rfft(taps, n=2 * L, axis=0)
    u_f = jnp.fft.rfft(u.astype(F32), n=2 * L, axis=1)
    return jnp.fft.irfft(u_f * k_f[None], n=2 * L, axis=1)[:, :L]


def mixer_b(u, conv_w, conv_b, w_in, b_in, w_hid, b_hid, w_out, freq, bias):
    L = u.shape[1]
    x0, x1, v = jnp.split(short_conv3(u, conv_w, conv_b), 3, axis=-1)
    h_fwd, h_bwd = hyena_filters(L, w_in, b_in, w_hid, b_hid, w_out, freq)
    g = (v * x1).astype(F32)
    y = two_sided_fftconv(g, h_fwd, h_bwd) + g * bias.astype(F32)
    return (y * x0.astype(F32)).astype(u.dtype)


def cmul(ar, ai, br, bi):
    return ar * br - ai * bi, ar * bi + ai * br


def s5_discretize(lam_re, lam_im, log_step, b_re, b_im):
    dt = jnp.exp(log_step.astype(F32))[:, None]
    lr, li = lam_re.astype(F32), lam_im.astype(F32)
    mag = jnp.exp(lr * dt)
    a_re, a_im = mag * jnp.cos(li * dt), mag * jnp.sin(li * dt)
    den = lr * lr + li * li
    q_re = ((a_re - 1.0) * lr + a_im * li) / den
    q_im = (a_im * lr - (a_re - 1.0) * li) / den
    bb_re, bb_im = cmul(q_re[..., None], q_im[..., None], b_re.astype(F32), b_im.astype(F32))
    return a_re, a_im, bb_re, bb_im


def s5_scan(u, a_re, a_im, bb_re, bb_im, h0, reverse):
    bu_re = jnp.einsum('blgc,gnc->blgn', u, bb_re)
    bu_im = jnp.einsum('blgc,gnc->blgn', u, bb_im)
    if h0 is not None:
        idx = -1 if reverse else 0
        hr, hi = cmul(a_re, a_im, h0[..., 0].astype(F32), h0[..., 1].astype(F32))
        bu_re = bu_re.at[:, idx].add(hr)
        bu_im = bu_im.at[:, idx].add(hi)
    ar = jnp.broadcast_to(a_re, bu_re.shape)
    ai = jnp.broadcast_to(a_im, bu_im.shape)

    def op(e1, e2):
        a1r, a1i, b1r, b1i = e1
        a2r, a2i, b2r, b2i = e2
        nar, nai = cmul(a2r, a2i, a1r, a1i)
        nbr, nbi = cmul(a2r, a2i, b1r, b1i)
        return nar, nai, nbr + b2r, nbi + b2i

    _, _, hr, hi = lax.associative_scan(op, (ar, ai, bu_re, bu_im), reverse=reverse, axis=1)
    return hr, hi


def mixer_c(u, lam_re, lam_im, log_step, b_re, b_im, c_re, c_im, d, glu_w, glu_b, ctx):
    B, L = u.shape[:2]
    uf = u.astype(F32).reshape(B, L, S5_GROUPS, S5_CH)
    y = uf * d.astype(F32).reshape(S5_GROUPS, S5_CH)
    finals = []
    for dr in range(2):
        a_re, a_im, bb_re, bb_im = s5_discretize(lam_re[dr], lam_im[dr], log_step[dr], b_re[dr], b_im[dr])
        h0 = None if ctx is None else ctx[:, dr]
        hr, hi = s5_scan(uf, a_re, a_im, bb_re, bb_im, h0, reverse=(dr == 1))
        y = y + jnp.einsum('blgn,gcn->blgc', hr, c_re[dr].astype(F32)) - jnp.einsum('blgn,gcn->blgc', hi, c_im[dr].astype(F32))
        if ctx is None:
            idx = 0 if dr == 1 else -1
            finals.append(jnp.stack([hr[:, idx], hi[:, idx]], axis=-1))
    gy = jax.nn.gelu(y.reshape(B, L, MIX_W)).astype(u.dtype)
    o, g = jnp.split(gy @ glu_w + glu_b, 2, axis=-1)
    out = o * jax.nn.sigmoid(g)
    new = jnp.stack(finals, axis=1) if ctx is None else None
    return out, new


def diff_attention(q, k, v, lam):
    s = jnp.einsum('bqhcd,bshcd->bchqs', q, k).astype(F32) * (D_HEAD_DIM ** -0.5)
    p = jax.nn.softmax(s, axis=-1)
    a = p[:, 0] - lam * p[:, 1]
    return jnp.einsum('bhqs,bshe->bqhe', a.astype(v.dtype), v)


def mixer_d(q, k, v, qn, kn, lam_vecs, subln, lam_init, ctx):
    B, L = q.shape[:2]
    q = rms(q.reshape(B, L, D_HEADS, 2, D_HEAD_DIM), qn)
    k = rms(k.reshape(B, L, D_HEADS, 2, D_HEAD_DIM), kn)
    v = v.reshape(B, L, D_HEADS, 2 * D_HEAD_DIM)
    lv = lam_vecs.astype(F32)
    lam = jnp.exp(jnp.sum(lv[0] * lv[1])) - jnp.exp(jnp.sum(lv[2] * lv[3])) + lam_init
    if ctx is None:
        keys, vals = k, v
        new = (k.reshape(B, L, D_HEADS, 2 * D_HEAD_DIM), v)
    else:
        k_ctx, v_ctx = ctx
        q, k = rope2d(q), rope2d(k)
        keys = jnp.concatenate([k, k_ctx.reshape(B, -1, D_HEADS, 2, D_HEAD_DIM)], axis=1)
        vals = jnp.concatenate([v, v_ctx], axis=1)
        new = None
    o = from_blocks(lax.map(lambda qb: diff_attention(qb, keys, vals, lam), to_blocks(q)))
    o = rms(o, subln) * (1.0 - lam_init)
    return o.reshape(B, L, MIX_W), new


def token_mixing(h, l, P, ctx):
    B, L = h.shape[:2]
    z = h @ P['w_in'][l]
    points = [int(p) for p in np.cumsum(IN_SPLITS)[:-1]]
    aq, ak, av, hy, su, dq, dk, dv, gt = jnp.split(z, points, axis=-1)
    ctx_a = ctx_d = ctx_c = None
    if ctx is not None:
        ctx_a, ctx_d, ctx_c = (ctx[0], ctx[1]), (ctx[2], ctx[3]), ctx[4]
    oa, new_a = mixer_a(aq, ak, av, P['a_q_norm'][l], P['a_k_norm'][l], P['a_sink'][l], ctx_a)
    ob = mixer_b(hy, P['hy_conv_w'][l], P['hy_conv_b'][l], P['hy_w_in'][l], P['hy_b_in'][l],
                 P['hy_w_hid'][l], P['hy_b_hid'][l], P['hy_w_out'][l], P['hy_freq'][l], P['hy_bias'][l])
    oc, new_c = mixer_c(su, P['s5_lam_re'][l], P['s5_lam_im'][l], P['s5_log_step'][l], P['s5_b_re'][l],
                        P['s5_b_im'][l], P['s5_c_re'][l], P['s5_c_im'][l], P['s5_d'][l],
                        P['s5_glu_w'][l], P['s5_glu_b'][l], ctx_c)
    lam_init = 0.8 - 0.6 * math.exp(-0.3 * l)
    od, new_d = mixer_d(dq, dk, dv, P['d_q_norm'][l], P['d_k_norm'][l], P['d_lambda'][l],
                        P['d_subln'][l], lam_init, ctx_d)
    branches = jnp.stack([oa, ob, oc, od], axis=2)
    y = jnp.einsum('blnm,nmd->blnd', branches, P['w_branch'][l])
    gates = jax.nn.sigmoid(gt.reshape(B, L, N_BRANCH, D_MODEL))
    out = jnp.sum(gates * y, axis=2) @ P['w_out'][l]
    new = None if ctx is not None else (new_a[0], new_a[1], new_d[0], new_d[1], new_c)
    return out, new


def layer(x, cond, l, P, ctx):
    mod = (jax.nn.silu(cond) @ P['ada_w'][l] + P['ada_b'][l])[:, None, :]
    sh1, sc1, g1, sh2, sc2, g2, sh3, sc3, g3 = jnp.split(mod, N_MOD, axis=-1)
    ng = P['norm_g'][l]
    h = rms(x, ng[0]) * (1.0 + sc1) + sh1
    x = x + 0.5 * g1 * swiglu(h, P['ffn_w_gu'][l, 0], P['ffn_w_d'][l, 0])
    h = rms(x, ng[1]) * (1.0 + sc2) + sh2
    m, new = token_mixing(h, l, P, ctx)
    x = x + g2 * m
    h = rms(x, ng[2]) * (1.0 + sc3) + sh3
    x = x + 0.5 * g3 * swiglu(h, P['ffn_w_gu'][l, 1], P['ffn_w_d'][l, 1])
    return x, new


def setup_inputs(seed: int = 0) -> dict:
    key = jax.random.key(seed)
    ks = iter(jax.random.split(key, 64))
    G, N = S5_GROUPS, S5_STATE

    def nrm(shape, s=1.0):
        return s * jax.random.normal(next(ks), shape, F32)

    def gain(shape):
        return 1.0 + 0.01 * jax.random.normal(next(ks), shape, F32)

    return {
        'x_prompt': nrm((BATCH, SEQ, D_MODEL)),
        'x_sample': nrm((DEC_BATCH, DEC_SEQ, D_MODEL)),
        'c': nrm((DEC_BATCH, D_MODEL)),
        'cache_a_k': nrm((DEC_BATCH, DEPTH, PAST_LEN, A_KV_HEADS, A_HEAD_DIM)),
        'cache_a_v': nrm((DEC_BATCH, DEPTH, PAST_LEN, A_KV_HEADS, A_HEAD_DIM)),
        'cache_d_k': nrm((DEC_BATCH, DEPTH, PAST_LEN, D_HEADS, 2 * D_HEAD_DIM)),
        'cache_d_v': nrm((DEC_BATCH, DEPTH, PAST_LEN, D_HEADS, 2 * D_HEAD_DIM)),
        'state_ssm': nrm((DEC_BATCH, DEPTH, 2, G, N, 2), 0.1),
        'c_ctx': nrm((D_MODEL,)),
        'ada_w': nrm((DEPTH, D_MODEL, N_MOD * D_MODEL), 0.5 * D_MODEL ** -0.5),
        'ada_b': nrm((DEPTH, N_MOD * D_MODEL), 0.01),
        'norm_g': gain((DEPTH, 3, D_MODEL)),
        'ffn_w_gu': nrm((DEPTH, 2, D_MODEL, 2 * D_FF), D_MODEL ** -0.5),
        'ffn_w_d': nrm((DEPTH, 2, D_FF, D_MODEL), D_FF ** -0.5),
        'w_in': nrm((DEPTH, D_MODEL, N_IN), D_MODEL ** -0.5),
        'a_q_norm': gain((DEPTH, A_HEAD_DIM)),
        'a_k_norm': gain((DEPTH, A_HEAD_DIM)),
        'a_sink': nrm((DEPTH, A_HEADS), 0.5),
        'hy_conv_w': nrm((DEPTH, 3, 3 * HY_W), 3 ** -0.5),
        'hy_conv_b': nrm((DEPTH, 3 * HY_W), 0.01),
        'hy_w_in': nrm((DEPTH, HY_EMB, HY_HID), HY_EMB ** -0.5),
        'hy_b_in': nrm((DEPTH, HY_HID), 0.1),
        'hy_w_hid': nrm((DEPTH, HY_N_INNER, HY_HID, HY_HID), HY_HID ** -0.5),
        'hy_b_hid': nrm((DEPTH, HY_N_INNER, HY_HID), 0.1),
        'hy_w_out': nrm((DEPTH, HY_HID, 2 * HY_W), 0.05 * HY_HID ** -0.5),
        'hy_freq': gain((DEPTH, HY_HID)),
        'hy_bias': nrm((DEPTH, HY_W)),
        's5_lam_re': -0.5 + nrm((DEPTH, 2, G, N), 0.01),
        's5_lam_im': math.pi * jnp.arange(N, dtype=F32) + nrm((DEPTH, 2, G, N), 0.01),
        's5_log_step': jax.random.uniform(next(ks), (DEPTH, 2, G), F32, math.log(1e-3), math.log(1e-1)),
        's5_b_re': nrm((DEPTH, 2, G, N, S5_CH), (2 * S5_CH) ** -0.5),
        's5_b_im': nrm((DEPTH, 2, G, N, S5_CH), (2 * S5_CH) ** -0.5),
        's5_c_re': nrm((DEPTH, 2, G, S5_CH, N), N ** -0.5),
        's5_c_im': nrm((DEPTH, 2, G, S5_CH, N), N ** -0.5),
        's5_d': nrm((DEPTH, MIX_W)),
        's5_glu_w': nrm((DEPTH, MIX_W, 2 * MIX_W), MIX_W ** -0.5),
        's5_glu_b': nrm((DEPTH, 2 * MIX_W), 0.01),
        'd_q_norm': gain((DEPTH, D_HEAD_DIM)),
        'd_k_norm': gain((DEPTH, D_HEAD_DIM)),
        'd_lambda': nrm((DEPTH, 4, D_HEAD_DIM), 0.1),
        'd_subln': gain((DEPTH, 2 * D_HEAD_DIM)),
        'w_branch': nrm((DEPTH, N_BRANCH, MIX_W, D_MODEL), MIX_W ** -0.5),
        'w_out': nrm((DEPTH, D_MODEL, D_MODEL), D_MODEL ** -0.5),
    }


def reference(x_prompt, x_sample, c, cache_a_k, cache_a_v, cache_d_k, cache_d_v, state_ssm, c_ctx,
              ada_w, ada_b, norm_g, ffn_w_gu, ffn_w_d, w_in, a_q_norm, a_k_norm, a_sink,
              hy_conv_w, hy_conv_b, hy_w_in, hy_b_in, hy_w_hid, hy_b_hid, hy_w_out, hy_freq, hy_bias,
              s5_lam_re, s5_lam_im, s5_log_step, s5_b_re, s5_b_im, s5_c_re, s5_c_im, s5_d,
              s5_glu_w, s5_glu_b, d_q_norm, d_k_norm, d_lambda, d_subln, w_branch, w_out):
    P = dict(ada_w=ada_w, ada_b=ada_b, norm_g=norm_g, ffn_w_gu=ffn_w_gu, ffn_w_d=ffn_w_d, w_in=w_in,
             a_q_norm=a_q_norm, a_k_norm=a_k_norm, a_sink=a_sink,
             hy_conv_w=hy_conv_w, hy_conv_b=hy_conv_b, hy_w_in=hy_w_in, hy_b_in=hy_b_in,
             hy_w_hid=hy_w_hid, hy_b_hid=hy_b_hid, hy_w_out=hy_w_out, hy_freq=hy_freq, hy_bias=hy_bias,
             s5_lam_re=s5_lam_re, s5_lam_im=s5_lam_im, s5_log_step=s5_log_step, s5_b_re=s5_b_re,
             s5_b_im=s5_b_im, s5_c_re=s5_c_re, s5_c_im=s5_c_im, s5_d=s5_d, s5_glu_w=s5_glu_w,
             s5_glu_b=s5_glu_b, d_q_norm=d_q_norm, d_k_norm=d_k_norm, d_lambda=d_lambda,
             d_subln=d_subln, w_branch=w_branch, w_out=w_out)
    y_prompt = x_prompt
    ak, av, dk, dv, ss = [], [], [], [], []
    for l in range(DEPTH):
        y_prompt, st = layer(y_prompt, c_ctx[None, :], l, P, None)
        ak.append(st[0]); av.append(st[1]); dk.append(st[2]); dv.append(st[3]); ss.append(st[4])
    y_sample = x_sample
    for l in range(DEPTH):
        ctx = (cache_a_k[:, l], cache_a_v[:, l], cache_d_k[:, l], cache_d_v[:, l], state_ssm[:, l])
        y_sample, _ = layer(y_sample, c, l, P, ctx)
    new_cache_a_k = jnp.stack(ak, axis=1)
    new_cache_a_v = jnp.stack(av, axis=1)
    new_cache_d_k = jnp.stack(dk, axis=1)
    new_cache_d_v = jnp.stack(dv, axis=1)
    new_state_ssm = jnp.stack(ss, axis=1)
    return (y_prompt, y_sample, new_cache_a_k, new_cache_a_v, new_cache_d_k, new_cache_d_v, new_state_ssm)
```

```python
import functools
import math

import numpy as np
import jax
import jax.numpy as jnp
from jax import lax
from jax.experimental import pallas as pl
from jax.experimental.pallas import tpu as pltpu

F32 = jnp.float32
BF16 = jnp.bfloat16
HIGHEST = lax.Precision.HIGHEST

GRID_W = 64
WINDOW = 128
ROPE_BASE = 10000.0
EPS = 1e-6
NEG_INF = -1e30
N_BRANCH = 4
A_HEAD_DIM = 128
A_GROUPS = 4
HY_EMB = 33
HY_TARGET = 1e-2
HY_FAST = 0.3
HY_SLOW = 1.5
S5_CH = 16
S5_STATE = 64
D_HEADS = 8
N_MOD = 9
N_COND_ROWS = 8

V7X_VMEM_BYTES = 64 * 1024 * 1024
VMEM_LIMIT = V7X_VMEM_BYTES - 8 * 1024 * 1024
LANES = 128
SUBLANES = 8


def _cparams(n_axes):
    return pltpu.CompilerParams(dimension_semantics=("arbitrary",) * n_axes, vmem_limit_bytes=VMEM_LIMIT)


def _sigmoid(x):
    return 1.0 / (1.0 + jnp.exp(-x))


def _dot(a, b):
    return jnp.dot(a, b, preferred_element_type=F32)


def _dot_nt(a, b):
    return lax.dot_general(a, b, (((1,), (1,)), ((), ())), preferred_element_type=F32)


def _dot_hi(a, b):
    return jnp.dot(a, b, preferred_element_type=F32, precision=HIGHEST)


def _ada_kernel(c_ref, w_ref, b_ref, o_ref):
    c = c_ref[...]
    s = (c * _sigmoid(c)).astype(BF16)
    o_ref[...] = _dot(s, w_ref[...].astype(BF16)) + b_ref[...]


def _ada(cond, ada_w, ada_b, tn=512):
    depth, d, n = ada_w.shape
    return pl.pallas_call(
        _ada_kernel,
        grid=(depth, n // tn),
        in_specs=[
            pl.BlockSpec((N_COND_ROWS, d), lambda l, j: (0, 0)),
            pl.BlockSpec((None, d, tn), lambda l, j: (l, 0, j)),
            pl.BlockSpec((None, 1, tn), lambda l, j: (l, 0, j)),
        ],
        out_specs=pl.BlockSpec((None, N_COND_ROWS, tn), lambda l, j: (l, 0, j)),
        out_shape=jax.ShapeDtypeStruct((depth, N_COND_ROWS, n), F32),
        compiler_params=_cparams(2),
        name="ada_mod",
    )(cond, ada_w, ada_b.reshape(depth, 1, n))


class _Tokens:
    def __init__(self, n_ctx, l_lat, m):
        self.n_ctx, self.l_lat, self.m = n_ctx, l_lat, m

    def cond(self, i, tm):
        row = i * tm
        return jnp.where(row >= self.n_ctx, (row - self.n_ctx) // self.l_lat + 1, 0)


def _rmsmod_kernel(x_ref, g_ref, sc_ref, sh_ref, o_ref):
    x = x_ref[...]
    ms = jnp.mean(x * x, axis=-1, keepdims=True)
    y = x * lax.rsqrt(ms + EPS) * g_ref[...]
    o_ref[...] = (y * (1.0 + sc_ref[...]) + sh_ref[...]).astype(BF16)


def _rmsmod(x, gain, modl, k_sh, k_sc, tok, tm=256):
    m, d = x.shape
    return pl.pallas_call(
        _rmsmod_kernel,
        grid=(m // tm,),
        in_specs=[
            pl.BlockSpec((tm, d), lambda i: (i, 0)),
            pl.BlockSpec((1, d), lambda i: (0, 0)),
            pl.BlockSpec((None, None, 1, d), lambda i: (tok.cond(i, tm), k_sc, 0, 0)),
            pl.BlockSpec((None, None, 1, d), lambda i: (tok.cond(i, tm), k_sh, 0, 0)),
        ],
        out_specs=pl.BlockSpec((tm, d), lambda i: (i, 0)),
        out_shape=jax.ShapeDtypeStruct((m, d), BF16),
        compiler_params=_cparams(1),
        name="rmsmod",
    )(x, gain.reshape(1, d), modl, modl)


def _mm_kernel(a_ref, w_ref, o_ref):
    o_ref[...] = _dot(a_ref[...], w_ref[...]).astype(o_ref.dtype)


def _mm(a, w, tm, tn, out_dtype=F32, name="mm"):
    m, k = a.shape
    n = w.shape[1]
    return pl.pallas_call(
        _mm_kernel,
        grid=(m // tm, n // tn),
        in_specs=[pl.BlockSpec((tm, k), lambda i, j: (i, 0)), pl.BlockSpec((k, tn), lambda i, j: (0, j))],
        out_specs=pl.BlockSpec((tm, tn), lambda i, j: (i, j)),
        out_shape=jax.ShapeDtypeStruct((m, n), out_dtype),
        compiler_params=_cparams(2),
        name=name,
    )(a, w)


def _gu_kernel(h_ref, wg_ref, wu_ref, o_ref):
    h = h_ref[...]
    g = _dot(h, wg_ref[...])
    u = _dot(h, wu_ref[...])
    o_ref[...] = (g * _sigmoid(g) * u).astype(BF16)


def _gu(h, w_gu, tm, tf):
    m, d = h.shape
    ff = w_gu.shape[1] // 2
    nf = ff // tf
    return pl.pallas_call(
        _gu_kernel,
        grid=(m // tm, nf),
        in_specs=[
            pl.BlockSpec((tm, d), lambda i, j: (i, 0)),
            pl.BlockSpec((d, tf), lambda i, j: (0, j)),
            pl.BlockSpec((d, tf), lambda i, j: (0, nf + j)),
        ],
        out_specs=pl.BlockSpec((tm, tf), lambda i, j: (i, j)),
        out_shape=jax.ShapeDtypeStruct((m, ff), BF16),
        compiler_params=_cparams(2),
        name="ffn_gate_up",
    )(h, w_gu, w_gu)


def _mm_resid_kernel(a_ref, w_ref, x_ref, gate_ref, o_ref, *, coef):
    y = _dot(a_ref[...], w_ref[...])
    o_ref[...] = x_ref[...] + (coef * gate_ref[...]) * y


def _mm_resid(a, w, x, modl, k_gate, coef, tok, tm, tn, name):
    m, k = a.shape
    n = w.shape[1]
    return pl.pallas_call(
        functools.partial(_mm_resid_kernel, coef=coef),
        grid=(m // tm, n // tn),
        in_specs=[
            pl.BlockSpec((tm, k), lambda i, j: (i, 0)),
            pl.BlockSpec((k, tn), lambda i, j: (0, j)),
            pl.BlockSpec((tm, tn), lambda i, j: (i, j)),
            pl.BlockSpec((None, None, 1, tn), lambda i, j: (tok.cond(i, tm), k_gate, 0, j)),
        ],
        out_specs=pl.BlockSpec((tm, tn), lambda i, j: (i, j)),
        out_shape=jax.ShapeDtypeStruct((m, n), F32),
        compiler_params=_cparams(2),
        name=name,
    )(a, w, x, modl)


def _branch_kernel(oa_ref, ob_ref, oc_ref, od_ref, w_ref, g0_ref, g1_ref, g2_ref, g3_ref, o_ref):
    acc = None
    for b, (o_b, g_b) in enumerate(((oa_ref, g0_ref), (ob_ref, g1_ref), (oc_ref, g2_ref), (od_ref, g3_ref))):
        t = _sigmoid(g_b[...]) * _dot(o_b[...], w_ref[b])
        acc = t if acc is None else acc + t
    o_ref[...] = acc.astype(BF16)


def _branch(oa, ob, oc, od, w_branch, z, gate_col0, tm, tn):
    m, mix = oa.shape
    d = w_branch.shape[2]
    o_spec = pl.BlockSpec((tm, mix), lambda i, j: (i, 0))

    def gate_spec(b):
        return pl.BlockSpec((tm, tn), lambda i, j: (i, (gate_col0 + b * d) // tn + j))

    return pl.pallas_call(
        _branch_kernel,
        grid=(m // tm, d // tn),
        in_specs=[o_spec, o_spec, o_spec, o_spec, pl.BlockSpec((N_BRANCH, mix, tn), lambda i, j: (0, 0, j))]
        + [gate_spec(b) for b in range(N_BRANCH)],
        out_specs=pl.BlockSpec((tm, tn), lambda i, j: (i, j)),
        out_shape=jax.ShapeDtypeStruct((m, d), BF16),
        compiler_params=_cparams(2),
        name="branch_gate",
    )(oa, ob, oc, od, w_branch, z, z, z, z)


def _rope_tables(length, n, reps):
    quarter = n // 4
    n_rows = length // GRID_W
    rows = jnp.repeat(jnp.arange(n_rows), GRID_W).astype(F32)
    cols = jnp.tile(jnp.arange(GRID_W), n_rows).astype(F32)
    freqs = ROPE_BASE ** (-jnp.arange(quarter, dtype=F32) / quarter)
    ang_r = rows[:, None] * freqs[None, :]
    ang_c = cols[:, None] * freqs[None, :]
    cos = jnp.concatenate([jnp.cos(ang_r), jnp.cos(ang_r), jnp.cos(ang_c), jnp.cos(ang_c)], axis=-1)
    sin = jnp.concatenate([-jnp.sin(ang_r), jnp.sin(ang_r), -jnp.sin(ang_c), jnp.sin(ang_c)], axis=-1)
    return jnp.tile(cos, (1, reps)), jnp.tile(sin, (1, reps))


def _rope(x, cos, sin, quarter):
    lanes = x.shape[-1]
    lane = lax.broadcasted_iota(jnp.int32, (1, lanes), 1)
    first = (lane % (2 * quarter)) < quarter
    partner = jnp.where(first, pltpu.roll(x, lanes - quarter, 1), pltpu.roll(x, quarter, 1))
    return x * cos + partner * sin


def _rms_rows(x, gain):
    ms = jnp.mean(x * x, axis=-1, keepdims=True)
    return x * lax.rsqrt(ms + EPS) * gain


def _attn_a_kernel(*refs, latent, scale):
    if latent:
        q_ref, k_ref, v_ref, qn_ref, kn_ref, sink_ref, cos_ref, sin_ref, kc_ref, vc_ref, o_ref = refs
    else:
        q_ref, k_ref, v_ref, qn_ref, kn_ref, sink_ref, o_ref, ko_ref, vo_ref = refs
    length = k_ref.shape[0]
    k = _rms_rows(k_ref[...], kn_ref[...])
    v = v_ref[...]
    if latent:
        cos, sin = cos_ref[...], sin_ref[...]
        k = _rope(k, cos, sin, A_HEAD_DIM // 4)
        kc = kc_ref[...].astype(BF16)
        vc = vc_ref[...].astype(BF16)
        qi = lax.broadcasted_iota(jnp.int32, (length, length), 0)
        ki = lax.broadcasted_iota(jnp.int32, (length, length), 1)
        band = jnp.abs(qi - ki) <= WINDOW
    else:
        ko_ref[...] = k
        vo_ref[...] = v
    kb = k.astype(BF16)
    vb = v.astype(BF16)
    for g in range(A_GROUPS):
        sl = slice(g * A_HEAD_DIM, (g + 1) * A_HEAD_DIM)
        q = _rms_rows(q_ref[:, sl], qn_ref[...])
        if latent:
            q = _rope(q, cos, sin, A_HEAD_DIM // 4)
        qb = q.astype(BF16)
        sink = sink_ref[:, g * A_HEAD_DIM:g * A_HEAD_DIM + 1]
        s = _dot_nt(qb, kb) * scale
        if latent:
            s = jnp.where(band, s, NEG_INF)
            s2 = _dot_nt(qb, kc) * scale
            mx = jnp.maximum(jnp.maximum(jnp.max(s, axis=-1, keepdims=True), jnp.max(s2, axis=-1, keepdims=True)), sink)
            p2 = jnp.exp(s2 - mx)
        else:
            mx = jnp.maximum(jnp.max(s, axis=-1, keepdims=True), sink)
        p = jnp.exp(s - mx)
        den = jnp.sum(p, axis=-1, keepdims=True) + jnp.exp(sink - mx)
        o = _dot(p.astype(BF16), vb)
        if latent:
            den = den + jnp.sum(p2, axis=-1, keepdims=True)
            o = o + _dot(p2.astype(BF16), vc)
        o_ref[:, sl] = (o / den).astype(BF16)


def _attn_a(z, row_blk0, batch, length, q_col, k_col, v_col, qn, kn, sink, ctx=None):
    kv_heads = (v_col - k_col) // A_HEAD_DIM
    qw = A_GROUPS * A_HEAD_DIM
    latent = ctx is not None
    sink_b = jnp.repeat(sink.reshape(kv_heads, A_GROUPS), A_HEAD_DIM, axis=1).reshape(kv_heads, 1, qw)
    in_specs = [
        pl.BlockSpec((length, qw), lambda b, h: (row_blk0 + b, q_col // qw + h)),
        pl.BlockSpec((length, A_HEAD_DIM), lambda b, h: (row_blk0 + b, k_col // A_HEAD_DIM + h)),
        pl.BlockSpec((length, A_HEAD_DIM), lambda b, h: (row_blk0 + b, v_col // A_HEAD_DIM + h)),
        pl.BlockSpec((1, A_HEAD_DIM), lambda b, h: (0, 0)),
        pl.BlockSpec((1, A_HEAD_DIM), lambda b, h: (0, 0)),
        pl.BlockSpec((None, 1, qw), lambda b, h: (h, 0, 0)),
    ]
    args = [z, z, z, qn.reshape(1, A_HEAD_DIM), kn.reshape(1, A_HEAD_DIM), sink_b]
    o_spec = pl.BlockSpec((length, qw), lambda b, h: (b, h))
    o_shape = jax.ShapeDtypeStruct((batch * length, kv_heads * qw), BF16)
    if latent:
        k_ctx, v_ctx = ctx
        past = k_ctx.shape[1]
        cos, sin = _rope_tables(length, A_HEAD_DIM, 1)
        tab = pl.BlockSpec((length, A_HEAD_DIM), lambda b, h: (0, 0))
        cache = pl.BlockSpec((None, past, A_HEAD_DIM), lambda b, h: (b, 0, h))
        in_specs += [tab, tab, cache, cache]
        args += [cos, sin, k_ctx, v_ctx]
        out_specs, out_shape = o_spec, o_shape
    else:
        kv_spec = pl.BlockSpec((length, A_HEAD_DIM), lambda b, h: (b, h))
        kv_shape = jax.ShapeDtypeStruct((batch * length, kv_heads * A_HEAD_DIM), F32)
        out_specs, out_shape = (o_spec, kv_spec, kv_spec), (o_shape, kv_shape, kv_shape)
    return pl.pallas_call(
        functools.partial(_attn_a_kernel, latent=latent, scale=A_HEAD_DIM ** -0.5),
        grid=(batch, kv_heads),
        in_specs=in_specs,
        out_specs=out_specs,
        out_shape=out_shape,
        compiler_params=_cparams(2),
        name="attn_a_lat" if latent else "attn_a_ctx",
    )(*args)


def _rms_halves(x, gain, lo):
    half = x.shape[-1] // 2
    x2 = x * x
    s_lo = jnp.sum(jnp.where(lo, x2, 0.0), axis=-1, keepdims=True)
    s_hi = jnp.sum(jnp.where(lo, 0.0, x2), axis=-1, keepdims=True)
    ms = jnp.where(lo, s_lo, s_hi) * (1.0 / half)
    return x * lax.rsqrt(ms + EPS) * gain


def _softmax_parts(parts):
    mx = None
    for s in parts:
        m = jnp.max(s, axis=-1, keepdims=True)
        mx = m if mx is None else jnp.maximum(mx, m)
    ps = [jnp.exp(s - mx) for s in parts]
    den = None
    for p in ps:
        t = jnp.sum(p, axis=-1, keepdims=True)
        den = t if den is None else den + t
    inv = 1.0 / den
    return [p * inv for p in ps]


def _attn_d_kernel(*refs, latent, scale, lam_init):
    if latent:
        q_ref, k_ref, v_ref, qn_ref, kn_ref, lam_ref, sub_ref, cos_ref, sin_ref, kc_ref, vc_ref, o_ref = refs
    else:
        q_ref, k_ref, v_ref, qn_ref, kn_ref, lam_ref, sub_ref, o_ref, ko_ref, vo_ref = refs
    width = q_ref.shape[1]
    dh = width // 2
    lane = lax.broadcasted_iota(jnp.int32, (1, width), 1)
    lo = lane < dh
    q = _rms_halves(q_ref[...], qn_ref[...], lo)
    k = _rms_halves(k_ref[...], kn_ref[...], lo)
    v = v_ref[...]
    lv = lam_ref[...]
    lam = (jnp.exp(jnp.sum(lv[0:1] * lv[1:2], axis=-1, keepdims=True))
           - jnp.exp(jnp.sum(lv[2:3] * lv[3:4], axis=-1, keepdims=True)) + lam_init)
    if latent:
        cos, sin = cos_ref[...], sin_ref[...]
        q = _rope(q, cos, sin, dh // 4)
        k = _rope(k, cos, sin, dh // 4)
        key_parts = [k.astype(BF16), kc_ref[...].astype(BF16)]
        val_parts = [v.astype(BF16), vc_ref[...].astype(BF16)]
    else:
        ko_ref[...] = k
        vo_ref[...] = v
        key_parts = [k.astype(BF16)]
        val_parts = [v.astype(BF16)]
    q0 = jnp.where(lo, q, 0.0).astype(BF16)
    q1 = jnp.where(lo, 0.0, q).astype(BF16)
    p0 = _softmax_parts([_dot_nt(q0, kp) * scale for kp in key_parts])
    p1 = _softmax_parts([_dot_nt(q1, kp) * scale for kp in key_parts])
    o = None
    for a0, a1, vp in zip(p0, p1, val_parts):
        t = _dot((a0 - lam * a1).astype(BF16), vp)
        o = t if o is None else o + t
    o = _rms_rows(o, sub_ref[...]) * (1.0 - lam_init)
    o_ref[...] = o.astype(BF16)


def _attn_d(z, row_blk0, batch, length, q_col, k_col, v_col, qn, kn, lam_vecs, subln, lam_init, ctx=None):
    width = (k_col - q_col) // D_HEADS
    dh = width // 2
    latent = ctx is not None

    def zspec(col):
        return pl.BlockSpec((length, width), lambda b, h: (row_blk0 + b, col // width + h))

    row = pl.BlockSpec((1, width), lambda b, h: (0, 0))
    in_specs = [zspec(q_col), zspec(k_col), zspec(v_col), row, row,
                pl.BlockSpec((4, dh), lambda b, h: (0, 0)), row]
    args = [z, z, z, jnp.tile(qn, 2).reshape(1, width), jnp.tile(kn, 2).reshape(1, width), lam_vecs,
            subln.reshape(1, width)]
    o_spec = pl.BlockSpec((length, width), lambda b, h: (b, h))
    o_shape = jax.ShapeDtypeStruct((batch * length, D_HEADS * width), BF16)
    if latent:
        k_ctx, v_ctx = ctx
        past = k_ctx.shape[1]
        cos, sin = _rope_tables(length, dh, 2)
        tab = pl.BlockSpec((length, width), lambda b, h: (0, 0))
        cache = pl.BlockSpec((None, past, width), lambda b, h: (b, 0, h))
        in_specs += [tab, tab, cache, cache]
        args += [cos, sin, k_ctx, v_ctx]
        out_specs, out_shape = o_spec, o_shape
    else:
        kv_shape = jax.ShapeDtypeStruct((batch * length, D_HEADS * width), F32)
        out_specs, out_shape = (o_spec, o_spec, o_spec), (o_shape, kv_shape, kv_shape)
    return pl.pallas_call(
        functools.partial(_attn_d_kernel, latent=latent, scale=dh ** -0.5, lam_init=lam_init),
        grid=(batch, D_HEADS),
        in_specs=in_specs,
        out_specs=out_specs,
        out_shape=out_shape,
        compiler_params=_cparams(2),
        name="attn_d_lat" if latent else "attn_d_ctx",
    )(*args)


def _dft_tables(length):
    m2 = 2 * length
    k = np.arange(length)
    ang = 2.0 * np.pi * ((k[:, None] * k[None, :]) % m2) / m2
    alt = np.where(k % 2 == 0, 1.0, -1.0)
    fr = np.cos(ang)
    fi = -np.sin(ang)
    fi[0, :] = alt
    ir = (2.0 / m2) * np.cos(ang)
    ir[:, 0] = 1.0 / m2
    ii = -(2.0 / m2) * np.sin(ang)
    ii[:, 0] = alt / m2
    return tuple(jnp.asarray(t, dtype=F32) for t in (fr, fi, ir, ii))


def _hy_filter_kernel(z_ref, wi_ref, bi_ref, wh_ref, bh_ref, wf_ref, wb_ref, fr_ref, dl_ref, fre_ref, fim_ref,
                      kr_ref, ki_ref, kr2_ref):
    length = z_ref.shape[0]
    fr = fr_ref[...]
    h = jnp.sin(fr * (_dot_hi(z_ref[...], wi_ref[...]) + bi_ref[...]))
    for i in range(wh_ref.shape[0]):
        h = jnp.sin(fr * (_dot_hi(h, wh_ref[i]) + bh_ref[i]))
    row = lax.broadcasted_iota(jnp.int32, (length, 1), 0)
    t = row.astype(F32) * (1.0 / (length - 1))
    decay = jnp.exp(-t * dl_ref[...])
    hf = _dot_hi(h, wf_ref[...]) * decay
    hb = jnp.where(row == 0, 0.0, _dot_hi(h, wb_ref[...]) * decay)
    kr = _dot_hi(fre_ref[...], hf + hb)
    a1 = _dot_hi(fim_ref[...], hf)
    a2 = _dot_hi(fim_ref[...], hb)
    nyq = a1[0:1] + a2[0:1]
    kr_ref[...] = kr
    ki_ref[...] = jnp.where(row == 0, 0.0, a1 - a2)
    kr2_ref[...] = jnp.where(row == 0, nyq, kr)


def _hy_filter(length, w_in, b_in, w_hid, b_hid, w_out, freq, tables, ct=256):
    hid = w_in.shape[1]
    hw = w_out.shape[1] // 2
    n_inner = w_hid.shape[0]
    t = jnp.linspace(0.0, 1.0, length, dtype=F32)[:, None]
    bands = (HY_EMB - 1) // 2
    w = 2.0 * math.pi * jnp.arange(length, dtype=F32)[:, None] / length
    f = jnp.linspace(1e-4, bands - 1, bands, dtype=F32)[None, :]
    feat = jnp.concatenate([t, jnp.cos(f * w), -jnp.sin(f * w)], axis=-1)
    feat = jnp.pad(feat, ((0, 0), (0, LANES - HY_EMB)))
    w_in = jnp.pad(w_in, ((0, LANES - HY_EMB), (0, 0)))
    deltas = jnp.abs(jnp.linspace(math.log(HY_TARGET) / HY_FAST, math.log(HY_TARGET) / HY_SLOW, hw, dtype=F32))
    fre, fim = tables[0], tables[1]
    nj = hw // ct
    full = lambda shape: pl.BlockSpec(shape, lambda j: (0,) * len(shape))
    out_spec = pl.BlockSpec((length, ct), lambda j: (0, j))
    out_shape = jax.ShapeDtypeStruct((length, hw), F32)
    return pl.pallas_call(
        _hy_filter_kernel,
        grid=(nj,),
        in_specs=[
            full((length, LANES)), full((LANES, hid)), full((1, hid)), full((n_inner, hid, hid)),
            full((n_inner, 1, hid)),
            pl.BlockSpec((hid, ct), lambda j: (0, j)), pl.BlockSpec((hid, ct), lambda j: (0, nj + j)),
            full((1, hid)), pl.BlockSpec((1, ct), lambda j: (0, j)), full((length, length)), full((length, length)),
        ],
        out_specs=(out_spec, out_spec, out_spec),
        out_shape=(out_shape, out_shape, out_shape),
        compiler_params=_cparams(1),
        name="hyena_filter",
    )(feat, w_in, b_in.reshape(1, hid), w_hid, b_hid.reshape(n_inner, 1, hid), w_out, w_out,
      freq.reshape(1, hid), deltas.reshape(1, hw), fre, fim)


def _conv3(x, w, b, row, length):
    prev = jnp.where(row == 0, 0.0, pltpu.roll(x, 1, 0))
    nxt = jnp.where(row == length - 1, 0.0, pltpu.roll(x, length - 1, 0))
    return prev * w[0:1] + x * w[1:2] + nxt * w[2:3] + b


def _hyena_kernel(x0_ref, x1_ref, v_ref, w0_ref, w1_ref, wv_ref, b0_ref, b1_ref, bv_ref, kr_ref, ki_ref, kr2_ref,
                  bias_ref, fre_ref, fim_ref, ire_ref, iim_ref, o_ref):
    length = x0_ref.shape[0]
    row = lax.broadcasted_iota(jnp.int32, (length, 1), 0)
    x0 = _conv3(x0_ref[...], w0_ref[...], b0_ref[...], row, length)
    x1 = _conv3(x1_ref[...], w1_ref[...], b1_ref[...], row, length)
    v = _conv3(v_ref[...], wv_ref[...], bv_ref[...], row, length)
    g = v * x1
    gr = _dot_hi(fre_ref[...], g)
    gi = _dot_hi(fim_ref[...], g)
    pr = gr * kr_ref[...] - gi * ki_ref[...]
    pi = gr * ki_ref[...] + gi * kr2_ref[...]
    y = _dot_hi(ire_ref[...], pr) + _dot_hi(iim_ref[...], pi) + g * bias_ref[...]
    o_ref[...] = (y * x0).astype(BF16)


def _hyena(z, row_blk0, batch, length, col0, hw, conv_w, conv_b, filt, bias, tables, ct=256):
    nj = hw // ct

    def zspec(part):
        return pl.BlockSpec((length, ct), lambda j, b: (row_blk0 + b, (col0 + part * hw) // ct + j))

    def wspec(part, rows):
        return pl.BlockSpec((rows, ct), lambda j, b: (0, part * nj + j))

    kspec = pl.BlockSpec((length, ct), lambda j, b: (0, j))
    tab = pl.BlockSpec((length, length), lambda j, b: (0, 0))
    conv_b2 = conv_b.reshape(1, 3 * hw)
    return pl.pallas_call(
        _hyena_kernel,
        grid=(nj, batch),
        in_specs=[zspec(0), zspec(1), zspec(2), wspec(0, 3), wspec(1, 3), wspec(2, 3), wspec(0, 1), wspec(1, 1),
                  wspec(2, 1), kspec, kspec, kspec, pl.BlockSpec((1, ct), lambda j, b: (0, j)), tab, tab, tab, tab],
        out_specs=pl.BlockSpec((length, ct), lambda j, b: (b, j)),
        out_shape=jax.ShapeDtypeStruct((batch * length, hw), BF16),
        compiler_params=_cparams(2),
        name="hyena_conv",
    )(z, z, z, conv_w, conv_w, conv_w, conv_b2, conv_b2, conv_b2, filt[0], filt[1], filt[2], bias.reshape(1, hw),
      *tables)


S5_BCHUNK = 256
S5_CCHUNK = 256
S5_SCAN_COLS = 512


def _s5_discretize(lam_re, lam_im, log_step, b_re, b_im):
    dt = jnp.exp(log_step.astype(F32))[..., None]
    lr, li = lam_re.astype(F32), lam_im.astype(F32)
    mag = jnp.exp(lr * dt)
    a_re, a_im = mag * jnp.cos(li * dt), mag * jnp.sin(li * dt)
    den = lr * lr + li * li
    q_re = ((a_re - 1.0) * lr + a_im * li) / den
    q_im = (a_im * lr - (a_re - 1.0) * li) / den
    qr, qi = q_re[..., None], q_im[..., None]
    br, bi = b_re.astype(F32), b_im.astype(F32)
    return a_re, a_im, qr * br - qi * bi, qr * bi + qi * br


def _s5_weights(lam_re, lam_im, log_step, b_re, b_im, c_re, c_im):
    a_re, a_im, bb_re, bb_im = _s5_discretize(lam_re, lam_im, log_step, b_re, b_im)
    n_dir, groups, n_state, ch = bb_re.shape
    ns = groups * n_state
    a = jnp.concatenate([a_re.reshape(n_dir, 1, ns), a_im.reshape(n_dir, 1, ns)], axis=-1)
    a = jnp.broadcast_to(a, (n_dir, SUBLANES, 2 * ns))
    gb = S5_BCHUNK // ch
    nb = groups // gb
    eye_b = jnp.eye(gb, dtype=F32)

    def wb_of(bb):
        t = bb.reshape(n_dir, nb, gb, n_state, ch)
        t = jnp.einsum('dkgnc,gh->dkgchn', t, eye_b)
        return t.reshape(n_dir, nb, gb * ch, gb * n_state)

    wb = jnp.concatenate([wb_of(bb_re), wb_of(bb_im)], axis=-1).astype(BF16)
    gc = S5_CCHUNK // n_state
    nc = groups // gc
    per_blk = S5_CCHUNK // (gc * ch)
    eye_c = jnp.eye(gc, dtype=F32)
    place = jax.nn.one_hot(jnp.arange(nc) % per_blk, per_blk, dtype=F32)

    def wc_of(cc):
        t = cc.astype(F32).reshape(n_dir, nc, gc, ch, n_state)
        t = jnp.einsum('dkgcn,gh->dkgnhc', t, eye_c).reshape(n_dir, nc, gc * n_state, gc * ch)
        t = jnp.einsum('dkrc,kj->dkrjc', t, place)
        return t.reshape(n_dir, nc, gc * n_state, per_blk * gc * ch)

    wc = jnp.concatenate([wc_of(c_re), -wc_of(c_im)], axis=2).astype(BF16)
    return a, wb, wc


def _s5_kernel(u_ref, wb_ref, wc_ref, a_ref, h0_ref, y_ref, hf_ref, bu_ref, h_ref, *, tc, n_chunks):
    d = pl.program_id(0)
    c = pl.program_id(2)
    ns = h_ref.shape[1] // 2
    width = u_ref.shape[2]

    @pl.when(c == 0)
    def _():
        h_ref[...] = h0_ref[...]

    u = u_ref[...].reshape(tc * SUBLANES, width).astype(BF16)
    nb = wb_ref.shape[0]
    bcols = wb_ref.shape[2] // 2
    for kc in range(nb):
        r = _dot(u[:, kc * S5_BCHUNK:(kc + 1) * S5_BCHUNK], wb_ref[kc])
        bu_ref[:, kc * bcols:(kc + 1) * bcols] = r[:, :bcols]
        bu_ref[:, ns + kc * bcols:ns + (kc + 1) * bcols] = r[:, bcols:]

    for cc in range(ns // S5_SCAN_COLS):
        re = slice(cc * S5_SCAN_COLS, (cc + 1) * S5_SCAN_COLS)
        im = slice(ns + cc * S5_SCAN_COLS, ns + (cc + 1) * S5_SCAN_COLS)
        ar, ai = a_ref[:, re], a_ref[:, im]

        def body(t, carry, re=re, im=im, ar=ar, ai=ai):
            hr, hi = carry
            te = t + d * (tc - 1 - 2 * t)
            rows = pl.ds(pl.multiple_of(te * SUBLANES, SUBLANES), SUBLANES)
            nr = ar * hr - ai * hi + bu_ref[rows, re]
            ni = ar * hi + ai * hr + bu_ref[rows, im]
            bu_ref[rows, re] = nr
            bu_ref[rows, im] = ni
            return nr, ni

        hr, hi = lax.fori_loop(0, tc, body, (h_ref[:, re], h_ref[:, im]), unroll=4)
        h_ref[:, re] = hr
        h_ref[:, im] = hi

    nc = wc_ref.shape[0]
    ow = wc_ref.shape[2]
    per_blk = nc // (width // ow)
    for blk in range(width // ow):
        acc = None
        for kk in range(per_blk):
            k = blk * per_blk + kk
            hre = bu_ref[:, k * S5_CCHUNK:(k + 1) * S5_CCHUNK].astype(BF16)
            him = bu_ref[:, ns + k * S5_CCHUNK:ns + (k + 1) * S5_CCHUNK].astype(BF16)
            t = _dot(hre, wc_ref[k, :S5_CCHUNK, :]) + _dot(him, wc_ref[k, S5_CCHUNK:, :])
            acc = t if acc is None else acc + t
        y_ref[:, :, blk * ow:(blk + 1) * ow] = acc.reshape(tc, SUBLANES, ow)

    @pl.when(c == n_chunks - 1)
    def _():
        hf_ref[...] = h_ref[...]


def _s5_scan(u_t, a, wb, wc, h0, tc=64):
    length, bp, width = u_t.shape
    n_chunks = length // tc
    ns2 = a.shape[2]

    def tmap(d, g, c):
        return c + d * (n_chunks - 1 - 2 * c)

    return pl.pallas_call(
        functools.partial(_s5_kernel, tc=tc, n_chunks=n_chunks),
        grid=(2, bp // SUBLANES, n_chunks),
        in_specs=[
            pl.BlockSpec((tc, SUBLANES, width), lambda d, g, c: (tmap(d, g, c), g, 0)),
            pl.BlockSpec((None,) + wb.shape[1:], lambda d, g, c: (d, 0, 0, 0)),
            pl.BlockSpec((None,) + wc.shape[1:], lambda d, g, c: (d, 0, 0, 0)),
            pl.BlockSpec((None, SUBLANES, ns2), lambda d, g, c: (d, 0, 0)),
            pl.BlockSpec((None, SUBLANES, ns2), lambda d, g, c: (d, g, 0)),
        ],
        out_specs=(
            pl.BlockSpec((None, tc, SUBLANES, width), lambda d, g, c: (d, tmap(d, g, c), g, 0)),
            pl.BlockSpec((None, SUBLANES, ns2), lambda d, g, c: (d, g, 0)),
        ),
        out_shape=(jax.ShapeDtypeStruct((2, length, bp, width), F32), jax.ShapeDtypeStruct((2, bp, ns2), F32)),
        scratch_shapes=[pltpu.VMEM((tc * SUBLANES, ns2), F32), pltpu.VMEM((SUBLANES, ns2), F32)],
        compiler_params=_cparams(3),
        name="s5_scan",
    )(u_t, wb, wc, a, h0)


def _s5_glu_kernel(yf_ref, yb_ref, u_ref, d_ref, w_ref, b_ref, o_ref):
    y = u_ref[...] * d_ref[...] + yf_ref[...] + yb_ref[...]
    gy = 0.5 * y * (1.0 + jnp.tanh(math.sqrt(2.0 / math.pi) * (y + 0.044715 * (y * y * y))))
    r = _dot(gy.astype(BF16), w_ref[...]) + b_ref[...]
    half = r.shape[1] // 2
    o_ref[...] = (r[:, :half] * _sigmoid(r[:, half:])).astype(BF16)


def _s5_glu(y2, u, d, glu_w, glu_b, tm=512):
    rows, width = u.shape
    row_spec = pl.BlockSpec((tm, width), lambda i: (i, 0))
    return pl.pallas_call(
        _s5_glu_kernel,
        grid=(rows // tm,),
        in_specs=[
            pl.BlockSpec((None, tm, width), lambda i: (0, i, 0)),
            pl.BlockSpec((None, tm, width), lambda i: (1, i, 0)),
            row_spec,
            pl.BlockSpec((1, width), lambda i: (0, 0)),
            pl.BlockSpec((width, 2 * width), lambda i: (0, 0)),
            pl.BlockSpec((1, 2 * width), lambda i: (0, 0)),
        ],
        out_specs=row_spec,
        out_shape=jax.ShapeDtypeStruct((rows, width), BF16),
        compiler_params=_cparams(1),
        name="s5_glu",
    )(y2, y2, u, d.reshape(1, width), glu_w, glu_b.reshape(1, 2 * width))


def _mixer_c(su, batch, length, s5w, d, glu_w, glu_b, h0):
    a, wb, wc = s5w
    width = su.shape[1]
    bp = -(-batch // SUBLANES) * SUBLANES
    u_t = jnp.transpose(su.reshape(batch, length, width), (1, 0, 2))
    if bp != batch:
        u_t = jnp.pad(u_t, ((0, 0), (0, bp - batch), (0, 0)))
        h0 = jnp.pad(h0, ((0, 0), (0, bp - batch), (0, 0)))
    y2, hf = _s5_scan(u_t, a, wb, wc, h0)
    oc_t = _s5_glu(y2.reshape(2, length * bp, width), u_t.reshape(length * bp, width), d, glu_w, glu_b)
    oc = jnp.transpose(oc_t.reshape(length, bp, width)[:, :batch], (1, 0, 2)).reshape(batch * length, width)
    return oc, hf[:, :batch]


def kernel(x_prompt, x_sample, c, cache_a_k, cache_a_v, cache_d_k, cache_d_v, state_ssm, c_ctx, ada_w, ada_b, norm_g, ffn_w_gu, ffn_w_d, w_in, a_q_norm, a_k_norm, a_sink, hy_conv_w, hy_conv_b, hy_w_in, hy_b_in, hy_w_hid, hy_b_hid, hy_w_out, hy_freq, hy_bias, s5_lam_re, s5_lam_im, s5_log_step, s5_b_re, s5_b_im, s5_c_re, s5_c_im, s5_d, s5_glu_w, s5_glu_b, d_q_norm, d_k_norm, d_lambda, d_subln, w_branch, w_out):
    bc, lc, dm = x_prompt.shape
    bl, ll, _ = x_sample.shape
    depth = ada_w.shape[0]
    mix = dm // 4
    n_ctx, n_lat = bc * lc, bl * ll
    m = n_ctx + n_lat
    assert n_ctx % ll == 0 and bl + 1 <= N_COND_ROWS
    tok = _Tokens(n_ctx, ll, m)
    past = cache_a_k.shape[2]
    a_kv = cache_a_k.shape[3]
    groups, n_state = s5_lam_re.shape[2], s5_lam_re.shape[3]
    ns = groups * n_state
    d_width = cache_d_k.shape[4]

    splits = (mix, a_kv * A_HEAD_DIM, a_kv * A_HEAD_DIM, 3 * mix, mix, mix, mix, mix, N_BRANCH * dm)
    offs = [0]
    for s in splits:
        offs.append(offs[-1] + s)
    c_aq, c_ak, c_av, c_hy, c_su, c_dq, c_dk, c_dv, c_gt = offs[:9]

    tm = 512 if (n_ctx % 1024 or n_lat % 1024) else 1024

    x = jnp.concatenate([x_prompt.reshape(n_ctx, dm), x_sample.reshape(n_lat, dm)], axis=0)
    cond = jnp.zeros((N_COND_ROWS, dm), F32).at[0].set(c_ctx).at[1:1 + bl].set(c)
    mod = _ada(cond, ada_w, ada_b)

    w_gu_b = ffn_w_gu.astype(BF16)
    w_d_b = ffn_w_d.astype(BF16)
    w_in_b = w_in.astype(BF16)
    w_br_b = w_branch.astype(BF16)
    w_out_b = w_out.astype(BF16)
    glu_w_b = s5_glu_w.astype(BF16)

    tabs_c = _dft_tables(lc)
    tabs_l = _dft_tables(ll)
    ctx_rb = 0
    lat_rb = n_ctx // ll

    new_ak, new_av, new_dk, new_dv, new_ss = [], [], [], [], []
    for l in range(depth):
        modl = mod[l].reshape(N_COND_ROWS, N_MOD, 1, dm)

        def ffn(x, j, k0):
            h = _rmsmod(x, norm_g[l, 2 * j], modl, k0, k0 + 1, tok)
            act = _gu(h, w_gu_b[l, j], tm, 256)
            return _mm_resid(act, w_d_b[l, j], x, modl, k0 + 2, 0.5, tok, 512, 256, "ffn_down")

        x = ffn(x, 0, 0)

        h = _rmsmod(x, norm_g[l, 1], modl, 3, 4, tok)
        z = _mm(h, w_in_b[l], tm, 512, name="in_proj")

        oa_c, ak_n, av_n = _attn_a(z, ctx_rb, bc, lc, c_aq, c_ak, c_av, a_q_norm[l], a_k_norm[l], a_sink[l])
        ctx_a = (cache_a_k[:, l].reshape(bl, past, a_kv * A_HEAD_DIM), cache_a_v[:, l].reshape(bl, past, a_kv * A_HEAD_DIM))
        oa_l = _attn_a(z, lat_rb, bl, ll, c_aq, c_ak, c_av, a_q_norm[l], a_k_norm[l], a_sink[l], ctx_a)
        new_ak.append(ak_n.reshape(bc, lc, a_kv, A_HEAD_DIM))
        new_av.append(av_n.reshape(bc, lc, a_kv, A_HEAD_DIM))

        hy_args = (hy_w_in[l], hy_b_in[l], hy_w_hid[l], hy_b_hid[l], hy_w_out[l], hy_freq[l])
        filt_c = _hy_filter(lc, *hy_args, tabs_c)
        filt_l = _hy_filter(ll, *hy_args, tabs_l)
        ob_c = _hyena(z, ctx_rb, bc, lc, c_hy, mix, hy_conv_w[l], hy_conv_b[l], filt_c, hy_bias[l], tabs_c)
        ob_l = _hyena(z, lat_rb, bl, ll, c_hy, mix, hy_conv_w[l], hy_conv_b[l], filt_l, hy_bias[l], tabs_l)

        s5w = _s5_weights(s5_lam_re[l], s5_lam_im[l], s5_log_step[l], s5_b_re[l], s5_b_im[l], s5_c_re[l], s5_c_im[l])
        su = z[:, c_su:c_su + mix]
        h0_c = jnp.zeros((2, bc, 2 * ns), F32)
        st = state_ssm[:, l]
        h0_l = jnp.transpose(st, (1, 0, 4, 2, 3)).reshape(2, bl, 2 * ns)
        oc_c, hf = _mixer_c(su[:n_ctx], bc, lc, s5w, s5_d[l], glu_w_b[l], s5_glu_b[l], h0_c)
        oc_l, _ = _mixer_c(su[n_ctx:], bl, ll, s5w, s5_d[l], glu_w_b[l], s5_glu_b[l], h0_l)
        new_ss.append(jnp.transpose(hf.reshape(2, bc, 2, groups, n_state), (1, 0, 3, 4, 2)))

        lam_init = 0.8 - 0.6 * math.exp(-0.3 * l)
        d_args = (d_q_norm[l], d_k_norm[l], d_lambda[l], d_subln[l], lam_init)
        od_c, dk_n, dv_n = _attn_d(z, ctx_rb, bc, lc, c_dq, c_dk, c_dv, *d_args)
        ctx_d = (cache_d_k[:, l].reshape(bl, past, D_HEADS * d_width), cache_d_v[:, l].reshape(bl, past, D_HEADS * d_width))
        od_l = _attn_d(z, lat_rb, bl, ll, c_dq, c_dk, c_dv, *d_args, ctx_d)
        new_dk.append(dk_n.reshape(bc, lc, D_HEADS, d_width))
        new_dv.append(dv_n.reshape(bc, lc, D_HEADS, d_width))

        cat = lambda a, b: jnp.concatenate([a, b], axis=0)
        s = _branch(cat(oa_c, oa_l), cat(ob_c, ob_l), cat(oc_c, oc_l), cat(od_c, od_l), w_br_b[l], z, c_gt, 512, 512)
        x = _mm_resid(s, w_out_b[l], x, modl, 5, 1.0, tok, tm, 512, "out_proj")

        x = ffn(x, 1, 6)

    y_prompt = x[:n_ctx].reshape(bc, lc, dm)
    y_sample = x[n_ctx:].reshape(bl, ll, dm)
    return (y_prompt, y_sample, jnp.stack(new_ak, axis=1), jnp.stack(new_av, axis=1), jnp.stack(new_dk, axis=1),
            jnp.stack(new_dv, axis=1), jnp.stack(new_ss, axis=1))
```

```python
import functools
import math

import numpy as np
import jax
import jax.numpy as jnp
from jax import lax
from jax.experimental import pallas as pl
from jax.experimental.pallas import tpu as pltpu

F32 = jnp.float32
BF16 = jnp.bfloat16
HIGHEST = lax.Precision.HIGHEST

GRID_W = 64
WINDOW = 128
ROPE_BASE = 10000.0
EPS = 1e-6
NEG_INF = -1e30
N_BRANCH = 4
A_HEAD_DIM = 128
A_GROUPS = 4
HY_EMB = 33
HY_TARGET = 1e-2
HY_FAST = 0.3
HY_SLOW = 1.5
S5_CH = 16
S5_STATE = 64
D_HEADS = 8
N_MOD = 9
N_COND_ROWS = 8

V7X_VMEM_BYTES = 64 * 1024 * 1024
VMEM_LIMIT = V7X_VMEM_BYTES - 8 * 1024 * 1024
LANES = 128
SUBLANES = 8


def _cparams(n_axes):
    return pltpu.CompilerParams(dimension_semantics=("arbitrary",) * n_axes, vmem_limit_bytes=VMEM_LIMIT)


def _sigmoid(x):
    return 1.0 / (1.0 + jnp.exp(-x))


def _dot(a, b):
    return jnp.dot(a, b, preferred_element_type=F32)


def _dot_nt(a, b):
    return lax.dot_general(a, b, (((1,), (1,)), ((), ())), preferred_element_type=F32)


def _dot_hi(a, b):
    return jnp.dot(a, b, preferred_element_type=F32, precision=HIGHEST)


def _ada_kernel(c_ref, w_ref, b_ref, o_ref):
    c = c_ref[...]
    s = (c * _sigmoid(c)).astype(BF16)
    o_ref[...] = _dot(s, w_ref[...].astype(BF16)) + b_ref[...]


def _ada(cond, ada_w, ada_b, tn=512):
    depth, d, n = ada_w.shape
    return pl.pallas_call(
        _ada_kernel,
        grid=(depth, n // tn),
        in_specs=[
            pl.BlockSpec((N_COND_ROWS, d), lambda l, j: (0, 0)),
            pl.BlockSpec((None, d, tn), lambda l, j: (l, 0, j)),
            pl.BlockSpec((None, 1, tn), lambda l, j: (l, 0, j)),
        ],
        out_specs=pl.BlockSpec((None, N_COND_ROWS, tn), lambda l, j: (l, 0, j)),
        out_shape=jax.ShapeDtypeStruct((depth, N_COND_ROWS, n), F32),
        compiler_params=_cparams(2),
        name="ada_mod",
    )(cond, ada_w, ada_b.reshape(depth, 1, n))


class _Tokens:
    def __init__(self, n_ctx, l_lat, m):
        self.n_ctx, self.l_lat, self.m = n_ctx, l_lat, m

    def cond(self, i, tm):
        row = i * tm
        return jnp.where(row >= self.n_ctx, (row - self.n_ctx) // self.l_lat + 1, 0)


def _rmsmod_kernel(x_ref, g_ref, sc_ref, sh_ref, o_ref):
    x = x_ref[...]
    ms = jnp.mean(x * x, axis=-1, keepdims=True)
    y = x * lax.rsqrt(ms + EPS) * g_ref[...]
    o_ref[...] = (y * (1.0 + sc_ref[...]) + sh_ref[...]).astype(BF16)


def _rmsmod(x, gain, modl, k_sh, k_sc, tok, tm=256):
    m, d = x.shape
    return pl.pallas_call(
        _rmsmod_kernel,
        grid=(m // tm,),
        in_specs=[
            pl.BlockSpec((tm, d), lambda i: (i, 0)),
            pl.BlockSpec((1, d), lambda i: (0, 0)),
            pl.BlockSpec((None, None, 1, d), lambda i: (tok.cond(i, tm), k_sc, 0, 0)),
            pl.BlockSpec((None, None, 1, d), lambda i: (tok.cond(i, tm), k_sh, 0, 0)),
        ],
        out_specs=pl.BlockSpec((tm, d), lambda i: (i, 0)),
        out_shape=jax.ShapeDtypeStruct((m, d), BF16),
        compiler_params=_cparams(1),
        name="rmsmod",
    )(x, gain.reshape(1, d), modl, modl)


def _mm_ws_kernel(a_ref, w_ref, o_ref, wb_ref):
    @pl.when(pl.program_id(1) == 0)
    def _():
        wb_ref[...] = w_ref[...].astype(BF16)

    o_ref[...] = _dot(a_ref[...], wb_ref[...])


def _mm_ws(a, w, lead, tm, tn, name):
    m, k = a.shape
    n = w.shape[-1]
    nlead = len(lead)
    return pl.pallas_call(
        _mm_ws_kernel,
        grid=(n // tn, m // tm),
        in_specs=[
            pl.BlockSpec((tm, k), lambda j, i: (i, 0)),
            pl.BlockSpec((None,) * nlead + (k, tn), lambda j, i: lead + (0, j)),
        ],
        out_specs=pl.BlockSpec((tm, tn), lambda j, i: (i, j)),
        out_shape=jax.ShapeDtypeStruct((m, n), F32),
        scratch_shapes=[pltpu.VMEM((k, tn), BF16)],
        compiler_params=_cparams(2),
        name=name,
    )(a, w)


def _gu_kernel(h_ref, wg_ref, wu_ref, o_ref, wgb_ref, wub_ref):
    @pl.when(pl.program_id(1) == 0)
    def _():
        wgb_ref[...] = wg_ref[...].astype(BF16)
        wub_ref[...] = wu_ref[...].astype(BF16)

    h = h_ref[...]
    g = _dot(h, wgb_ref[...])
    u = _dot(h, wub_ref[...])
    o_ref[...] = (g * _sigmoid(g) * u).astype(BF16)


def _gu(h, w_gu, lead, tm, tf):
    m, d = h.shape
    ff = w_gu.shape[-1] // 2
    nf = ff // tf
    nlead = len(lead)
    wblock = (None,) * nlead + (d, tf)
    return pl.pallas_call(
        _gu_kernel,
        grid=(nf, m // tm),
        in_specs=[
            pl.BlockSpec((tm, d), lambda j, i: (i, 0)),
            pl.BlockSpec(wblock, lambda j, i: lead + (0, j)),
            pl.BlockSpec(wblock, lambda j, i: lead + (0, nf + j)),
        ],
        out_specs=pl.BlockSpec((tm, tf), lambda j, i: (i, j)),
        out_shape=jax.ShapeDtypeStruct((m, ff), BF16),
        scratch_shapes=[pltpu.VMEM((d, tf), BF16), pltpu.VMEM((d, tf), BF16)],
        compiler_params=_cparams(2),
        name="ffn_gate_up",
    )(h, w_gu, w_gu)


def _mm_resid_kernel(a_ref, w_ref, x_ref, gate_ref, o_ref, *, coef):
    y = _dot(a_ref[...], w_ref[...])
    o_ref[...] = x_ref[...] + (coef * gate_ref[...]) * y


def _mm_resid(a, w, lead, x, modl, k_gate, coef, tok, tm, tn, name):
    m, k = a.shape
    n = w.shape[-1]
    nlead = len(lead)
    return pl.pallas_call(
        functools.partial(_mm_resid_kernel, coef=coef),
        grid=(m // tm, n // tn),
        in_specs=[
            pl.BlockSpec((tm, k), lambda i, j: (i, 0)),
            pl.BlockSpec((None,) * nlead + (k, tn), lambda i, j: lead + (0, j)),
            pl.BlockSpec((tm, tn), lambda i, j: (i, j)),
            pl.BlockSpec((None, None, 1, tn), lambda i, j: (tok.cond(i, tm), k_gate, 0, j)),
        ],
        out_specs=pl.BlockSpec((tm, tn), lambda i, j: (i, j)),
        out_shape=jax.ShapeDtypeStruct((m, n), F32),
        compiler_params=_cparams(2),
        name=name,
    )(a, w, x, modl)


def _branch_kernel(oa_ref, ob_ref, oc_ref, od_ref, w_ref, g0_ref, g1_ref, g2_ref, g3_ref, o_ref):
    acc = None
    for b, (o_b, g_b) in enumerate(((oa_ref, g0_ref), (ob_ref, g1_ref), (oc_ref, g2_ref), (od_ref, g3_ref))):
        t = _sigmoid(g_b[...]) * _dot(o_b[...], w_ref[b])
        acc = t if acc is None else acc + t
    o_ref[...] = acc.astype(BF16)


def _branch(oa, ob, oc, od, w_branch, layer, z, gate_col0, tm, tn):
    m, mix = oa.shape
    d = w_branch.shape[-1]
    o_spec = pl.BlockSpec((tm, mix), lambda i, j: (i, 0))

    def gate_spec(b):
        return pl.BlockSpec((tm, tn), lambda i, j: (i, (gate_col0 + b * d) // tn + j))

    return pl.pallas_call(
        _branch_kernel,
        grid=(m // tm, d // tn),
        in_specs=[o_spec, o_spec, o_spec, o_spec,
                  pl.BlockSpec((None, N_BRANCH, mix, tn), lambda i, j: (layer, 0, 0, j))]
        + [gate_spec(b) for b in range(N_BRANCH)],
        out_specs=pl.BlockSpec((tm, tn), lambda i, j: (i, j)),
        out_shape=jax.ShapeDtypeStruct((m, d), BF16),
        compiler_params=_cparams(2),
        name="branch_gate",
    )(oa, ob, oc, od, w_branch, z, z, z, z)


def _rope_tables(length, n, reps):
    quarter = n // 4
    n_rows = length // GRID_W
    rows = jnp.repeat(jnp.arange(n_rows), GRID_W).astype(F32)
    cols = jnp.tile(jnp.arange(GRID_W), n_rows).astype(F32)
    freqs = ROPE_BASE ** (-jnp.arange(quarter, dtype=F32) / quarter)
    ang_r = rows[:, None] * freqs[None, :]
    ang_c = cols[:, None] * freqs[None, :]
    cos = jnp.concatenate([jnp.cos(ang_r), jnp.cos(ang_r), jnp.cos(ang_c), jnp.cos(ang_c)], axis=-1)
    sin = jnp.concatenate([-jnp.sin(ang_r), jnp.sin(ang_r), -jnp.sin(ang_c), jnp.sin(ang_c)], axis=-1)
    return jnp.tile(cos, (1, reps)), jnp.tile(sin, (1, reps))


def _rope(x, cos, sin, quarter):
    lanes = x.shape[-1]
    lane = lax.broadcasted_iota(jnp.int32, (1, lanes), 1)
    first = (lane % (2 * quarter)) < quarter
    partner = jnp.where(first, pltpu.roll(x, lanes - quarter, 1), pltpu.roll(x, quarter, 1))
    return x * cos + partner * sin


def _rms_rows(x, gain):
    ms = jnp.mean(x * x, axis=-1, keepdims=True)
    return x * lax.rsqrt(ms + EPS) * gain


def _attn_a_kernel(*refs, latent, scale):
    if latent:
        q_ref, k_ref, v_ref, qn_ref, kn_ref, sink_ref, cos_ref, sin_ref, kc_ref, vc_ref, o_ref = refs
    else:
        q_ref, k_ref, v_ref, qn_ref, kn_ref, sink_ref, o_ref, ko_ref, vo_ref = refs
    length = k_ref.shape[0]
    k = _rms_rows(k_ref[...], kn_ref[...])
    v = v_ref[...]
    if latent:
        cos, sin = cos_ref[...], sin_ref[...]
        k = _rope(k, cos, sin, A_HEAD_DIM // 4)
        kc = kc_ref[...].astype(BF16)
        vc = vc_ref[...].astype(BF16)
        qi = lax.broadcasted_iota(jnp.int32, (length, length), 0)
        ki = lax.broadcasted_iota(jnp.int32, (length, length), 1)
        band = jnp.abs(qi - ki) <= WINDOW
    else:
        ko_ref[...] = k
        vo_ref[...] = v
    kb = k.astype(BF16)
    vb = v.astype(BF16)
    for g in range(A_GROUPS):
        sl = slice(g * A_HEAD_DIM, (g + 1) * A_HEAD_DIM)
        q = _rms_rows(q_ref[:, sl], qn_ref[...])
        if latent:
            q = _rope(q, cos, sin, A_HEAD_DIM // 4)
        qb = q.astype(BF16)
        sink = sink_ref[:, g * A_HEAD_DIM:g * A_HEAD_DIM + 1]
        s = _dot_nt(qb, kb) * scale
        if latent:
            s = jnp.where(band, s, NEG_INF)
            s2 = _dot_nt(qb, kc) * scale
            mx = jnp.maximum(jnp.maximum(jnp.max(s, axis=-1, keepdims=True), jnp.max(s2, axis=-1, keepdims=True)), sink)
            p2 = jnp.exp(s2 - mx)
        else:
            mx = jnp.maximum(jnp.max(s, axis=-1, keepdims=True), sink)
        p = jnp.exp(s - mx)
        den = jnp.sum(p, axis=-1, keepdims=True) + jnp.exp(sink - mx)
        o = _dot(p.astype(BF16), vb)
        if latent:
            den = den + jnp.sum(p2, axis=-1, keepdims=True)
            o = o + _dot(p2.astype(BF16), vc)
        o_ref[:, sl] = (o / den).astype(BF16)


def _attn_a(z, row_blk0, batch, length, q_col, k_col, v_col, qn, kn, sink, ctx=None):
    kv_heads = (v_col - k_col) // A_HEAD_DIM
    qw = A_GROUPS * A_HEAD_DIM
    latent = ctx is not None
    sink_b = jnp.repeat(sink.reshape(kv_heads, A_GROUPS), A_HEAD_DIM, axis=1).reshape(kv_heads, 1, qw)
    in_specs = [
        pl.BlockSpec((length, qw), lambda b, h: (row_blk0 + b, q_col // qw + h)),
        pl.BlockSpec((length, A_HEAD_DIM), lambda b, h: (row_blk0 + b, k_col // A_HEAD_DIM + h)),
        pl.BlockSpec((length, A_HEAD_DIM), lambda b, h: (row_blk0 + b, v_col // A_HEAD_DIM + h)),
        pl.BlockSpec((1, A_HEAD_DIM), lambda b, h: (0, 0)),
        pl.BlockSpec((1, A_HEAD_DIM), lambda b, h: (0, 0)),
        pl.BlockSpec((None, 1, qw), lambda b, h: (h, 0, 0)),
    ]
    args = [z, z, z, qn.reshape(1, A_HEAD_DIM), kn.reshape(1, A_HEAD_DIM), sink_b]
    o_spec = pl.BlockSpec((length, qw), lambda b, h: (b, h))
    o_shape = jax.ShapeDtypeStruct((batch * length, kv_heads * qw), BF16)
    if latent:
        k_ctx, v_ctx = ctx
        past = k_ctx.shape[1]
        cos, sin = _rope_tables(length, A_HEAD_DIM, 1)
        tab = pl.BlockSpec((length, A_HEAD_DIM), lambda b, h: (0, 0))
        cache = pl.BlockSpec((None, past, A_HEAD_DIM), lambda b, h: (b, 0, h))
        in_specs += [tab, tab, cache, cache]
        args += [cos, sin, k_ctx, v_ctx]
        out_specs, out_shape = o_spec, o_shape
    else:
        kv_spec = pl.BlockSpec((length, A_HEAD_DIM), lambda b, h: (b, h))
        kv_shape = jax.ShapeDtypeStruct((batch * length, kv_heads * A_HEAD_DIM), F32)
        out_specs, out_shape = (o_spec, kv_spec, kv_spec), (o_shape, kv_shape, kv_shape)
    return pl.pallas_call(
        functools.partial(_attn_a_kernel, latent=latent, scale=A_HEAD_DIM ** -0.5),
        grid=(batch, kv_heads),
        in_specs=in_specs,
        out_specs=out_specs,
        out_shape=out_shape,
        compiler_params=_cparams(2),
        name="attn_a_lat" if latent else "attn_a_ctx",
    )(*args)


def _rms_halves(x, gain, lo):
    half = x.shape[-1] // 2
    x2 = x * x
    s_lo = jnp.sum(jnp.where(lo, x2, 0.0), axis=-1, keepdims=True)
    s_hi = jnp.sum(jnp.where(lo, 0.0, x2), axis=-1, keepdims=True)
    ms = jnp.where(lo, s_lo, s_hi) * (1.0 / half)
    return x * lax.rsqrt(ms + EPS) * gain


def _softmax_parts(parts):
    mx = None
    for s in parts:
        m = jnp.max(s, axis=-1, keepdims=True)
        mx = m if mx is None else jnp.maximum(mx, m)
    ps = [jnp.exp(s - mx) for s in parts]
    den = None
    for p in ps:
        t = jnp.sum(p, axis=-1, keepdims=True)
        den = t if den is None else den + t
    inv = 1.0 / den
    return [p * inv for p in ps]


def _attn_d_kernel(*refs, latent, scale, lam_init):
    if latent:
        q_ref, k_ref, v_ref, qn_ref, kn_ref, lam_ref, sub_ref, cos_ref, sin_ref, kc_ref, vc_ref, o_ref = refs
    else:
        q_ref, k_ref, v_ref, qn_ref, kn_ref, lam_ref, sub_ref, o_ref, ko_ref, vo_ref = refs
    width = q_ref.shape[1]
    dh = width // 2
    lane = lax.broadcasted_iota(jnp.int32, (1, width), 1)
    lo = lane < dh
    q = _rms_halves(q_ref[...], qn_ref[...], lo)
    k = _rms_halves(k_ref[...], kn_ref[...], lo)
    v = v_ref[...]
    lv = lam_ref[...]
    lam = (jnp.exp(jnp.sum(lv[0:1] * lv[1:2], axis=-1, keepdims=True))
           - jnp.exp(jnp.sum(lv[2:3] * lv[3:4], axis=-1, keepdims=True)) + lam_init)
    if latent:
        cos, sin = cos_ref[...], sin_ref[...]
        q = _rope(q, cos, sin, dh // 4)
        k = _rope(k, cos, sin, dh // 4)
        key_parts = [k.astype(BF16), kc_ref[...].astype(BF16)]
        val_parts = [v.astype(BF16), vc_ref[...].astype(BF16)]
    else:
        ko_ref[...] = k
        vo_ref[...] = v
        key_parts = [k.astype(BF16)]
        val_parts = [v.astype(BF16)]
    q0 = jnp.where(lo, q, 0.0).astype(BF16)
    q1 = jnp.where(lo, 0.0, q).astype(BF16)
    p0 = _softmax_parts([_dot_nt(q0, kp) * scale for kp in key_parts])
    p1 = _softmax_parts([_dot_nt(q1, kp) * scale for kp in key_parts])
    o = None
    for a0, a1, vp in zip(p0, p1, val_parts):
        t = _dot((a0 - lam * a1).astype(BF16), vp)
        o = t if o is None else o + t
    o = _rms_rows(o, sub_ref[...]) * (1.0 - lam_init)
    o_ref[...] = o.astype(BF16)


def _attn_d(z, row_blk0, batch, length, q_col, k_col, v_col, qn, kn, lam_vecs, subln, lam_init, ctx=None):
    width = (k_col - q_col) // D_HEADS
    dh = width // 2
    latent = ctx is not None

    def zspec(col):
        return pl.BlockSpec((length, width), lambda b, h: (row_blk0 + b, col // width + h))

    row = pl.BlockSpec((1, width), lambda b, h: (0, 0))
    in_specs = [zspec(q_col), zspec(k_col), zspec(v_col), row, row,
                pl.BlockSpec((4, dh), lambda b, h: (0, 0)), row]
    args = [z, z, z, jnp.tile(qn, 2).reshape(1, width), jnp.tile(kn, 2).reshape(1, width), lam_vecs,
            subln.reshape(1, width)]
    o_spec = pl.BlockSpec((length, width), lambda b, h: (b, h))
    o_shape = jax.ShapeDtypeStruct((batch * length, D_HEADS * width), BF16)
    if latent:
        k_ctx, v_ctx = ctx
        past = k_ctx.shape[1]
        cos, sin = _rope_tables(length, dh, 2)
        tab = pl.BlockSpec((length, width), lambda b, h: (0, 0))
        cache = pl.BlockSpec((None, past, width), lambda b, h: (b, 0, h))
        in_specs += [tab, tab, cache, cache]
        args += [cos, sin, k_ctx, v_ctx]
        out_specs, out_shape = o_spec, o_shape
    else:
        kv_shape = jax.ShapeDtypeStruct((batch * length, D_HEADS * width), F32)
        out_specs, out_shape = (o_spec, o_spec, o_spec), (o_shape, kv_shape, kv_shape)
    return pl.pallas_call(
        functools.partial(_attn_d_kernel, latent=latent, scale=dh ** -0.5, lam_init=lam_init),
        grid=(batch, D_HEADS),
        in_specs=in_specs,
        out_specs=out_specs,
        out_shape=out_shape,
        compiler_params=_cparams(2),
        name="attn_d_lat" if latent else "attn_d_ctx",
    )(*args)


def _dft_tables(length):
    m2 = 2 * length
    k = np.arange(length)
    ang = 2.0 * np.pi * ((k[:, None] * k[None, :]) % m2) / m2
    alt = np.where(k % 2 == 0, 1.0, -1.0)
    fr = np.cos(ang)
    fi = -np.sin(ang)
    fi[0, :] = alt
    ir = (2.0 / m2) * np.cos(ang)
    ir[:, 0] = 1.0 / m2
    ii = -(2.0 / m2) * np.sin(ang)
    ii[:, 0] = alt / m2
    exact = tuple(jnp.asarray(t, dtype=F32) for t in (fr, fi))
    rounded = tuple(jnp.asarray(t, dtype=F32).astype(BF16) for t in (fr, fi, ir, ii))
    return exact, rounded


def _hy_filter_kernel(z_ref, wi_ref, bi_ref, wh_ref, bh_ref, wf_ref, wb_ref, fr_ref, dl_ref, fre_ref, fim_ref,
                      kr_ref, ki_ref, kr2_ref):
    length = z_ref.shape[0]
    fr = fr_ref[...]
    h = jnp.sin(fr * (_dot_hi(z_ref[...], wi_ref[...]) + bi_ref[...]))
    for i in range(wh_ref.shape[0]):
        h = jnp.sin(fr * (_dot_hi(h, wh_ref[i]) + bh_ref[i]))
    row = lax.broadcasted_iota(jnp.int32, (length, 1), 0)
    t = row.astype(F32) * (1.0 / (length - 1))
    decay = jnp.exp(-t * dl_ref[...])
    hf = _dot_hi(h, wf_ref[...]) * decay
    hb = jnp.where(row == 0, 0.0, _dot_hi(h, wb_ref[...]) * decay)
    kr = _dot_hi(fre_ref[...], hf + hb)
    a1 = _dot_hi(fim_ref[...], hf)
    a2 = _dot_hi(fim_ref[...], hb)
    nyq = a1[0:1] + a2[0:1]
    kr_ref[...] = kr
    ki_ref[...] = jnp.where(row == 0, 0.0, a1 - a2)
    kr2_ref[...] = jnp.where(row == 0, nyq, kr)


def _hy_filter(length, w_in, b_in, w_hid, b_hid, w_out, freq, tables, ct=256):
    hid = w_in.shape[1]
    hw = w_out.shape[1] // 2
    n_inner = w_hid.shape[0]
    t = jnp.linspace(0.0, 1.0, length, dtype=F32)[:, None]
    bands = (HY_EMB - 1) // 2
    w = 2.0 * math.pi * jnp.arange(length, dtype=F32)[:, None] / length
    f = jnp.linspace(1e-4, bands - 1, bands, dtype=F32)[None, :]
    feat = jnp.concatenate([t, jnp.cos(f * w), -jnp.sin(f * w)], axis=-1)
    feat = jnp.pad(feat, ((0, 0), (0, LANES - HY_EMB)))
    w_in = jnp.pad(w_in, ((0, LANES - HY_EMB), (0, 0)))
    deltas = jnp.abs(jnp.linspace(math.log(HY_TARGET) / HY_FAST, math.log(HY_TARGET) / HY_SLOW, hw, dtype=F32))
    fre, fim = tables[0], tables[1]
    nj = hw // ct
    full = lambda shape: pl.BlockSpec(shape, lambda j: (0,) * len(shape))
    out_spec = pl.BlockSpec((length, ct), lambda j: (0, j))
    out_shape = jax.ShapeDtypeStruct((length, hw), F32)
    return pl.pallas_call(
        _hy_filter_kernel,
        grid=(nj,),
        in_specs=[
            full((length, LANES)), full((LANES, hid)), full((1, hid)), full((n_inner, hid, hid)),
            full((n_inner, 1, hid)),
            pl.BlockSpec((hid, ct), lambda j: (0, j)), pl.BlockSpec((hid, ct), lambda j: (0, nj + j)),
            full((1, hid)), pl.BlockSpec((1, ct), lambda j: (0, j)), full((length, length)), full((length, length)),
        ],
        out_specs=(out_spec, out_spec, out_spec),
        out_shape=(out_shape, out_shape, out_shape),
        compiler_params=_cparams(1),
        name="hyena_filter",
    )(feat, w_in, b_in.reshape(1, hid), w_hid, b_hid.reshape(n_inner, 1, hid), w_out, w_out,
      freq.reshape(1, hid), deltas.reshape(1, hw), fre, fim)


def _conv3(x, w, b, row, length):
    prev = jnp.where(row == 0, 0.0, pltpu.roll(x, 1, 0))
    nxt = jnp.where(row == length - 1, 0.0, pltpu.roll(x, length - 1, 0))
    return prev * w[0:1] + x * w[1:2] + nxt * w[2:3] + b


def _hyena_kernel(x0_ref, x1_ref, v_ref, w0_ref, w1_ref, wv_ref, b0_ref, b1_ref, bv_ref, kr_ref, ki_ref, kr2_ref,
                  bias_ref, fre_ref, fim_ref, ire_ref, iim_ref, o_ref):
    length = x0_ref.shape[0]
    row = lax.broadcasted_iota(jnp.int32, (length, 1), 0)
    x0 = _conv3(x0_ref[...], w0_ref[...], b0_ref[...], row, length)
    x1 = _conv3(x1_ref[...], w1_ref[...], b1_ref[...], row, length)
    v = _conv3(v_ref[...], wv_ref[...], bv_ref[...], row, length)
    g = v * x1
    gb = g.astype(BF16)
    gr = _dot(fre_ref[...], gb)
    gi = _dot(fim_ref[...], gb)
    pr = gr * kr_ref[...] - gi * ki_ref[...]
    pi = gr * ki_ref[...] + gi * kr2_ref[...]
    y = _dot(ire_ref[...], pr.astype(BF16)) + _dot(iim_ref[...], pi.astype(BF16)) + g * bias_ref[...]
    o_ref[...] = (y * x0).astype(BF16)


def _hyena(z, row_blk0, batch, length, col0, hw, conv_w, conv_b, filt, bias, tables, ct=256):
    nj = hw // ct

    def zspec(part):
        return pl.BlockSpec((length, ct), lambda j, b: (row_blk0 + b, (col0 + part * hw) // ct + j))

    def wspec(part, rows):
        return pl.BlockSpec((rows, ct), lambda j, b: (0, part * nj + j))

    kspec = pl.BlockSpec((length, ct), lambda j, b: (0, j))
    tab = pl.BlockSpec((length, length), lambda j, b: (0, 0))
    conv_b2 = conv_b.reshape(1, 3 * hw)
    return pl.pallas_call(
        _hyena_kernel,
        grid=(nj, batch),
        in_specs=[zspec(0), zspec(1), zspec(2), wspec(0, 3), wspec(1, 3), wspec(2, 3), wspec(0, 1), wspec(1, 1),
                  wspec(2, 1), kspec, kspec, kspec, pl.BlockSpec((1, ct), lambda j, b: (0, j)), tab, tab, tab, tab],
        out_specs=pl.BlockSpec((length, ct), lambda j, b: (b, j)),
        out_shape=jax.ShapeDtypeStruct((batch * length, hw), BF16),
        compiler_params=_cparams(2),
        name="hyena_conv",
    )(z, z, z, conv_w, conv_w, conv_w, conv_b2, conv_b2, conv_b2, filt[0], filt[1], filt[2], bias.reshape(1, hw),
      *tables)


S5_BCHUNK = 256
S5_CCHUNK = 256
S5_SCAN_COLS = 1024


def _s5_discretize(lam_re, lam_im, log_step, b_re, b_im):
    dt = jnp.exp(log_step.astype(F32))[..., None]
    lr, li = lam_re.astype(F32), lam_im.astype(F32)
    mag = jnp.exp(lr * dt)
    a_re, a_im = mag * jnp.cos(li * dt), mag * jnp.sin(li * dt)
    den = lr * lr + li * li
    q_re = ((a_re - 1.0) * lr + a_im * li) / den
    q_im = (a_im * lr - (a_re - 1.0) * li) / den
    qr, qi = q_re[..., None], q_im[..., None]
    br, bi = b_re.astype(F32), b_im.astype(F32)
    return a_re, a_im, qr * br - qi * bi, qr * bi + qi * br


def _s5_weights(lam_re, lam_im, log_step, b_re, b_im, c_re, c_im):
    a_re, a_im, bb_re, bb_im = _s5_discretize(lam_re, lam_im, log_step, b_re, b_im)
    n_dir, groups, n_state, ch = bb_re.shape
    ns = groups * n_state
    a = jnp.concatenate([a_re.reshape(n_dir, 1, ns), a_im.reshape(n_dir, 1, ns)], axis=-1)
    a = jnp.broadcast_to(a, (n_dir, SUBLANES, 2 * ns))
    gb = S5_BCHUNK // ch
    nb = groups // gb
    eye_b = jnp.eye(gb, dtype=F32)

    def wb_of(bb):
        t = bb.reshape(n_dir, nb, gb, n_state, ch)
        t = jnp.einsum('dkgnc,gh->dkgchn', t, eye_b)
        return t.reshape(n_dir, nb, gb * ch, gb * n_state)

    wb = jnp.concatenate([wb_of(bb_re), wb_of(bb_im)], axis=-1).astype(BF16)
    gc = S5_CCHUNK // n_state
    nc = groups // gc
    per_blk = S5_CCHUNK // (gc * ch)
    eye_c = jnp.eye(gc, dtype=F32)
    place = jax.nn.one_hot(jnp.arange(nc) % per_blk, per_blk, dtype=F32)

    def wc_of(cc):
        t = cc.astype(F32).reshape(n_dir, nc, gc, ch, n_state)
        t = jnp.einsum('dkgcn,gh->dkgnhc', t, eye_c).reshape(n_dir, nc, gc * n_state, gc * ch)
        t = jnp.einsum('dkrc,kj->dkrjc', t, place)
        return t.reshape(n_dir, nc, gc * n_state, per_blk * gc * ch)

    wc = jnp.concatenate([wc_of(c_re), -wc_of(c_im)], axis=2).astype(BF16)
    return a, wb, wc


def _s5_kernel(u_ref, wb_ref, wc_ref, a_ref, h0_ref, y_ref, hf_ref, bu_ref, h_ref, *, tc, n_chunks):
    d = pl.program_id(0)
    c = pl.program_id(2)
    ns = h_ref.shape[1] // 2
    width = u_ref.shape[2]

    @pl.when(c == 0)
    def _():
        h_ref[...] = h0_ref[...]

    u = u_ref[...].reshape(tc * SUBLANES, width).astype(BF16)
    nb = wb_ref.shape[0]
    bcols = wb_ref.shape[2] // 2
    for kc in range(nb):
        r = _dot(u[:, kc * S5_BCHUNK:(kc + 1) * S5_BCHUNK], wb_ref[kc])
        bu_ref[:, kc * bcols:(kc + 1) * bcols] = r[:, :bcols]
        bu_ref[:, ns + kc * bcols:ns + (kc + 1) * bcols] = r[:, bcols:]

    for cc in range(ns // S5_SCAN_COLS):
        re = slice(cc * S5_SCAN_COLS, (cc + 1) * S5_SCAN_COLS)
        im = slice(ns + cc * S5_SCAN_COLS, ns + (cc + 1) * S5_SCAN_COLS)
        ar, ai = a_ref[:, re], a_ref[:, im]

        def body(t, carry, re=re, im=im, ar=ar, ai=ai):
            hr, hi = carry
            te = t + d * (tc - 1 - 2 * t)
            rows = pl.ds(pl.multiple_of(te * SUBLANES, SUBLANES), SUBLANES)
            nr = ar * hr - ai * hi + bu_ref[rows, re]
            ni = ar * hi + ai * hr + bu_ref[rows, im]
            bu_ref[rows, re] = nr
            bu_ref[rows, im] = ni
            return nr, ni

        hr, hi = lax.fori_loop(0, tc, body, (h_ref[:, re], h_ref[:, im]), unroll=4)
        h_ref[:, re] = hr
        h_ref[:, im] = hi

    nc = wc_ref.shape[0]
    ow = wc_ref.shape[2]
    per_blk = nc // (width // ow)
    for blk in range(width // ow):
        acc = None
        for kk in range(per_blk):
            k = blk * per_blk + kk
            hre = bu_ref[:, k * S5_CCHUNK:(k + 1) * S5_CCHUNK].astype(BF16)
            him = bu_ref[:, ns + k * S5_CCHUNK:ns + (k + 1) * S5_CCHUNK].astype(BF16)
            t = _dot(hre, wc_ref[k, :S5_CCHUNK, :]) + _dot(him, wc_ref[k, S5_CCHUNK:, :])
            acc = t if acc is None else acc + t
        y_ref[:, :, blk * ow:(blk + 1) * ow] = acc.reshape(tc, SUBLANES, ow)

    @pl.when(c == n_chunks - 1)
    def _():
        hf_ref[...] = h_ref[...]


def _s5_scan(u_t, a, wb, wc, h0, tc=64):
    length, bp, width = u_t.shape
    n_chunks = length // tc
    ns2 = a.shape[2]

    def tmap(d, g, c):
        return c + d * (n_chunks - 1 - 2 * c)

    return pl.pallas_call(
        functools.partial(_s5_kernel, tc=tc, n_chunks=n_chunks),
        grid=(2, bp // SUBLANES, n_chunks),
        in_specs=[
            pl.BlockSpec((tc, SUBLANES, width), lambda d, g, c: (tmap(d, g, c), g, 0)),
            pl.BlockSpec((None,) + wb.shape[1:], lambda d, g, c: (d, 0, 0, 0)),
            pl.BlockSpec((None,) + wc.shape[1:], lambda d, g, c: (d, 0, 0, 0)),
            pl.BlockSpec((None, SUBLANES, ns2), lambda d, g, c: (d, 0, 0)),
            pl.BlockSpec((None, SUBLANES, ns2), lambda d, g, c: (d, g, 0)),
        ],
        out_specs=(
            pl.BlockSpec((None, tc, SUBLANES, width), lambda d, g, c: (d, tmap(d, g, c), g, 0)),
            pl.BlockSpec((None, SUBLANES, ns2), lambda d, g, c: (d, g, 0)),
        ),
        out_shape=(jax.ShapeDtypeStruct((2, length, bp, width), F32), jax.ShapeDtypeStruct((2, bp, ns2), F32)),
        scratch_shapes=[pltpu.VMEM((tc * SUBLANES, ns2), F32), pltpu.VMEM((SUBLANES, ns2), F32)],
        compiler_params=_cparams(3),
        name="s5_scan",
    )(u_t, wb, wc, a, h0)


def _s5_glu_kernel(yf_ref, yb_ref, u_ref, d_ref, w_ref, b_ref, o_ref):
    y = u_ref[...] * d_ref[...] + yf_ref[...] + yb_ref[...]
    gy = 0.5 * y * (1.0 + jnp.tanh(math.sqrt(2.0 / math.pi) * (y + 0.044715 * (y * y * y))))
    r = _dot(gy.astype(BF16), w_ref[...]) + b_ref[...]
    half = r.shape[1] // 2
    o_ref[...] = (r[:, :half] * _sigmoid(r[:, half:])).astype(BF16)


def _s5_glu(y2, u, d, glu_w, glu_b, tm=512):
    rows, width = u.shape
    row_spec = pl.BlockSpec((tm, width), lambda i: (i, 0))
    return pl.pallas_call(
        _s5_glu_kernel,
        grid=(rows // tm,),
        in_specs=[
            pl.BlockSpec((None, tm, width), lambda i: (0, i, 0)),
            pl.BlockSpec((None, tm, width), lambda i: (1, i, 0)),
            row_spec,
            pl.BlockSpec((1, width), lambda i: (0, 0)),
            pl.BlockSpec((width, 2 * width), lambda i: (0, 0)),
            pl.BlockSpec((1, 2 * width), lambda i: (0, 0)),
        ],
        out_specs=row_spec,
        out_shape=jax.ShapeDtypeStruct((rows, width), BF16),
        compiler_params=_cparams(1),
        name="s5_glu",
    )(y2, y2, u, d.reshape(1, width), glu_w, glu_b.reshape(1, 2 * width))


def _mixer_c(su, batch, length, s5w, d, glu_w, glu_b, h0):
    a, wb, wc = s5w
    width = su.shape[1]
    bp = -(-batch // SUBLANES) * SUBLANES
    u_t = jnp.transpose(su.reshape(batch, length, width), (1, 0, 2))
    if bp != batch:
        u_t = jnp.pad(u_t, ((0, 0), (0, bp - batch), (0, 0)))
        h0 = jnp.pad(h0, ((0, 0), (0, bp - batch), (0, 0)))
    y2, hf = _s5_scan(u_t, a, wb, wc, h0)
    oc_t = _s5_glu(y2.reshape(2, length * bp, width), u_t.reshape(length * bp, width), d, glu_w, glu_b)
    oc = jnp.transpose(oc_t.reshape(length, bp, width)[:, :batch], (1, 0, 2)).reshape(batch * length, width)
    return oc, hf[:, :batch]


def kernel(x_prompt, x_sample, c, cache_a_k, cache_a_v, cache_d_k, cache_d_v, state_ssm, c_ctx, ada_w, ada_b, norm_g, ffn_w_gu, ffn_w_d, w_in, a_q_norm, a_k_norm, a_sink, hy_conv_w, hy_conv_b, hy_w_in, hy_b_in, hy_w_hid, hy_b_hid, hy_w_out, hy_freq, hy_bias, s5_lam_re, s5_lam_im, s5_log_step, s5_b_re, s5_b_im, s5_c_re, s5_c_im, s5_d, s5_glu_w, s5_glu_b, d_q_norm, d_k_norm, d_lambda, d_subln, w_branch, w_out):
    bc, lc, dm = x_prompt.shape
    bl, ll, _ = x_sample.shape
    depth = ada_w.shape[0]
    mix = dm // 4
    n_ctx, n_lat = bc * lc, bl * ll
    m = n_ctx + n_lat
    assert n_ctx % ll == 0 and bl + 1 <= N_COND_ROWS
    tok = _Tokens(n_ctx, ll, m)
    past = cache_a_k.shape[2]
    a_kv = cache_a_k.shape[3]
    groups, n_state = s5_lam_re.shape[2], s5_lam_re.shape[3]
    ns = groups * n_state
    d_width = cache_d_k.shape[4]

    splits = (mix, a_kv * A_HEAD_DIM, a_kv * A_HEAD_DIM, 3 * mix, mix, mix, mix, mix, N_BRANCH * dm)
    offs = [0]
    for s in splits:
        offs.append(offs[-1] + s)
    c_aq, c_ak, c_av, c_hy, c_su, c_dq, c_dk, c_dv, c_gt = offs[:9]

    tm = 512 if (n_ctx % 1024 or n_lat % 1024) else 1024

    x = jnp.concatenate([x_prompt.reshape(n_ctx, dm), x_sample.reshape(n_lat, dm)], axis=0)
    cond = jnp.zeros((N_COND_ROWS, dm), F32).at[0].set(c_ctx).at[1:1 + bl].set(c)
    mod = _ada(cond, ada_w, ada_b)

    w_d_b = ffn_w_d.astype(BF16)
    w_br_b = w_branch.astype(BF16)
    w_out_b = w_out.astype(BF16)
    glu_w_b = s5_glu_w.astype(BF16)

    tabs_c, tabs_cb = _dft_tables(lc)
    tabs_l, tabs_lb = _dft_tables(ll)
    ctx_rb = 0
    lat_rb = n_ctx // ll

    new_ak, new_av, new_dk, new_dv, new_ss = [], [], [], [], []
    for l in range(depth):
        modl = mod[l].reshape(N_COND_ROWS, N_MOD, 1, dm)

        def ffn(x, j, k0):
            h = _rmsmod(x, norm_g[l, 2 * j], modl, k0, k0 + 1, tok)
            act = _gu(h, ffn_w_gu, (l, j), tm, 256)
            return _mm_resid(act, w_d_b, (l, j), x, modl, k0 + 2, 0.5, tok, 512, 256, "ffn_down")

        x = ffn(x, 0, 0)

        h = _rmsmod(x, norm_g[l, 1], modl, 3, 4, tok)
        z = _mm_ws(h, w_in, (l,), tm, 512, "in_proj")

        oa_c, ak_n, av_n = _attn_a(z, ctx_rb, bc, lc, c_aq, c_ak, c_av, a_q_norm[l], a_k_norm[l], a_sink[l])
        ctx_a = (cache_a_k[:, l].reshape(bl, past, a_kv * A_HEAD_DIM), cache_a_v[:, l].reshape(bl, past, a_kv * A_HEAD_DIM))
        oa_l = _attn_a(z, lat_rb, bl, ll, c_aq, c_ak, c_av, a_q_norm[l], a_k_norm[l], a_sink[l], ctx_a)
        new_ak.append(ak_n.reshape(bc, lc, a_kv, A_HEAD_DIM))
        new_av.append(av_n.reshape(bc, lc, a_kv, A_HEAD_DIM))

        hy_args = (hy_w_in[l], hy_b_in[l], hy_w_hid[l], hy_b_hid[l], hy_w_out[l], hy_freq[l])
        filt_c = _hy_filter(lc, *hy_args, tabs_c)
        filt_l = _hy_filter(ll, *hy_args, tabs_l)
        ob_c = _hyena(z, ctx_rb, bc, lc, c_hy, mix, hy_conv_w[l], hy_conv_b[l], filt_c, hy_bias[l], tabs_cb)
        ob_l = _hyena(z, lat_rb, bl, ll, c_hy, mix, hy_conv_w[l], hy_conv_b[l], filt_l, hy_bias[l], tabs_lb)

        s5w = _s5_weights(s5_lam_re[l], s5_lam_im[l], s5_log_step[l], s5_b_re[l], s5_b_im[l], s5_c_re[l], s5_c_im[l])
        su = z[:, c_su:c_su + mix]
        h0_c = jnp.zeros((2, bc, 2 * ns), F32)
        st = state_ssm[:, l]
        h0_l = jnp.transpose(st, (1, 0, 4, 2, 3)).reshape(2, bl, 2 * ns)
        oc_c, hf = _mixer_c(su[:n_ctx], bc, lc, s5w, s5_d[l], glu_w_b[l], s5_glu_b[l], h0_c)
        oc_l, _ = _mixer_c(su[n_ctx:], bl, ll, s5w, s5_d[l], glu_w_b[l], s5_glu_b[l], h0_l)
        new_ss.append(jnp.transpose(hf.reshape(2, bc, 2, groups, n_state), (1, 0, 3, 4, 2)))

        lam_init = 0.8 - 0.6 * math.exp(-0.3 * l)
        d_args = (d_q_norm[l], d_k_norm[l], d_lambda[l], d_subln[l], lam_init)
        od_c, dk_n, dv_n = _attn_d(z, ctx_rb, bc, lc, c_dq, c_dk, c_dv, *d_args)
        ctx_d = (cache_d_k[:, l].reshape(bl, past, D_HEADS * d_width), cache_d_v[:, l].reshape(bl, past, D_HEADS * d_width))
        od_l = _attn_d(z, lat_rb, bl, ll, c_dq, c_dk, c_dv, *d_args, ctx_d)
        new_dk.append(dk_n.reshape(bc, lc, D_HEADS, d_width))
        new_dv.append(dv_n.reshape(bc, lc, D_HEADS, d_width))

        cat = lambda a, b: jnp.concatenate([a, b], axis=0)
        s = _branch(cat(oa_c, oa_l), cat(ob_c, ob_l), cat(oc_c, oc_l), cat(od_c, od_l), w_br_b, l, z, c_gt, 512, 512)
        x = _mm_resid(s, w_out_b, (l,), x, modl, 5, 1.0, tok, tm, 512, "out_proj")

        x = ffn(x, 1, 6)

    y_prompt = x[:n_ctx].reshape(bc, lc, dm)
    y_sample = x[n_ctx:].reshape(bl, ll, dm)
    return (y_prompt, y_sample, jnp.stack(new_ak, axis=1), jnp.stack(new_av, axis=1), jnp.stack(new_dk, axis=1),
            jnp.stack(new_dv, axis=1), jnp.stack(new_ss, axis=1))
```

```python
import functools
import math

import numpy as np
import jax
import jax.numpy as jnp
from jax import lax
from jax.experimental import pallas as pl
from jax.experimental.pallas import tpu as pltpu

F32 = jnp.float32
BF16 = jnp.bfloat16
HIGHEST = lax.Precision.HIGHEST

GRID_W = 64
WINDOW = 128
ROPE_BASE = 10000.0
EPS = 1e-6
NEG_INF = -1e30
N_BRANCH = 4
A_HEAD_DIM = 128
A_GROUPS = 4
HY_EMB = 33
HY_TARGET = 1e-2
HY_FAST = 0.3
HY_SLOW = 1.5
S5_CH = 16
S5_STATE = 64
D_HEADS = 8
N_MOD = 9
N_COND_ROWS = 8

V7X_VMEM_BYTES = 64 * 1024 * 1024
VMEM_LIMIT = V7X_VMEM_BYTES - 8 * 1024 * 1024
LANES = 128
SUBLANES = 8


def _cparams(n_axes):
    return pltpu.CompilerParams(dimension_semantics=("arbitrary",) * n_axes, vmem_limit_bytes=VMEM_LIMIT)


def _sigmoid(x):
    return 1.0 / (1.0 + jnp.exp(-x))


def _dot(a, b):
    return jnp.dot(a, b, preferred_element_type=F32)


def _dot_nt(a, b):
    return lax.dot_general(a, b, (((1,), (1,)), ((), ())), preferred_element_type=F32)


def _dot_hi(a, b):
    return jnp.dot(a, b, preferred_element_type=F32, precision=HIGHEST)


def _ada_kernel(c_ref, w_ref, b_ref, o_ref):
    c = c_ref[...]
    s = (c * _sigmoid(c)).astype(BF16)
    o_ref[...] = _dot(s, w_ref[...].astype(BF16)) + b_ref[...]


def _ada(cond, ada_w, ada_b, tn=512):
    depth, d, n = ada_w.shape
    return pl.pallas_call(
        _ada_kernel,
        grid=(depth, n // tn),
        in_specs=[
            pl.BlockSpec((N_COND_ROWS, d), lambda l, j: (0, 0)),
            pl.BlockSpec((None, d, tn), lambda l, j: (l, 0, j)),
            pl.BlockSpec((None, 1, tn), lambda l, j: (l, 0, j)),
        ],
        out_specs=pl.BlockSpec((None, N_COND_ROWS, tn), lambda l, j: (l, 0, j)),
        out_shape=jax.ShapeDtypeStruct((depth, N_COND_ROWS, n), F32),
        compiler_params=_cparams(2),
        name="ada_mod",
    )(cond, ada_w, ada_b.reshape(depth, 1, n))


class _Tokens:
    def __init__(self, n_ctx, l_lat, m):
        self.n_ctx, self.l_lat, self.m = n_ctx, l_lat, m

    def cond(self, i, tm):
        row = i * tm
        return jnp.where(row >= self.n_ctx, (row - self.n_ctx) // self.l_lat + 1, 0)


def _rmsmod_kernel(x_ref, g_ref, sc_ref, sh_ref, o_ref):
    x = x_ref[...]
    ms = jnp.mean(x * x, axis=-1, keepdims=True)
    y = x * lax.rsqrt(ms + EPS) * g_ref[...]
    o_ref[...] = (y * (1.0 + sc_ref[...]) + sh_ref[...]).astype(BF16)


def _rmsmod(x, gain, modl, k_sh, k_sc, tok, tm=256):
    m, d = x.shape
    return pl.pallas_call(
        _rmsmod_kernel,
        grid=(m // tm,),
        in_specs=[
            pl.BlockSpec((tm, d), lambda i: (i, 0)),
            pl.BlockSpec((1, d), lambda i: (0, 0)),
            pl.BlockSpec((None, None, 1, d), lambda i: (tok.cond(i, tm), k_sc, 0, 0)),
            pl.BlockSpec((None, None, 1, d), lambda i: (tok.cond(i, tm), k_sh, 0, 0)),
        ],
        out_specs=pl.BlockSpec((tm, d), lambda i: (i, 0)),
        out_shape=jax.ShapeDtypeStruct((m, d), BF16),
        compiler_params=_cparams(1),
        name="rmsmod",
    )(x, gain.reshape(1, d), modl, modl)


def _mm_ws_kernel(a_ref, w_ref, o_ref, wb_ref):
    @pl.when(pl.program_id(1) == 0)
    def _():
        wb_ref[...] = w_ref[...].astype(BF16)

    o_ref[...] = _dot(a_ref[...], wb_ref[...]).astype(o_ref.dtype)


def _mm_ws(a, w, lead, cols, out_dtype, tm, tn, name):
    m, k = a.shape
    col0, n = cols
    cb = col0 // tn
    assert col0 % tn == 0 and n % tn == 0
    nlead = len(lead)
    return pl.pallas_call(
        _mm_ws_kernel,
        grid=(n // tn, m // tm),
        in_specs=[
            pl.BlockSpec((tm, k), lambda j, i: (i, 0)),
            pl.BlockSpec((None,) * nlead + (k, tn), lambda j, i: lead + (0, cb + j)),
        ],
        out_specs=pl.BlockSpec((tm, tn), lambda j, i: (i, j)),
        out_shape=jax.ShapeDtypeStruct((m, n), out_dtype),
        scratch_shapes=[pltpu.VMEM((k, tn), BF16)],
        compiler_params=_cparams(2),
        name=name,
    )(a, w)


def _gu_kernel(h_ref, wg_ref, wu_ref, wd_ref, o_ref, wdb_ref, wgb_ref, wub_ref):
    @pl.when(pl.program_id(1) == 0)
    def _():
        wgb_ref[...] = wg_ref[...].astype(BF16)
        wub_ref[...] = wu_ref[...].astype(BF16)
        wdb_ref[...] = wd_ref[...].astype(BF16)

    h = h_ref[...]
    g = _dot(h, wgb_ref[...])
    u = _dot(h, wub_ref[...])
    o_ref[...] = (g * _sigmoid(g) * u).astype(BF16)


def _gu(h, w_gu, w_d, lead, tm, tf):
    m, d = h.shape
    ff = w_gu.shape[-1] // 2
    nf = ff // tf
    nlead = len(lead)
    wblock = (None,) * nlead + (d, tf)
    return pl.pallas_call(
        _gu_kernel,
        grid=(nf, m // tm),
        in_specs=[
            pl.BlockSpec((tm, d), lambda j, i: (i, 0)),
            pl.BlockSpec(wblock, lambda j, i: lead + (0, j)),
            pl.BlockSpec(wblock, lambda j, i: lead + (0, nf + j)),
            pl.BlockSpec((None,) * nlead + (tf, d), lambda j, i: lead + (j, 0)),
        ],
        out_specs=(pl.BlockSpec((tm, tf), lambda j, i: (i, j)), pl.BlockSpec((tf, d), lambda j, i: (j, 0))),
        out_shape=(jax.ShapeDtypeStruct((m, ff), BF16), jax.ShapeDtypeStruct((ff, d), BF16)),
        scratch_shapes=[pltpu.VMEM((d, tf), BF16), pltpu.VMEM((d, tf), BF16)],
        compiler_params=_cparams(2),
        name="ffn_gate_up",
    )(h, w_gu, w_gu, w_d)


def _mm_resid_kernel(a_ref, w_ref, x_ref, gate_ref, o_ref, *, coef):
    y = _dot(a_ref[...], w_ref[...])
    o_ref[...] = x_ref[...] + (coef * gate_ref[...]) * y


def _mm_resid(a, w, lead, x, modl, k_gate, coef, tok, tm, tn, name, rows=None):
    m, k = a.shape
    n = w.shape[-1]
    nlead = len(lead)
    row0, m_out = (0, m) if rows is None else rows
    rb = row0 // tm
    return pl.pallas_call(
        functools.partial(_mm_resid_kernel, coef=coef),
        grid=(m_out // tm, n // tn),
        in_specs=[
            pl.BlockSpec((tm, k), lambda i, j: (rb + i, 0)),
            pl.BlockSpec((None,) * nlead + (k, tn), lambda i, j: lead + (0, j)),
            pl.BlockSpec((tm, tn), lambda i, j: (rb + i, j)),
            pl.BlockSpec((None, None, 1, tn), lambda i, j: (tok.cond(rb + i, tm), k_gate, 0, j)),
        ],
        out_specs=pl.BlockSpec((tm, tn), lambda i, j: (i, j)),
        out_shape=jax.ShapeDtypeStruct((m_out, n), F32),
        compiler_params=_cparams(2),
        name=name,
    )(a, w, x, modl)


def _branch_kernel(oa_ref, ob_ref, oc_ref, od_ref, w_ref, g0_ref, g1_ref, g2_ref, g3_ref, o_ref):
    acc = None
    for b, (o_b, g_b) in enumerate(((oa_ref, g0_ref), (ob_ref, g1_ref), (oc_ref, g2_ref), (od_ref, g3_ref))):
        t = _sigmoid(g_b[...].astype(F32)) * _dot(o_b[...], w_ref[b])
        acc = t if acc is None else acc + t
    o_ref[...] = acc.astype(BF16)


def _branch(oa, ob, oc, od, w_branch, layer, zg, tm, tn):
    m, mix = oa.shape
    d = w_branch.shape[-1]
    o_spec = pl.BlockSpec((tm, mix), lambda i, j: (i, 0))

    def gate_spec(b):
        return pl.BlockSpec((tm, tn), lambda i, j: (i, (b * d) // tn + j))

    return pl.pallas_call(
        _branch_kernel,
        grid=(m // tm, d // tn),
        in_specs=[o_spec, o_spec, o_spec, o_spec,
                  pl.BlockSpec((None, N_BRANCH, mix, tn), lambda i, j: (layer, 0, 0, j))]
        + [gate_spec(b) for b in range(N_BRANCH)],
        out_specs=pl.BlockSpec((tm, tn), lambda i, j: (i, j)),
        out_shape=jax.ShapeDtypeStruct((m, d), BF16),
        compiler_params=_cparams(2),
        name="branch_gate",
    )(oa, ob, oc, od, w_branch, zg, zg, zg, zg)


def _rope_tables(length, n, reps):
    quarter = n // 4
    n_rows = length // GRID_W
    rows = jnp.repeat(jnp.arange(n_rows), GRID_W).astype(F32)
    cols = jnp.tile(jnp.arange(GRID_W), n_rows).astype(F32)
    freqs = ROPE_BASE ** (-jnp.arange(quarter, dtype=F32) / quarter)
    ang_r = rows[:, None] * freqs[None, :]
    ang_c = cols[:, None] * freqs[None, :]
    cos = jnp.concatenate([jnp.cos(ang_r), jnp.cos(ang_r), jnp.cos(ang_c), jnp.cos(ang_c)], axis=-1)
    sin = jnp.concatenate([-jnp.sin(ang_r), jnp.sin(ang_r), -jnp.sin(ang_c), jnp.sin(ang_c)], axis=-1)
    return jnp.tile(cos, (1, reps)), jnp.tile(sin, (1, reps))


def _rope(x, cos, sin, quarter):
    lanes = x.shape[-1]
    lane = lax.broadcasted_iota(jnp.int32, (1, lanes), 1)
    first = (lane % (2 * quarter)) < quarter
    partner = jnp.where(first, pltpu.roll(x, lanes - quarter, 1), pltpu.roll(x, quarter, 1))
    return x * cos + partner * sin


def _rms_rows(x, gain):
    ms = jnp.mean(x * x, axis=-1, keepdims=True)
    return x * lax.rsqrt(ms + EPS) * gain


def _attn_a_kernel(*refs, latent, scale):
    if latent:
        q_ref, k_ref, v_ref, qn_ref, kn_ref, sink_ref, cos_ref, sin_ref, kc_ref, vc_ref, o_ref = refs
    else:
        q_ref, k_ref, v_ref, qn_ref, kn_ref, sink_ref, o_ref, ko_ref, vo_ref = refs
    length = k_ref.shape[0]
    k = _rms_rows(k_ref[...], kn_ref[...])
    v = v_ref[...]
    if latent:
        cos, sin = cos_ref[...], sin_ref[...]
        k = _rope(k, cos, sin, A_HEAD_DIM // 4)
        kc = kc_ref[...].astype(BF16)
        vc = vc_ref[...].astype(BF16)
        qi = lax.broadcasted_iota(jnp.int32, (length, length), 0)
        ki = lax.broadcasted_iota(jnp.int32, (length, length), 1)
        band = jnp.abs(qi - ki) <= WINDOW
    else:
        ko_ref[...] = k
        vo_ref[...] = v
    kb = k.astype(BF16)
    vb = v.astype(BF16)
    for g in range(A_GROUPS):
        sl = slice(g * A_HEAD_DIM, (g + 1) * A_HEAD_DIM)
        q = _rms_rows(q_ref[:, sl], qn_ref[...])
        if latent:
            q = _rope(q, cos, sin, A_HEAD_DIM // 4)
        qb = (q * scale).astype(BF16)
        sink = sink_ref[:, g * A_HEAD_DIM:g * A_HEAD_DIM + 1]
        s = _dot_nt(qb, kb)
        if latent:
            s = jnp.where(band, s, NEG_INF)
            s2 = _dot_nt(qb, kc)
            mx = jnp.maximum(jnp.maximum(jnp.max(s, axis=-1, keepdims=True), jnp.max(s2, axis=-1, keepdims=True)), sink)
            p2 = jnp.exp(s2 - mx)
        else:
            mx = jnp.maximum(jnp.max(s, axis=-1, keepdims=True), sink)
        p = jnp.exp(s - mx)
        den = jnp.sum(p, axis=-1, keepdims=True) + jnp.exp(sink - mx)
        o = _dot(p.astype(BF16), vb)
        if latent:
            den = den + jnp.sum(p2, axis=-1, keepdims=True)
            o = o + _dot(p2.astype(BF16), vc)
        o_ref[:, sl] = (o / den).astype(BF16)


def _attn_a(z, row_blk0, batch, length, q_col, k_col, v_col, qn, kn, sink, ctx=None):
    kv_heads = (v_col - k_col) // A_HEAD_DIM
    qw = A_GROUPS * A_HEAD_DIM
    latent = ctx is not None
    sink_b = jnp.repeat(sink.reshape(kv_heads, A_GROUPS), A_HEAD_DIM, axis=1).reshape(kv_heads, 1, qw)
    in_specs = [
        pl.BlockSpec((length, qw), lambda b, h: (row_blk0 + b, q_col // qw + h)),
        pl.BlockSpec((length, A_HEAD_DIM), lambda b, h: (row_blk0 + b, k_col // A_HEAD_DIM + h)),
        pl.BlockSpec((length, A_HEAD_DIM), lambda b, h: (row_blk0 + b, v_col // A_HEAD_DIM + h)),
        pl.BlockSpec((1, A_HEAD_DIM), lambda b, h: (0, 0)),
        pl.BlockSpec((1, A_HEAD_DIM), lambda b, h: (0, 0)),
        pl.BlockSpec((None, 1, qw), lambda b, h: (h, 0, 0)),
    ]
    args = [z, z, z, qn.reshape(1, A_HEAD_DIM), kn.reshape(1, A_HEAD_DIM), sink_b]
    o_spec = pl.BlockSpec((length, qw), lambda b, h: (b, h))
    o_shape = jax.ShapeDtypeStruct((batch * length, kv_heads * qw), BF16)
    if latent:
        k_ctx, v_ctx = ctx
        past = k_ctx.shape[1]
        cos, sin = _rope_tables(length, A_HEAD_DIM, 1)
        tab = pl.BlockSpec((length, A_HEAD_DIM), lambda b, h: (0, 0))
        cache = pl.BlockSpec((None, past, A_HEAD_DIM), lambda b, h: (b, 0, h))
        in_specs += [tab, tab, cache, cache]
        args += [cos, sin, k_ctx, v_ctx]
        out_specs, out_shape = o_spec, o_shape
    else:
        kv_spec = pl.BlockSpec((length, A_HEAD_DIM), lambda b, h: (b, h))
        kv_shape = jax.ShapeDtypeStruct((batch * length, kv_heads * A_HEAD_DIM), F32)
        out_specs, out_shape = (o_spec, kv_spec, kv_spec), (o_shape, kv_shape, kv_shape)
    return pl.pallas_call(
        functools.partial(_attn_a_kernel, latent=latent, scale=A_HEAD_DIM ** -0.5),
        grid=(batch, kv_heads),
        in_specs=in_specs,
        out_specs=out_specs,
        out_shape=out_shape,
        compiler_params=_cparams(2),
        name="attn_a_lat" if latent else "attn_a_ctx",
    )(*args)


def _rms_halves(x, gain, lo):
    half = x.shape[-1] // 2
    x2 = x * x
    s_lo = jnp.sum(jnp.where(lo, x2, 0.0), axis=-1, keepdims=True)
    s_hi = jnp.sum(jnp.where(lo, 0.0, x2), axis=-1, keepdims=True)
    ms = jnp.where(lo, s_lo, s_hi) * (1.0 / half)
    return x * lax.rsqrt(ms + EPS) * gain


def _attn_d_kernel(*refs, latent, heads, scale, lam_init):
    if latent:
        q_ref, k_ref, v_ref, qn_ref, kn_ref, lam_ref, sub_ref, cos_ref, sin_ref, kc_ref, vc_ref, o_ref = refs
    else:
        q_ref, k_ref, v_ref, qn_ref, kn_ref, lam_ref, sub_ref, o_ref, ko_ref, vo_ref = refs
    width = qn_ref.shape[1]
    dh = width // 2
    lane = lax.broadcasted_iota(jnp.int32, (1, width), 1)
    lo = lane < dh
    lv = lam_ref[...]
    lam = (jnp.exp(jnp.sum(lv[0:1] * lv[1:2], axis=-1, keepdims=True))
           - jnp.exp(jnp.sum(lv[2:3] * lv[3:4], axis=-1, keepdims=True)) + lam_init)
    for hh in range(heads):
        sl = slice(hh * width, (hh + 1) * width)
        q = _rms_halves(q_ref[:, sl], qn_ref[...], lo)
        k = _rms_halves(k_ref[:, sl], kn_ref[...], lo)
        v = v_ref[:, sl]
        if latent:
            cos, sin = cos_ref[...], sin_ref[...]
            q = _rope(q, cos, sin, dh // 4)
            k = _rope(k, cos, sin, dh // 4)
            key_parts = [k.astype(BF16), kc_ref[:, sl].astype(BF16)]
            val_parts = [v.astype(BF16), vc_ref[:, sl].astype(BF16)]
        else:
            ko_ref[:, sl] = k
            vo_ref[:, sl] = v
            key_parts = [k.astype(BF16)]
            val_parts = [v.astype(BF16)]
        q = q * scale
        o = None
        for comp in range(2):
            qc = jnp.where(lo, q, 0.0) if comp == 0 else jnp.where(lo, 0.0, q)
            qc = qc.astype(BF16)
            scores = [_dot_nt(qc, kp) for kp in key_parts]
            mx = None
            for s in scores:
                m = jnp.max(s, axis=-1, keepdims=True)
                mx = m if mx is None else jnp.maximum(mx, m)
            acc = den = None
            for s, vp in zip(scores, val_parts):
                p = jnp.exp(s - mx)
                d = jnp.sum(p, axis=-1, keepdims=True)
                t = _dot(p.astype(BF16), vp)
                den = d if den is None else den + d
                acc = t if acc is None else acc + t
            oc = acc * (1.0 / den)
            o = oc if comp == 0 else o - lam * oc
        o = _rms_rows(o, sub_ref[...]) * (1.0 - lam_init)
        o_ref[:, sl] = o.astype(BF16)


def _attn_d(z, row_blk0, batch, length, q_col, k_col, v_col, qn, kn, lam_vecs, subln, lam_init, ctx=None):
    width = (k_col - q_col) // D_HEADS
    dh = width // 2
    latent = ctx is not None
    heads = 1 if latent else 4
    bw = heads * width
    steps = D_HEADS // heads
    assert q_col % bw == 0 and k_col % bw == 0 and v_col % bw == 0

    def zspec(col):
        return pl.BlockSpec((length, bw), lambda b, h: (row_blk0 + b, col // bw + h))

    row = pl.BlockSpec((1, width), lambda b, h: (0, 0))
    in_specs = [zspec(q_col), zspec(k_col), zspec(v_col), row, row,
                pl.BlockSpec((4, dh), lambda b, h: (0, 0)), row]
    args = [z, z, z, jnp.tile(qn, 2).reshape(1, width), jnp.tile(kn, 2).reshape(1, width), lam_vecs,
            subln.reshape(1, width)]
    o_spec = pl.BlockSpec((length, bw), lambda b, h: (b, h))
    o_shape = jax.ShapeDtypeStruct((batch * length, D_HEADS * width), BF16)
    if latent:
        k_ctx, v_ctx = ctx
        past = k_ctx.shape[1]
        cos, sin = _rope_tables(length, dh, 2)
        tab = pl.BlockSpec((length, width), lambda b, h: (0, 0))
        cache = pl.BlockSpec((None, past, bw), lambda b, h: (b, 0, h))
        in_specs += [tab, tab, cache, cache]
        args += [cos, sin, k_ctx, v_ctx]
        out_specs, out_shape = o_spec, o_shape
    else:
        kv_shape = jax.ShapeDtypeStruct((batch * length, D_HEADS * width), F32)
        out_specs, out_shape = (o_spec, o_spec, o_spec), (o_shape, kv_shape, kv_shape)
    return pl.pallas_call(
        functools.partial(_attn_d_kernel, latent=latent, heads=heads, scale=dh ** -0.5, lam_init=lam_init),
        grid=(batch, steps),
        in_specs=in_specs,
        out_specs=out_specs,
        out_shape=out_shape,
        compiler_params=_cparams(2),
        name="attn_d_lat" if latent else "attn_d_ctx",
    )(*args)


def _dft_tables(length):
    m2 = 2 * length
    k = np.arange(length)
    ang = 2.0 * np.pi * ((k[:, None] * k[None, :]) % m2) / m2
    alt = np.where(k % 2 == 0, 1.0, -1.0)
    fr = np.cos(ang)
    fi = -np.sin(ang)
    fi[0, :] = alt
    ir = (2.0 / m2) * np.cos(ang)
    ir[:, 0] = 1.0 / m2
    ii = -(2.0 / m2) * np.sin(ang)
    ii[:, 0] = alt / m2
    exact = tuple(jnp.asarray(t, dtype=F32) for t in (fr, fi))
    rounded = tuple(jnp.asarray(t, dtype=F32).astype(BF16) for t in (fr, fi, ir, ii))
    return exact, rounded


def _hy_filter_kernel(z_ref, wi_ref, bi_ref, wh_ref, bh_ref, wf_ref, wb_ref, fr_ref, dl_ref, fre_ref, fim_ref,
                      kr_ref, ki_ref, kr2_ref):
    length = z_ref.shape[0]
    fr = fr_ref[...]
    h = jnp.sin(fr * (_dot_hi(z_ref[...], wi_ref[...]) + bi_ref[...]))
    for i in range(wh_ref.shape[0]):
        h = jnp.sin(fr * (_dot_hi(h, wh_ref[i]) + bh_ref[i]))
    row = lax.broadcasted_iota(jnp.int32, (length, 1), 0)
    t = row.astype(F32) * (1.0 / (length - 1))
    decay = jnp.exp(-t * dl_ref[...])
    hf = _dot_hi(h, wf_ref[...]) * decay
    hb = jnp.where(row == 0, 0.0, _dot_hi(h, wb_ref[...]) * decay)
    hs = hf + hb
    kr = _dot_hi(fre_ref[...], hs)
    ki = _dot_hi(fim_ref[...], hf - hb)
    alt = 1.0 - 2.0 * (row % 2).astype(F32)
    nyq = jnp.sum(alt * hs, axis=0, keepdims=True)
    kr_ref[...] = kr
    ki_ref[...] = jnp.where(row == 0, 0.0, ki)
    kr2_ref[...] = jnp.where(row == 0, nyq, kr)


def _hy_filter(length, w_in, b_in, w_hid, b_hid, w_out, freq, tables, ct=256):
    hid = w_in.shape[1]
    hw = w_out.shape[1] // 2
    n_inner = w_hid.shape[0]
    t = jnp.linspace(0.0, 1.0, length, dtype=F32)[:, None]
    bands = (HY_EMB - 1) // 2
    w = 2.0 * math.pi * jnp.arange(length, dtype=F32)[:, None] / length
    f = jnp.linspace(1e-4, bands - 1, bands, dtype=F32)[None, :]
    feat = jnp.concatenate([t, jnp.cos(f * w), -jnp.sin(f * w)], axis=-1)
    feat = jnp.pad(feat, ((0, 0), (0, LANES - HY_EMB)))
    w_in = jnp.pad(w_in, ((0, LANES - HY_EMB), (0, 0)))
    deltas = jnp.abs(jnp.linspace(math.log(HY_TARGET) / HY_FAST, math.log(HY_TARGET) / HY_SLOW, hw, dtype=F32))
    fre, fim = tables[0], tables[1]
    nj = hw // ct
    full = lambda shape: pl.BlockSpec(shape, lambda j: (0,) * len(shape))
    out_spec = pl.BlockSpec((length, ct), lambda j: (0, j))
    out_shape = jax.ShapeDtypeStruct((length, hw), F32)
    return pl.pallas_call(
        _hy_filter_kernel,
        grid=(nj,),
        in_specs=[
            full((length, LANES)), full((LANES, hid)), full((1, hid)), full((n_inner, hid, hid)),
            full((n_inner, 1, hid)),
            pl.BlockSpec((hid, ct), lambda j: (0, j)), pl.BlockSpec((hid, ct), lambda j: (0, nj + j)),
            full((1, hid)), pl.BlockSpec((1, ct), lambda j: (0, j)), full((length, length)), full((length, length)),
        ],
        out_specs=(out_spec, out_spec, out_spec),
        out_shape=(out_shape, out_shape, out_shape),
        compiler_params=_cparams(1),
        name="hyena_filter",
    )(feat, w_in, b_in.reshape(1, hid), w_hid, b_hid.reshape(n_inner, 1, hid), w_out, w_out,
      freq.reshape(1, hid), deltas.reshape(1, hw), fre, fim)


def _conv3(x, w, b, row, length):
    prev = jnp.where(row == 0, 0.0, pltpu.roll(x, 1, 0))
    nxt = jnp.where(row == length - 1, 0.0, pltpu.roll(x, length - 1, 0))
    return prev * w[0:1] + x * w[1:2] + nxt * w[2:3] + b


def _hyena_kernel(x0_ref, x1_ref, v_ref, w0_ref, w1_ref, wv_ref, b0_ref, b1_ref, bv_ref, kr_ref, ki_ref, kr2_ref,
                  bias_ref, fre_ref, fim_ref, ire_ref, iim_ref, o_ref):
    length = x0_ref.shape[0]
    row = lax.broadcasted_iota(jnp.int32, (length, 1), 0)
    x0 = _conv3(x0_ref[...], w0_ref[...], b0_ref[...], row, length)
    x1 = _conv3(x1_ref[...], w1_ref[...], b1_ref[...], row, length)
    v = _conv3(v_ref[...], wv_ref[...], bv_ref[...], row, length)
    g = v * x1
    gb = g.astype(BF16)
    gr = _dot(fre_ref[...], gb)
    gi = _dot(fim_ref[...], gb)
    pr = gr * kr_ref[...] - gi * ki_ref[...]
    pi = gr * ki_ref[...] + gi * kr2_ref[...]
    y = _dot(ire_ref[...], pr.astype(BF16)) + _dot(iim_ref[...], pi.astype(BF16)) + g * bias_ref[...]
    o_ref[...] = (y * x0).astype(BF16)


def _hyena(z, row_blk0, batch, length, col0, hw, conv_w, conv_b, filt, bias, tables, ct=256):
    nj = hw // ct

    def zspec(part):
        return pl.BlockSpec((length, ct), lambda j, b: (row_blk0 + b, (col0 + part * hw) // ct + j))

    def wspec(part, rows):
        return pl.BlockSpec((rows, ct), lambda j, b: (0, part * nj + j))

    kspec = pl.BlockSpec((length, ct), lambda j, b: (0, j))
    tab = pl.BlockSpec((length, length), lambda j, b: (0, 0))
    conv_b2 = conv_b.reshape(1, 3 * hw)
    return pl.pallas_call(
        _hyena_kernel,
        grid=(nj, batch),
        in_specs=[zspec(0), zspec(1), zspec(2), wspec(0, 3), wspec(1, 3), wspec(2, 3), wspec(0, 1), wspec(1, 1),
                  wspec(2, 1), kspec, kspec, kspec, pl.BlockSpec((1, ct), lambda j, b: (0, j)), tab, tab, tab, tab],
        out_specs=pl.BlockSpec((length, ct), lambda j, b: (b, j)),
        out_shape=jax.ShapeDtypeStruct((batch * length, hw), BF16),
        compiler_params=_cparams(2),
        name="hyena_conv",
    )(z, z, z, conv_w, conv_w, conv_w, conv_b2, conv_b2, conv_b2, filt[0], filt[1], filt[2], bias.reshape(1, hw),
      *tables)


S5_BCHUNK = 256
S5_CCHUNK = 256
S5_SCAN_COLS = 1024


def _s5_discretize(lam_re, lam_im, log_step, b_re, b_im):
    dt = jnp.exp(log_step.astype(F32))[..., None]
    lr, li = lam_re.astype(F32), lam_im.astype(F32)
    mag = jnp.exp(lr * dt)
    a_re, a_im = mag * jnp.cos(li * dt), mag * jnp.sin(li * dt)
    den = lr * lr + li * li
    q_re = ((a_re - 1.0) * lr + a_im * li) / den
    q_im = (a_im * lr - (a_re - 1.0) * li) / den
    qr, qi = q_re[..., None], q_im[..., None]
    br, bi = b_re.astype(F32), b_im.astype(F32)
    return a_re, a_im, qr * br - qi * bi, qr * bi + qi * br


def _s5_weights(lam_re, lam_im, log_step, b_re, b_im, c_re, c_im):
    a_re, a_im, bb_re, bb_im = _s5_discretize(lam_re, lam_im, log_step, b_re, b_im)
    n_dir, groups, n_state, ch = bb_re.shape
    ns = groups * n_state
    a = jnp.concatenate([a_re.reshape(n_dir, 1, ns), a_im.reshape(n_dir, 1, ns)], axis=-1)
    a = jnp.broadcast_to(a, (n_dir, SUBLANES, 2 * ns))
    gb = S5_BCHUNK // ch
    nb = groups // gb
    eye_b = jnp.eye(gb, dtype=F32)

    def wb_of(bb):
        t = bb.reshape(n_dir, nb, gb, n_state, ch)
        t = jnp.einsum('dkgnc,gh->dkgchn', t, eye_b)
        return t.reshape(n_dir, nb, gb * ch, gb * n_state)

    wb = jnp.concatenate([wb_of(bb_re), wb_of(bb_im)], axis=-1).astype(BF16)
    gc = S5_CCHUNK // n_state
    nc = groups // gc
    per_blk = S5_CCHUNK // (gc * ch)
    eye_c = jnp.eye(gc, dtype=F32)
    place = jax.nn.one_hot(jnp.arange(nc) % per_blk, per_blk, dtype=F32)

    def wc_of(cc):
        t = cc.astype(F32).reshape(n_dir, nc, gc, ch, n_state)
        t = jnp.einsum('dkgcn,gh->dkgnhc', t, eye_c).reshape(n_dir, nc, gc * n_state, gc * ch)
        t = jnp.einsum('dkrc,kj->dkrjc', t, place)
        return t.reshape(n_dir, nc, gc * n_state, per_blk * gc * ch)

    wc = jnp.concatenate([wc_of(c_re), -wc_of(c_im)], axis=2).astype(BF16)
    return a, wb, wc


def _s5_kernel(u_ref, wb_ref, wc_ref, a_ref, h0_ref, y_ref, hf_ref, bu_ref, h_ref, *, tc, n_chunks):
    d = pl.program_id(0)
    c = pl.program_id(2)
    ns = h_ref.shape[1] // 2
    width = u_ref.shape[2]

    @pl.when(c == 0)
    def _():
        h_ref[...] = h0_ref[...]

    u = u_ref[...].reshape(tc * SUBLANES, width).astype(BF16)
    nb = wb_ref.shape[0]
    bcols = wb_ref.shape[2] // 2
    for kc in range(nb):
        r = _dot(u[:, kc * S5_BCHUNK:(kc + 1) * S5_BCHUNK], wb_ref[kc])
        bu_ref[:, kc * bcols:(kc + 1) * bcols] = r[:, :bcols]
        bu_ref[:, ns + kc * bcols:ns + (kc + 1) * bcols] = r[:, bcols:]

    for cc in range(ns // S5_SCAN_COLS):
        re = slice(cc * S5_SCAN_COLS, (cc + 1) * S5_SCAN_COLS)
        im = slice(ns + cc * S5_SCAN_COLS, ns + (cc + 1) * S5_SCAN_COLS)
        ar, ai = a_ref[:, re], a_ref[:, im]

        def body(t, carry, re=re, im=im, ar=ar, ai=ai):
            hr, hi = carry
            te = t + d * (tc - 1 - 2 * t)
            rows = pl.ds(pl.multiple_of(te * SUBLANES, SUBLANES), SUBLANES)
            nr = ar * hr - ai * hi + bu_ref[rows, re]
            ni = ar * hi + ai * hr + bu_ref[rows, im]
            bu_ref[rows, re] = nr
            bu_ref[rows, im] = ni
            return nr, ni

        hr, hi = lax.fori_loop(0, tc, body, (h_ref[:, re], h_ref[:, im]), unroll=4)
        h_ref[:, re] = hr
        h_ref[:, im] = hi

    nc = wc_ref.shape[0]
    ow = wc_ref.shape[2]
    per_blk = nc // (width // ow)
    for blk in range(width // ow):
        acc = None
        for kk in range(per_blk):
            k = blk * per_blk + kk
            hre = bu_ref[:, k * S5_CCHUNK:(k + 1) * S5_CCHUNK].astype(BF16)
            him = bu_ref[:, ns + k * S5_CCHUNK:ns + (k + 1) * S5_CCHUNK].astype(BF16)
            t = _dot(hre, wc_ref[k, :S5_CCHUNK, :]) + _dot(him, wc_ref[k, S5_CCHUNK:, :])
            acc = t if acc is None else acc + t
        y_ref[:, :, blk * ow:(blk + 1) * ow] = acc.reshape(tc, SUBLANES, ow)

    @pl.when(c == n_chunks - 1)
    def _():
        hf_ref[...] = h_ref[...]


def _s5_scan(u_t, a, wb, wc, h0, tc=64):
    length, bp, width = u_t.shape
    n_chunks = length // tc
    ns2 = a.shape[2]

    def tmap(d, g, c):
        return c + d * (n_chunks - 1 - 2 * c)

    return pl.pallas_call(
        functools.partial(_s5_kernel, tc=tc, n_chunks=n_chunks),
        grid=(2, bp // SUBLANES, n_chunks),
        in_specs=[
            pl.BlockSpec((tc, SUBLANES, width), lambda d, g, c: (tmap(d, g, c), g, 0)),
            pl.BlockSpec((None,) + wb.shape[1:], lambda d, g, c: (d, 0, 0, 0)),
            pl.BlockSpec((None,) + wc.shape[1:], lambda d, g, c: (d, 0, 0, 0)),
            pl.BlockSpec((None, SUBLANES, ns2), lambda d, g, c: (d, 0, 0)),
            pl.BlockSpec((None, SUBLANES, ns2), lambda d, g, c: (d, g, 0)),
        ],
        out_specs=(
            pl.BlockSpec((None, tc, SUBLANES, width), lambda d, g, c: (d, tmap(d, g, c), g, 0)),
            pl.BlockSpec((None, SUBLANES, ns2), lambda d, g, c: (d, g, 0)),
        ),
        out_shape=(jax.ShapeDtypeStruct((2, length, bp, width), F32), jax.ShapeDtypeStruct((2, bp, ns2), F32)),
        scratch_shapes=[pltpu.VMEM((tc * SUBLANES, ns2), F32), pltpu.VMEM((SUBLANES, ns2), F32)],
        compiler_params=_cparams(3),
        name="s5_scan",
    )(u_t, wb, wc, a, h0)


def _s5_glu_kernel(yf_ref, yb_ref, u_ref, d_ref, w_ref, b_ref, o_ref):
    y = u_ref[...] * d_ref[...] + yf_ref[...] + yb_ref[...]
    gy = 0.5 * y * (1.0 + jnp.tanh(math.sqrt(2.0 / math.pi) * (y + 0.044715 * (y * y * y))))
    r = _dot(gy.astype(BF16), w_ref[...]) + b_ref[...]
    half = r.shape[1] // 2
    o_ref[...] = (r[:, :half] * _sigmoid(r[:, half:])).astype(BF16)


def _s5_glu(y2, u, d, glu_w, glu_b, tm=512):
    rows, width = u.shape
    row_spec = pl.BlockSpec((tm, width), lambda i: (i, 0))
    return pl.pallas_call(
        _s5_glu_kernel,
        grid=(rows // tm,),
        in_specs=[
            pl.BlockSpec((None, tm, width), lambda i: (0, i, 0)),
            pl.BlockSpec((None, tm, width), lambda i: (1, i, 0)),
            row_spec,
            pl.BlockSpec((1, width), lambda i: (0, 0)),
            pl.BlockSpec((width, 2 * width), lambda i: (0, 0)),
            pl.BlockSpec((1, 2 * width), lambda i: (0, 0)),
        ],
        out_specs=row_spec,
        out_shape=jax.ShapeDtypeStruct((rows, width), BF16),
        compiler_params=_cparams(1),
        name="s5_glu",
    )(y2, y2, u, d.reshape(1, width), glu_w, glu_b.reshape(1, 2 * width))


def _mixer_c(su, batch, length, s5w, d, glu_w, glu_b, h0):
    a, wb, wc = s5w
    width = su.shape[1]
    bp = -(-batch // SUBLANES) * SUBLANES
    u_t = jnp.transpose(su.reshape(batch, length, width), (1, 0, 2))
    if bp != batch:
        u_t = jnp.pad(u_t, ((0, 0), (0, bp - batch), (0, 0)))
        h0 = jnp.pad(h0, ((0, 0), (0, bp - batch), (0, 0)))
    y2, hf = _s5_scan(u_t, a, wb, wc, h0)
    oc_t = _s5_glu(y2.reshape(2, length * bp, width), u_t.reshape(length * bp, width), d, glu_w, glu_b)
    oc = jnp.transpose(oc_t.reshape(length, bp, width)[:, :batch], (1, 0, 2)).reshape(batch * length, width)
    return oc, hf[:, :batch]


def kernel(x_prompt, x_sample, c, cache_a_k, cache_a_v, cache_d_k, cache_d_v, state_ssm, c_ctx, ada_w, ada_b, norm_g, ffn_w_gu, ffn_w_d, w_in, a_q_norm, a_k_norm, a_sink, hy_conv_w, hy_conv_b, hy_w_in, hy_b_in, hy_w_hid, hy_b_hid, hy_w_out, hy_freq, hy_bias, s5_lam_re, s5_lam_im, s5_log_step, s5_b_re, s5_b_im, s5_c_re, s5_c_im, s5_d, s5_glu_w, s5_glu_b, d_q_norm, d_k_norm, d_lambda, d_subln, w_branch, w_out):
    bc, lc, dm = x_prompt.shape
    bl, ll, _ = x_sample.shape
    depth = ada_w.shape[0]
    mix = dm // 4
    n_ctx, n_lat = bc * lc, bl * ll
    m = n_ctx + n_lat
    assert n_ctx % ll == 0 and bl + 1 <= N_COND_ROWS
    tok = _Tokens(n_ctx, ll, m)
    past = cache_a_k.shape[2]
    a_kv = cache_a_k.shape[3]
    groups, n_state = s5_lam_re.shape[2], s5_lam_re.shape[3]
    ns = groups * n_state
    d_width = cache_d_k.shape[4]

    splits = (mix, a_kv * A_HEAD_DIM, a_kv * A_HEAD_DIM, 3 * mix, mix, mix, mix, mix, N_BRANCH * dm)
    offs = [0]
    for s in splits:
        offs.append(offs[-1] + s)
    c_aq, c_ak, c_av, c_hy, c_su, c_dq, c_dk, c_dv, c_gt = offs[:9]

    tm = 512 if (n_ctx % 1024 or n_lat % 1024) else 1024

    x = jnp.concatenate([x_prompt.reshape(n_ctx, dm), x_sample.reshape(n_lat, dm)], axis=0)
    cond = jnp.zeros((N_COND_ROWS, dm), F32).at[0].set(c_ctx).at[1:1 + bl].set(c)
    mod = _ada(cond, ada_w, ada_b)

    w_br_b = w_branch.astype(BF16)
    w_out_b = w_out.astype(BF16)
    glu_w_b = s5_glu_w.astype(BF16)

    tabs_c, tabs_cb = _dft_tables(lc)
    tabs_l, tabs_lb = _dft_tables(ll)
    ctx_rb = 0
    lat_rb = n_ctx // ll

    new_ak, new_av, new_dk, new_dv, new_ss = [], [], [], [], []
    for l in range(depth):
        modl = mod[l].reshape(N_COND_ROWS, N_MOD, 1, dm)

        def ffn(x, j, k0, row_ranges=(None,)):
            h = _rmsmod(x, norm_g[l, 2 * j], modl, k0, k0 + 1, tok)
            act, w_d_b = _gu(h, ffn_w_gu, ffn_w_d, (l, j), tm, 256)
            outs = [_mm_resid(act, w_d_b, (), x, modl, k0 + 2, 0.5, tok, 512, 512, "ffn_down", rows)
                    for rows in row_ranges]
            return outs[0] if len(outs) == 1 else outs

        x = ffn(x, 0, 0)

        h = _rmsmod(x, norm_g[l, 1], modl, 3, 4, tok)
        z = _mm_ws(h, w_in, (l,), (0, c_gt), F32, tm, 512, "in_proj")
        zg = _mm_ws(h, w_in, (l,), (c_gt, N_BRANCH * dm), BF16, tm, 512, "in_proj_gate")

        oa_c, ak_n, av_n = _attn_a(z, ctx_rb, bc, lc, c_aq, c_ak, c_av, a_q_norm[l], a_k_norm[l], a_sink[l])
        ctx_a = (cache_a_k[:, l].reshape(bl, past, a_kv * A_HEAD_DIM), cache_a_v[:, l].reshape(bl, past, a_kv * A_HEAD_DIM))
        oa_l = _attn_a(z, lat_rb, bl, ll, c_aq, c_ak, c_av, a_q_norm[l], a_k_norm[l], a_sink[l], ctx_a)
        new_ak.append(ak_n.reshape(bc, lc, a_kv, A_HEAD_DIM))
        new_av.append(av_n.reshape(bc, lc, a_kv, A_HEAD_DIM))

        hy_args = (hy_w_in[l], hy_b_in[l], hy_w_hid[l], hy_b_hid[l], hy_w_out[l], hy_freq[l])
        filt_c = _hy_filter(lc, *hy_args, tabs_c)
        filt_l = _hy_filter(ll, *hy_args, tabs_l)
        ob_c = _hyena(z, ctx_rb, bc, lc, c_hy, mix, hy_conv_w[l], hy_conv_b[l], filt_c, hy_bias[l], tabs_cb)
        ob_l = _hyena(z, lat_rb, bl, ll, c_hy, mix, hy_conv_w[l], hy_conv_b[l], filt_l, hy_bias[l], tabs_lb)

        s5w = _s5_weights(s5_lam_re[l], s5_lam_im[l], s5_log_step[l], s5_b_re[l], s5_b_im[l], s5_c_re[l], s5_c_im[l])
        su = z[:, c_su:c_su + mix]
        h0_c = jnp.zeros((2, bc, 2 * ns), F32)
        st = state_ssm[:, l]
        h0_l = jnp.transpose(st, (1, 0, 4, 2, 3)).reshape(2, bl, 2 * ns)
        oc_c, hf = _mixer_c(su[:n_ctx], bc, lc, s5w, s5_d[l], glu_w_b[l], s5_glu_b[l], h0_c)
        oc_l, _ = _mixer_c(su[n_ctx:], bl, ll, s5w, s5_d[l], glu_w_b[l], s5_glu_b[l], h0_l)
        new_ss.append(jnp.transpose(hf.reshape(2, bc, 2, groups, n_state), (1, 0, 3, 4, 2)))

        lam_init = 0.8 - 0.6 * math.exp(-0.3 * l)
        d_args = (d_q_norm[l], d_k_norm[l], d_lambda[l], d_subln[l], lam_init)
        od_c, dk_n, dv_n = _attn_d(z, ctx_rb, bc, lc, c_dq, c_dk, c_dv, *d_args)
        ctx_d = (cache_d_k[:, l].reshape(bl, past, D_HEADS * d_width), cache_d_v[:, l].reshape(bl, past, D_HEADS * d_width))
        od_l = _attn_d(z, lat_rb, bl, ll, c_dq, c_dk, c_dv, *d_args, ctx_d)
        new_dk.append(dk_n.reshape(bc, lc, D_HEADS, d_width))
        new_dv.append(dv_n.reshape(bc, lc, D_HEADS, d_width))

        cat = lambda a, b: jnp.concatenate([a, b], axis=0)
        s = _branch(cat(oa_c, oa_l), cat(ob_c, ob_l), cat(oc_c, oc_l), cat(od_c, od_l), w_br_b, l, zg, tm, 512)
        x = _mm_resid(s, w_out_b, (l,), x, modl, 5, 1.0, tok, tm, 512, "out_proj")

        if l + 1 < depth:
            x = ffn(x, 1, 6)
        else:
            y_prompt, y_sample = ffn(x, 1, 6, ((0, n_ctx), (n_ctx, n_lat)))

    y_prompt = y_prompt.reshape(bc, lc, dm)
    y_sample = y_sample.reshape(bl, ll, dm)
    return (y_prompt, y_sample, jnp.stack(new_ak, axis=1), jnp.stack(new_av, axis=1), jnp.stack(new_dk, axis=1),
            jnp.stack(new_dv, axis=1), jnp.stack(new_ss, axis=1))
```

```python
import functools
import math

import numpy as np
import jax
import jax.numpy as jnp
from jax import lax
from jax.experimental import pallas as pl
from jax.experimental.pallas import tpu as pltpu

F32 = jnp.float32
BF16 = jnp.bfloat16
HIGHEST = lax.Precision.HIGHEST

GRID_W = 64
WINDOW = 128
ROPE_BASE = 10000.0
EPS = 1e-6
NEG_INF = -1e30
N_BRANCH = 4
A_HEAD_DIM = 128
A_GROUPS = 4
HY_EMB = 33
HY_TARGET = 1e-2
HY_FAST = 0.3
HY_SLOW = 1.5
S5_CH = 16
S5_STATE = 64
D_HEADS = 8
N_MOD = 9
N_COND_ROWS = 8

V7X_VMEM_BYTES = 64 * 1024 * 1024
VMEM_LIMIT = V7X_VMEM_BYTES - 8 * 1024 * 1024
LANES = 128
SUBLANES = 8


def _cparams(n_axes):
    return pltpu.CompilerParams(dimension_semantics=("arbitrary",) * n_axes, vmem_limit_bytes=VMEM_LIMIT)


def _sigmoid(x):
    return 1.0 / (1.0 + jnp.exp(-x))


def _dot(a, b):
    return jnp.dot(a, b, preferred_element_type=F32)


def _dot_nt(a, b):
    return lax.dot_general(a, b, (((1,), (1,)), ((), ())), preferred_element_type=F32)


def _dot_hi(a, b):
    return jnp.dot(a, b, preferred_element_type=F32, precision=HIGHEST)


def _ada_kernel(c_ref, w_ref, b_ref, o_ref):
    c = c_ref[...]
    s = (c * _sigmoid(c)).astype(BF16)
    o_ref[...] = _dot(s, w_ref[...].astype(BF16)) + b_ref[...]


def _ada(cond, ada_w, ada_b, tn=512):
    depth, d, n = ada_w.shape
    return pl.pallas_call(
        _ada_kernel,
        grid=(depth, n // tn),
        in_specs=[
            pl.BlockSpec((N_COND_ROWS, d), lambda l, j: (0, 0)),
            pl.BlockSpec((None, d, tn), lambda l, j: (l, 0, j)),
            pl.BlockSpec((None, 1, tn), lambda l, j: (l, 0, j)),
        ],
        out_specs=pl.BlockSpec((None, N_COND_ROWS, tn), lambda l, j: (l, 0, j)),
        out_shape=jax.ShapeDtypeStruct((depth, N_COND_ROWS, n), F32),
        compiler_params=_cparams(2),
        name="ada_mod",
    )(cond, ada_w, ada_b.reshape(depth, 1, n))


class _Tokens:
    def __init__(self, n_ctx, l_lat, m):
        self.n_ctx, self.l_lat, self.m = n_ctx, l_lat, m

    def cond(self, i, tm):
        row = i * tm
        return jnp.where(row >= self.n_ctx, (row - self.n_ctx) // self.l_lat + 1, 0)

    def split_specs(self, tm, width, col=None):
        nct = self.n_ctx // tm
        nlt = (self.m - self.n_ctx) // tm

        def cspec(*ids):
            return (jnp.minimum(ids[0], nct - 1), col(*ids) if col else 0)

        def lspec(*ids):
            return (jnp.clip(ids[0] - nct, 0, nlt - 1), col(*ids) if col else 0)

        return pl.BlockSpec((tm, width), cspec), pl.BlockSpec((tm, width), lspec), nct


def _rmsmod_body(x_ref, g_ref, sc_ref, sh_ref, o_ref):
    x = x_ref[...]
    ms = jnp.mean(x * x, axis=-1, keepdims=True)
    y = x * lax.rsqrt(ms + EPS) * g_ref[...]
    o_ref[...] = (y * (1.0 + sc_ref[...]) + sh_ref[...]).astype(BF16)


def _by_group(n_ctx_tiles, body, src_c, src_l, *rest):
    is_ctx = pl.program_id(0) < n_ctx_tiles

    @pl.when(is_ctx)
    def _():
        body(src_c, *rest)

    @pl.when(jnp.logical_not(is_ctx))
    def _():
        body(src_l, *rest)


def _rmsmod(x, gain, modl, k_sh, k_sc, tok, tm=256):
    d = gain.shape[0]
    if isinstance(x, tuple):
        xc_spec, xl_spec, nct = tok.split_specs(tm, d)
        x_specs, x_args = [xc_spec, xl_spec], list(x)
        body = functools.partial(_by_group, nct, _rmsmod_body)
    else:
        x_specs, x_args, body = [pl.BlockSpec((tm, d), lambda i: (i, 0))], [x], _rmsmod_body
    return pl.pallas_call(
        body,
        grid=(tok.m // tm,),
        in_specs=x_specs + [
            pl.BlockSpec((1, d), lambda i: (0, 0)),
            pl.BlockSpec((None, None, 1, d), lambda i: (tok.cond(i, tm), k_sc, 0, 0)),
            pl.BlockSpec((None, None, 1, d), lambda i: (tok.cond(i, tm), k_sh, 0, 0)),
        ],
        out_specs=pl.BlockSpec((tm, d), lambda i: (i, 0)),
        out_shape=jax.ShapeDtypeStruct((tok.m, d), BF16),
        compiler_params=_cparams(1),
        name="rmsmod",
    )(*x_args, gain.reshape(1, d), modl, modl)


def _mm_ws_kernel(a_ref, w_ref, o_ref, wb_ref):
    @pl.when(pl.program_id(1) == 0)
    def _():
        wb_ref[...] = w_ref[...].astype(BF16)

    o_ref[...] = _dot(a_ref[...], wb_ref[...]).astype(o_ref.dtype)


def _mm_ws(a, w, lead, cols, out_dtype, tm, tn, name):
    m, k = a.shape
    col0, n = cols
    cb = col0 // tn
    assert col0 % tn == 0 and n % tn == 0
    nlead = len(lead)
    return pl.pallas_call(
        _mm_ws_kernel,
        grid=(n // tn, m // tm),
        in_specs=[
            pl.BlockSpec((tm, k), lambda j, i: (i, 0)),
            pl.BlockSpec((None,) * nlead + (k, tn), lambda j, i: lead + (0, cb + j)),
        ],
        out_specs=pl.BlockSpec((tm, tn), lambda j, i: (i, j)),
        out_shape=jax.ShapeDtypeStruct((m, n), out_dtype),
        scratch_shapes=[pltpu.VMEM((k, tn), BF16)],
        compiler_params=_cparams(2),
        name=name,
    )(a, w)


def _gu_kernel(h_ref, wg_ref, wu_ref, wd_ref, o_ref, wdb_ref, wgb_ref, wub_ref):
    @pl.when(pl.program_id(1) == 0)
    def _():
        wgb_ref[...] = wg_ref[...].astype(BF16)
        wub_ref[...] = wu_ref[...].astype(BF16)
        wdb_ref[...] = wd_ref[...].astype(BF16)

    h = h_ref[...]
    g = _dot(h, wgb_ref[...])
    u = _dot(h, wub_ref[...])
    o_ref[...] = (g * _sigmoid(g) * u).astype(BF16)


def _gu(h, w_gu, w_d, lead, tm, tf):
    m, d = h.shape
    ff = w_gu.shape[-1] // 2
    nf = ff // tf
    nlead = len(lead)
    wblock = (None,) * nlead + (d, tf)
    return pl.pallas_call(
        _gu_kernel,
        grid=(nf, m // tm),
        in_specs=[
            pl.BlockSpec((tm, d), lambda j, i: (i, 0)),
            pl.BlockSpec(wblock, lambda j, i: lead + (0, j)),
            pl.BlockSpec(wblock, lambda j, i: lead + (0, nf + j)),
            pl.BlockSpec((None,) * nlead + (tf, d), lambda j, i: lead + (j, 0)),
        ],
        out_specs=(pl.BlockSpec((tm, tf), lambda j, i: (i, j)), pl.BlockSpec((tf, d), lambda j, i: (j, 0))),
        out_shape=(jax.ShapeDtypeStruct((m, ff), BF16), jax.ShapeDtypeStruct((ff, d), BF16)),
        scratch_shapes=[pltpu.VMEM((d, tf), BF16), pltpu.VMEM((d, tf), BF16)],
        compiler_params=_cparams(2),
        name="ffn_gate_up",
    )(h, w_gu, w_gu, w_d)


def _mm_resid_body(x_ref, a_ref, w_ref, gate_ref, o_ref, *, coef):
    y = _dot(a_ref[...], w_ref[...])
    o_ref[...] = x_ref[...] + (coef * gate_ref[...]) * y


def _mm_resid(a, w, lead, x, modl, k_gate, coef, tok, tm, tn, name, rows=None):
    m, k = a.shape
    n = w.shape[-1]
    nlead = len(lead)
    row0, m_out = (0, m) if rows is None else rows
    rb = row0 // tm
    body = functools.partial(_mm_resid_body, coef=coef)
    if isinstance(x, tuple):
        assert rows is None
        xc_spec, xl_spec, nct = tok.split_specs(tm, tn, col=lambda i, j: j)
        x_specs, x_args = [xc_spec, xl_spec], list(x)
        body = functools.partial(_by_group, nct, body)
    else:
        x_specs, x_args = [pl.BlockSpec((tm, tn), lambda i, j: (rb + i, j))], [x]
    return pl.pallas_call(
        body,
        grid=(m_out // tm, n // tn),
        in_specs=x_specs + [
            pl.BlockSpec((tm, k), lambda i, j: (rb + i, 0)),
            pl.BlockSpec((None,) * nlead + (k, tn), lambda i, j: lead + (0, j)),
            pl.BlockSpec((None, None, 1, tn), lambda i, j: (tok.cond(rb + i, tm), k_gate, 0, j)),
        ],
        out_specs=pl.BlockSpec((tm, tn), lambda i, j: (i, j)),
        out_shape=jax.ShapeDtypeStruct((m_out, n), F32),
        compiler_params=_cparams(2),
        name=name,
    )(*x_args, a, w, modl)


def _branch_kernel(*refs, n_ctx_tiles):
    o_ctx, o_lat = refs[:N_BRANCH], refs[N_BRANCH:2 * N_BRANCH]
    w_ref = refs[2 * N_BRANCH]
    gates = refs[2 * N_BRANCH + 1:3 * N_BRANCH + 1]
    o_ref = refs[3 * N_BRANCH + 1]

    def body(branches):
        acc = None
        for b in range(N_BRANCH):
            t = _sigmoid(gates[b][...].astype(F32)) * _dot(branches[b][...], w_ref[b])
            acc = t if acc is None else acc + t
        o_ref[...] = acc.astype(BF16)

    is_ctx = pl.program_id(0) < n_ctx_tiles

    @pl.when(is_ctx)
    def _():
        body(o_ctx)

    @pl.when(jnp.logical_not(is_ctx))
    def _():
        body(o_lat)


def _branch(o_ctx, o_lat, w_branch, layer, zg, tok, tm, tn):
    mix = o_ctx[0].shape[1]
    d = w_branch.shape[-1]
    oc_spec, ol_spec, nct = tok.split_specs(tm, mix)

    def gate_spec(b):
        return pl.BlockSpec((tm, tn), lambda i, j: (i, (b * d) // tn + j))

    return pl.pallas_call(
        functools.partial(_branch_kernel, n_ctx_tiles=nct),
        grid=(tok.m // tm, d // tn),
        in_specs=[oc_spec] * N_BRANCH + [ol_spec] * N_BRANCH
        + [pl.BlockSpec((None, N_BRANCH, mix, tn), lambda i, j: (layer, 0, 0, j))]
        + [gate_spec(b) for b in range(N_BRANCH)],
        out_specs=pl.BlockSpec((tm, tn), lambda i, j: (i, j)),
        out_shape=jax.ShapeDtypeStruct((tok.m, d), BF16),
        compiler_params=_cparams(2),
        name="branch_gate",
    )(*o_ctx, *o_lat, w_branch, zg, zg, zg, zg)


def _rope_tables(length, n, reps):
    quarter = n // 4
    n_rows = length // GRID_W
    rows = jnp.repeat(jnp.arange(n_rows), GRID_W).astype(F32)
    cols = jnp.tile(jnp.arange(GRID_W), n_rows).astype(F32)
    freqs = ROPE_BASE ** (-jnp.arange(quarter, dtype=F32) / quarter)
    ang_r = rows[:, None] * freqs[None, :]
    ang_c = cols[:, None] * freqs[None, :]
    cos = jnp.concatenate([jnp.cos(ang_r), jnp.cos(ang_r), jnp.cos(ang_c), jnp.cos(ang_c)], axis=-1)
    sin = jnp.concatenate([-jnp.sin(ang_r), jnp.sin(ang_r), -jnp.sin(ang_c), jnp.sin(ang_c)], axis=-1)
    return jnp.tile(cos, (1, reps)), jnp.tile(sin, (1, reps))


def _rope(x, cos, sin, quarter):
    lanes = x.shape[-1]
    lane = lax.broadcasted_iota(jnp.int32, (1, lanes), 1)
    first = (lane % (2 * quarter)) < quarter
    partner = jnp.where(first, pltpu.roll(x, lanes - quarter, 1), pltpu.roll(x, quarter, 1))
    return x * cos + partner * sin


def _rms_rows(x, gain):
    ms = jnp.mean(x * x, axis=-1, keepdims=True)
    return x * lax.rsqrt(ms + EPS) * gain


def _attn_a_kernel(*refs, latent, scale):
    if latent:
        q_ref, k_ref, v_ref, qn_ref, kn_ref, sink_ref, cos_ref, sin_ref, kc_ref, vc_ref, o_ref = refs
    else:
        q_ref, k_ref, v_ref, qn_ref, kn_ref, sink_ref, o_ref, ko_ref, vo_ref = refs
    length = k_ref.shape[0]
    k = _rms_rows(k_ref[...], kn_ref[...])
    v = v_ref[...]
    if latent:
        cos, sin = cos_ref[...], sin_ref[...]
        k = _rope(k, cos, sin, A_HEAD_DIM // 4)
        kc = kc_ref[...].astype(BF16)
        vc = vc_ref[...].astype(BF16)
        qi = lax.broadcasted_iota(jnp.int32, (length, length), 0)
        ki = lax.broadcasted_iota(jnp.int32, (length, length), 1)
        band = jnp.abs(qi - ki) <= WINDOW
    else:
        ko_ref[...] = k
        vo_ref[...] = v
    kb = k.astype(BF16)
    vb = v.astype(BF16)
    for g in range(A_GROUPS):
        sl = slice(g * A_HEAD_DIM, (g + 1) * A_HEAD_DIM)
        q = _rms_rows(q_ref[:, sl], qn_ref[...])
        if latent:
            q = _rope(q, cos, sin, A_HEAD_DIM // 4)
        qb = (q * scale).astype(BF16)
        sink = sink_ref[:, g * A_HEAD_DIM:g * A_HEAD_DIM + 1]
        s = _dot_nt(qb, kb)
        if latent:
            s = jnp.where(band, s, NEG_INF)
            s2 = _dot_nt(qb, kc)
            mx = jnp.maximum(jnp.maximum(jnp.max(s, axis=-1, keepdims=True), jnp.max(s2, axis=-1, keepdims=True)), sink)
            p2 = jnp.exp(s2 - mx)
        else:
            mx = jnp.maximum(jnp.max(s, axis=-1, keepdims=True), sink)
        p = jnp.exp(s - mx)
        den = jnp.sum(p, axis=-1, keepdims=True) + jnp.exp(sink - mx)
        o = _dot(p.astype(BF16), vb)
        if latent:
            den = den + jnp.sum(p2, axis=-1, keepdims=True)
            o = o + _dot(p2.astype(BF16), vc)
        o_ref[:, sl] = (o / den).astype(BF16)


def _attn_a(z, row_blk0, batch, length, q_col, k_col, v_col, qn, kn, sink, ctx=None):
    kv_heads = (v_col - k_col) // A_HEAD_DIM
    qw = A_GROUPS * A_HEAD_DIM
    latent = ctx is not None
    sink_b = jnp.repeat(sink.reshape(kv_heads, A_GROUPS), A_HEAD_DIM, axis=1).reshape(kv_heads, 1, qw)
    in_specs = [
        pl.BlockSpec((length, qw), lambda b, h: (row_blk0 + b, q_col // qw + h)),
        pl.BlockSpec((length, A_HEAD_DIM), lambda b, h: (row_blk0 + b, k_col // A_HEAD_DIM + h)),
        pl.BlockSpec((length, A_HEAD_DIM), lambda b, h: (row_blk0 + b, v_col // A_HEAD_DIM + h)),
        pl.BlockSpec((1, A_HEAD_DIM), lambda b, h: (0, 0)),
        pl.BlockSpec((1, A_HEAD_DIM), lambda b, h: (0, 0)),
        pl.BlockSpec((None, 1, qw), lambda b, h: (h, 0, 0)),
    ]
    args = [z, z, z, qn.reshape(1, A_HEAD_DIM), kn.reshape(1, A_HEAD_DIM), sink_b]
    o_spec = pl.BlockSpec((length, qw), lambda b, h: (b, h))
    o_shape = jax.ShapeDtypeStruct((batch * length, kv_heads * qw), BF16)
    if latent:
        k_ctx, v_ctx = ctx
        past = k_ctx.shape[1]
        cos, sin = _rope_tables(length, A_HEAD_DIM, 1)
        tab = pl.BlockSpec((length, A_HEAD_DIM), lambda b, h: (0, 0))
        cache = pl.BlockSpec((None, past, A_HEAD_DIM), lambda b, h: (b, 0, h))
        in_specs += [tab, tab, cache, cache]
        args += [cos, sin, k_ctx, v_ctx]
        out_specs, out_shape = o_spec, o_shape
    else:
        kv_spec = pl.BlockSpec((length, A_HEAD_DIM), lambda b, h: (b, h))
        kv_shape = jax.ShapeDtypeStruct((batch * length, kv_heads * A_HEAD_DIM), F32)
        out_specs, out_shape = (o_spec, kv_spec, kv_spec), (o_shape, kv_shape, kv_shape)
    return pl.pallas_call(
        functools.partial(_attn_a_kernel, latent=latent, scale=A_HEAD_DIM ** -0.5),
        grid=(batch, kv_heads),
        in_specs=in_specs,
        out_specs=out_specs,
        out_shape=out_shape,
        compiler_params=_cparams(2),
        name="attn_a_lat" if latent else "attn_a_ctx",
    )(*args)


def _rms_halves(x, gain, lo):
    half = x.shape[-1] // 2
    x2 = x * x
    s_lo = jnp.sum(jnp.where(lo, x2, 0.0), axis=-1, keepdims=True)
    s_hi = jnp.sum(jnp.where(lo, 0.0, x2), axis=-1, keepdims=True)
    ms = jnp.where(lo, s_lo, s_hi) * (1.0 / half)
    return x * lax.rsqrt(ms + EPS) * gain


def _attn_d_kernel(*refs, latent, heads, scale, lam_init):
    if latent:
        q_ref, k_ref, v_ref, qn_ref, kn_ref, lam_ref, sub_ref, cos_ref, sin_ref, kc_ref, vc_ref, o_ref = refs
    else:
        q_ref, k_ref, v_ref, qn_ref, kn_ref, lam_ref, sub_ref, o_ref, ko_ref, vo_ref = refs
    width = qn_ref.shape[1]
    dh = width // 2
    lane = lax.broadcasted_iota(jnp.int32, (1, width), 1)
    lo = lane < dh
    lv = lam_ref[...]
    lam = (jnp.exp(jnp.sum(lv[0:1] * lv[1:2], axis=-1, keepdims=True))
           - jnp.exp(jnp.sum(lv[2:3] * lv[3:4], axis=-1, keepdims=True)) + lam_init)
    for hh in range(heads):
        sl = slice(hh * width, (hh + 1) * width)
        q = _rms_halves(q_ref[:, sl], qn_ref[...], lo)
        k = _rms_halves(k_ref[:, sl], kn_ref[...], lo)
        v = v_ref[:, sl]
        if latent:
            cos, sin = cos_ref[...], sin_ref[...]
            q = _rope(q, cos, sin, dh // 4)
            k = _rope(k, cos, sin, dh // 4)
            key_parts = [k.astype(BF16), kc_ref[:, sl].astype(BF16)]
            val_parts = [v.astype(BF16), vc_ref[:, sl].astype(BF16)]
        else:
            ko_ref[:, sl] = k
            vo_ref[:, sl] = v
            key_parts = [k.astype(BF16)]
            val_parts = [v.astype(BF16)]
        q = q * scale
        o = None
        for comp in range(2):
            qc = jnp.where(lo, q, 0.0) if comp == 0 else jnp.where(lo, 0.0, q)
            qc = qc.astype(BF16)
            scores = [_dot_nt(qc, kp) for kp in key_parts]
            mx = None
            for s in scores:
                m = jnp.max(s, axis=-1, keepdims=True)
                mx = m if mx is None else jnp.maximum(mx, m)
            acc = den = None
            for s, vp in zip(scores, val_parts):
                p = jnp.exp(s - mx)
                d = jnp.sum(p, axis=-1, keepdims=True)
                t = _dot(p.astype(BF16), vp)
                den = d if den is None else den + d
                acc = t if acc is None else acc + t
            oc = acc * (1.0 / den)
            o = oc if comp == 0 else o - lam * oc
        o = _rms_rows(o, sub_ref[...]) * (1.0 - lam_init)
        o_ref[:, sl] = o.astype(BF16)


def _attn_d(z, row_blk0, batch, length, q_col, k_col, v_col, qn, kn, lam_vecs, subln, lam_init, ctx=None):
    width = (k_col - q_col) // D_HEADS
    dh = width // 2
    latent = ctx is not None
    heads = 1 if latent else 4
    bw = heads * width
    steps = D_HEADS // heads
    assert q_col % bw == 0 and k_col % bw == 0 and v_col % bw == 0

    def zspec(col):
        return pl.BlockSpec((length, bw), lambda b, h: (row_blk0 + b, col // bw + h))

    row = pl.BlockSpec((1, width), lambda b, h: (0, 0))
    in_specs = [zspec(q_col), zspec(k_col), zspec(v_col), row, row,
                pl.BlockSpec((4, dh), lambda b, h: (0, 0)), row]
    args = [z, z, z, jnp.tile(qn, 2).reshape(1, width), jnp.tile(kn, 2).reshape(1, width), lam_vecs,
            subln.reshape(1, width)]
    o_spec = pl.BlockSpec((length, bw), lambda b, h: (b, h))
    o_shape = jax.ShapeDtypeStruct((batch * length, D_HEADS * width), BF16)
    if latent:
        k_ctx, v_ctx = ctx
        past = k_ctx.shape[1]
        cos, sin = _rope_tables(length, dh, 2)
        tab = pl.BlockSpec((length, width), lambda b, h: (0, 0))
        cache = pl.BlockSpec((None, past, bw), lambda b, h: (b, 0, h))
        in_specs += [tab, tab, cache, cache]
        args += [cos, sin, k_ctx, v_ctx]
        out_specs, out_shape = o_spec, o_shape
    else:
        kv_shape = jax.ShapeDtypeStruct((batch * length, D_HEADS * width), F32)
        out_specs, out_shape = (o_spec, o_spec, o_spec), (o_shape, kv_shape, kv_shape)
    return pl.pallas_call(
        functools.partial(_attn_d_kernel, latent=latent, heads=heads, scale=dh ** -0.5, lam_init=lam_init),
        grid=(batch, steps),
        in_specs=in_specs,
        out_specs=out_specs,
        out_shape=out_shape,
        compiler_params=_cparams(2),
        name="attn_d_lat" if latent else "attn_d_ctx",
    )(*args)


def _dft_tables(length):
    m2 = 2 * length
    k = np.arange(length)
    ang = 2.0 * np.pi * ((k[:, None] * k[None, :]) % m2) / m2
    alt = np.where(k % 2 == 0, 1.0, -1.0)
    fr = np.cos(ang)
    fi = -np.sin(ang)
    fi[0, :] = alt
    ir = (2.0 / m2) * np.cos(ang)
    ir[:, 0] = 1.0 / m2
    ii = -(2.0 / m2) * np.sin(ang)
    ii[:, 0] = alt / m2
    exact = tuple(jnp.asarray(t, dtype=F32) for t in (fr, fi))
    rounded = tuple(jnp.asarray(t, dtype=F32).astype(BF16) for t in (fr, fi, ir, ii))
    return exact, rounded


def _hy_filter_kernel(z_ref, wi_ref, bi_ref, wh_ref, bh_ref, wf_ref, wb_ref, fr_ref, dl_ref, fre_ref, fim_ref,
                      kr_ref, ki_ref, kr2_ref):
    length = z_ref.shape[0]
    fr = fr_ref[...]
    h = jnp.sin(fr * (_dot_hi(z_ref[...], wi_ref[...]) + bi_ref[...]))
    for i in range(wh_ref.shape[0]):
        h = jnp.sin(fr * (_dot_hi(h, wh_ref[i]) + bh_ref[i]))
    row = lax.broadcasted_iota(jnp.int32, (length, 1), 0)
    t = row.astype(F32) * (1.0 / (length - 1))
    decay = jnp.exp(-t * dl_ref[...])
    hf = _dot_hi(h, wf_ref[...]) * decay
    hb = jnp.where(row == 0, 0.0, _dot_hi(h, wb_ref[...]) * decay)
    hs = hf + hb
    kr = _dot_hi(fre_ref[...], hs)
    ki = _dot_hi(fim_ref[...], hf - hb)
    alt = 1.0 - 2.0 * (row % 2).astype(F32)
    nyq = jnp.sum(alt * hs, axis=0, keepdims=True)
    kr_ref[...] = kr
    ki_ref[...] = jnp.where(row == 0, 0.0, ki)
    kr2_ref[...] = jnp.where(row == 0, nyq, kr)


def _hy_filter(length, w_in, b_in, w_hid, b_hid, w_out, freq, tables, ct=256):
    hid = w_in.shape[1]
    hw = w_out.shape[1] // 2
    n_inner = w_hid.shape[0]
    t = jnp.linspace(0.0, 1.0, length, dtype=F32)[:, None]
    bands = (HY_EMB - 1) // 2
    w = 2.0 * math.pi * jnp.arange(length, dtype=F32)[:, None] / length
    f = jnp.linspace(1e-4, bands - 1, bands, dtype=F32)[None, :]
    feat = jnp.concatenate([t, jnp.cos(f * w), -jnp.sin(f * w)], axis=-1)
    feat = jnp.pad(feat, ((0, 0), (0, LANES - HY_EMB)))
    w_in = jnp.pad(w_in, ((0, LANES - HY_EMB), (0, 0)))
    deltas = jnp.abs(jnp.linspace(math.log(HY_TARGET) / HY_FAST, math.log(HY_TARGET) / HY_SLOW, hw, dtype=F32))
    fre, fim = tables[0], tables[1]
    nj = hw // ct
    full = lambda shape: pl.BlockSpec(shape, lambda j: (0,) * len(shape))
    out_spec = pl.BlockSpec((length, ct), lambda j: (0, j))
    out_shape = jax.ShapeDtypeStruct((length, hw), F32)
    return pl.pallas_call(
        _hy_filter_kernel,
        grid=(nj,),
        in_specs=[
            full((length, LANES)), full((LANES, hid)), full((1, hid)), full((n_inner, hid, hid)),
            full((n_inner, 1, hid)),
            pl.BlockSpec((hid, ct), lambda j: (0, j)), pl.BlockSpec((hid, ct), lambda j: (0, nj + j)),
            full((1, hid)), pl.BlockSpec((1, ct), lambda j: (0, j)), full((length, length)), full((length, length)),
        ],
        out_specs=(out_spec, out_spec, out_spec),
        out_shape=(out_shape, out_shape, out_shape),
        compiler_params=_cparams(1),
        name="hyena_filter",
    )(feat, w_in, b_in.reshape(1, hid), w_hid, b_hid.reshape(n_inner, 1, hid), w_out, w_out,
      freq.reshape(1, hid), deltas.reshape(1, hw), fre, fim)


def _conv3(x, w, b, row, length):
    prev = jnp.where(row == 0, 0.0, pltpu.roll(x, 1, 0))
    nxt = jnp.where(row == length - 1, 0.0, pltpu.roll(x, length - 1, 0))
    return prev * w[0:1] + x * w[1:2] + nxt * w[2:3] + b


def _hyena_kernel(x0_ref, x1_ref, v_ref, w0_ref, w1_ref, wv_ref, b0_ref, b1_ref, bv_ref, kr_ref, ki_ref, kr2_ref,
                  bias_ref, fre_ref, fim_ref, ire_ref, iim_ref, o_ref):
    length = x0_ref.shape[0]
    row = lax.broadcasted_iota(jnp.int32, (length, 1), 0)
    x0 = _conv3(x0_ref[...], w0_ref[...], b0_ref[...], row, length)
    x1 = _conv3(x1_ref[...], w1_ref[...], b1_ref[...], row, length)
    v = _conv3(v_ref[...], wv_ref[...], bv_ref[...], row, length)
    g = v * x1
    gb = g.astype(BF16)
    gr = _dot(fre_ref[...], gb)
    gi = _dot(fim_ref[...], gb)
    pr = gr * kr_ref[...] - gi * ki_ref[...]
    pi = gr * ki_ref[...] + gi * kr2_ref[...]
    y = _dot(ire_ref[...], pr.astype(BF16)) + _dot(iim_ref[...], pi.astype(BF16)) + g * bias_ref[...]
    o_ref[...] = (y * x0).astype(BF16)


def _hyena(z, row_blk0, batch, length, col0, hw, conv_w, conv_b, filt, bias, tables, ct=256):
    nj = hw // ct

    def zspec(part):
        return pl.BlockSpec((length, ct), lambda j, b: (row_blk0 + b, (col0 + part * hw) // ct + j))

    def wspec(part, rows):
        return pl.BlockSpec((rows, ct), lambda j, b: (0, part * nj + j))

    kspec = pl.BlockSpec((length, ct), lambda j, b: (0, j))
    tab = pl.BlockSpec((length, length), lambda j, b: (0, 0))
    conv_b2 = conv_b.reshape(1, 3 * hw)
    return pl.pallas_call(
        _hyena_kernel,
        grid=(nj, batch),
        in_specs=[zspec(0), zspec(1), zspec(2), wspec(0, 3), wspec(1, 3), wspec(2, 3), wspec(0, 1), wspec(1, 1),
                  wspec(2, 1), kspec, kspec, kspec, pl.BlockSpec((1, ct), lambda j, b: (0, j)), tab, tab, tab, tab],
        out_specs=pl.BlockSpec((length, ct), lambda j, b: (b, j)),
        out_shape=jax.ShapeDtypeStruct((batch * length, hw), BF16),
        compiler_params=_cparams(2),
        name="hyena_conv",
    )(z, z, z, conv_w, conv_w, conv_w, conv_b2, conv_b2, conv_b2, filt[0], filt[1], filt[2], bias.reshape(1, hw),
      *tables)


S5_BCHUNK = 256
S5_CCHUNK = 256
S5_SCAN_COLS = 1024


def _s5_discretize(lam_re, lam_im, log_step, b_re, b_im):
    dt = jnp.exp(log_step.astype(F32))[..., None]
    lr, li = lam_re.astype(F32), lam_im.astype(F32)
    mag = jnp.exp(lr * dt)
    a_re, a_im = mag * jnp.cos(li * dt), mag * jnp.sin(li * dt)
    den = lr * lr + li * li
    q_re = ((a_re - 1.0) * lr + a_im * li) / den
    q_im = (a_im * lr - (a_re - 1.0) * li) / den
    qr, qi = q_re[..., None], q_im[..., None]
    br, bi = b_re.astype(F32), b_im.astype(F32)
    return a_re, a_im, qr * br - qi * bi, qr * bi + qi * br


def _s5_weights(lam_re, lam_im, log_step, b_re, b_im, c_re, c_im):
    a_re, a_im, bb_re, bb_im = _s5_discretize(lam_re, lam_im, log_step, b_re, b_im)
    n_dir, groups, n_state, ch = bb_re.shape
    ns = groups * n_state
    a = jnp.concatenate([a_re.reshape(n_dir, 1, ns), a_im.reshape(n_dir, 1, ns)], axis=-1)
    a = jnp.broadcast_to(a, (n_dir, SUBLANES, 2 * ns))
    gb = S5_BCHUNK // ch
    nb = groups // gb
    eye_b = jnp.eye(gb, dtype=F32)

    def wb_of(bb):
        t = bb.reshape(n_dir, nb, gb, n_state, ch)
        t = jnp.einsum('dkgnc,gh->dkgchn', t, eye_b)
        return t.reshape(n_dir, nb, gb * ch, gb * n_state)

    wb = jnp.concatenate([wb_of(bb_re), wb_of(bb_im)], axis=-1).astype(BF16)
    gc = S5_CCHUNK // n_state
    nc = groups // gc
    per_blk = S5_CCHUNK // (gc * ch)
    eye_c = jnp.eye(gc, dtype=F32)
    place = jax.nn.one_hot(jnp.arange(nc) % per_blk, per_blk, dtype=F32)

    def wc_of(cc):
        t = cc.astype(F32).reshape(n_dir, nc, gc, ch, n_state)
        t = jnp.einsum('dkgcn,gh->dkgnhc', t, eye_c).reshape(n_dir, nc, gc * n_state, gc * ch)
        t = jnp.einsum('dkrc,kj->dkrjc', t, place)
        return t.reshape(n_dir, nc, gc * n_state, per_blk * gc * ch)

    wc = jnp.concatenate([wc_of(c_re), -wc_of(c_im)], axis=2).astype(BF16)
    return a, wb, wc


def _s5_kernel(u_ref, wb_ref, wc_ref, a_ref, h0_ref, y_ref, hf_ref, bu_ref, h_ref, *, tc, n_chunks):
    d = pl.program_id(0)
    c = pl.program_id(2)
    ns = h_ref.shape[1] // 2
    width = u_ref.shape[2]

    @pl.when(c == 0)
    def _():
        h_ref[...] = h0_ref[...]

    u = u_ref[...].reshape(tc * SUBLANES, width).astype(BF16)
    nb = wb_ref.shape[0]
    bcols = wb_ref.shape[2] // 2
    for kc in range(nb):
        r = _dot(u[:, kc * S5_BCHUNK:(kc + 1) * S5_BCHUNK], wb_ref[kc])
        bu_ref[:, kc * bcols:(kc + 1) * bcols] = r[:, :bcols]
        bu_ref[:, ns + kc * bcols:ns + (kc + 1) * bcols] = r[:, bcols:]

    for cc in range(ns // S5_SCAN_COLS):
        re = slice(cc * S5_SCAN_COLS, (cc + 1) * S5_SCAN_COLS)
        im = slice(ns + cc * S5_SCAN_COLS, ns + (cc + 1) * S5_SCAN_COLS)
        ar, ai = a_ref[:, re], a_ref[:, im]

        def body(t, carry, re=re, im=im, ar=ar, ai=ai):
            hr, hi = carry
            te = t + d * (tc - 1 - 2 * t)
            rows = pl.ds(pl.multiple_of(te * SUBLANES, SUBLANES), SUBLANES)
            nr = ar * hr - ai * hi + bu_ref[rows, re]
            ni = ar * hi + ai * hr + bu_ref[rows, im]
            bu_ref[rows, re] = nr
            bu_ref[rows, im] = ni
            return nr, ni

        hr, hi = lax.fori_loop(0, tc, body, (h_ref[:, re], h_ref[:, im]), unroll=4)
        h_ref[:, re] = hr
        h_ref[:, im] = hi

    nc = wc_ref.shape[0]
    ow = wc_ref.shape[2]
    per_blk = nc // (width // ow)
    for blk in range(width // ow):
        acc = None
        for kk in range(per_blk):
            k = blk * per_blk + kk
            hre = bu_ref[:, k * S5_CCHUNK:(k + 1) * S5_CCHUNK].astype(BF16)
            him = bu_ref[:, ns + k * S5_CCHUNK:ns + (k + 1) * S5_CCHUNK].astype(BF16)
            t = _dot(hre, wc_ref[k, :S5_CCHUNK, :]) + _dot(him, wc_ref[k, S5_CCHUNK:, :])
            acc = t if acc is None else acc + t
        y_ref[:, :, blk * ow:(blk + 1) * ow] = acc.reshape(tc, SUBLANES, ow)

    @pl.when(c == n_chunks - 1)
    def _():
        hf_ref[...] = h_ref[...]


def _s5_scan(u_t, a, wb, wc, h0, tc=64):
    length, bp, width = u_t.shape
    n_chunks = length // tc
    ns2 = a.shape[2]

    def tmap(d, g, c):
        return c + d * (n_chunks - 1 - 2 * c)

    return pl.pallas_call(
        functools.partial(_s5_kernel, tc=tc, n_chunks=n_chunks),
        grid=(2, bp // SUBLANES, n_chunks),
        in_specs=[
            pl.BlockSpec((tc, SUBLANES, width), lambda d, g, c: (tmap(d, g, c), g, 0)),
            pl.BlockSpec((None,) + wb.shape[1:], lambda d, g, c: (d, 0, 0, 0)),
            pl.BlockSpec((None,) + wc.shape[1:], lambda d, g, c: (d, 0, 0, 0)),
            pl.BlockSpec((None, SUBLANES, ns2), lambda d, g, c: (d, 0, 0)),
            pl.BlockSpec((None, SUBLANES, ns2), lambda d, g, c: (d, g, 0)),
        ],
        out_specs=(
            pl.BlockSpec((None, tc, SUBLANES, width), lambda d, g, c: (d, tmap(d, g, c), g, 0)),
            pl.BlockSpec((None, SUBLANES, ns2), lambda d, g, c: (d, g, 0)),
        ),
        out_shape=(jax.ShapeDtypeStruct((2, length, bp, width), F32), jax.ShapeDtypeStruct((2, bp, ns2), F32)),
        scratch_shapes=[pltpu.VMEM((tc * SUBLANES, ns2), F32), pltpu.VMEM((SUBLANES, ns2), F32)],
        compiler_params=_cparams(3),
        name="s5_scan",
    )(u_t, wb, wc, a, h0)


def _s5_glu_kernel(yf_ref, yb_ref, u_ref, d_ref, w_ref, b_ref, o_ref):
    y = u_ref[...] * d_ref[...] + yf_ref[...] + yb_ref[...]
    gy = 0.5 * y * (1.0 + jnp.tanh(math.sqrt(2.0 / math.pi) * (y + 0.044715 * (y * y * y))))
    r = _dot(gy.astype(BF16), w_ref[...]) + b_ref[...]
    half = r.shape[1] // 2
    o_ref[...] = (r[:, :half] * _sigmoid(r[:, half:])).astype(BF16)


def _s5_glu(y2, u, d, glu_w, glu_b, tm=512):
    rows, width = u.shape
    row_spec = pl.BlockSpec((tm, width), lambda i: (i, 0))
    return pl.pallas_call(
        _s5_glu_kernel,
        grid=(rows // tm,),
        in_specs=[
            pl.BlockSpec((None, tm, width), lambda i: (0, i, 0)),
            pl.BlockSpec((None, tm, width), lambda i: (1, i, 0)),
            row_spec,
            pl.BlockSpec((1, width), lambda i: (0, 0)),
            pl.BlockSpec((width, 2 * width), lambda i: (0, 0)),
            pl.BlockSpec((1, 2 * width), lambda i: (0, 0)),
        ],
        out_specs=row_spec,
        out_shape=jax.ShapeDtypeStruct((rows, width), BF16),
        compiler_params=_cparams(1),
        name="s5_glu",
    )(y2, y2, u, d.reshape(1, width), glu_w, glu_b.reshape(1, 2 * width))


def _mixer_c(su, batch, length, s5w, d, glu_w, glu_b, h0):
    a, wb, wc = s5w
    width = su.shape[1]
    bp = -(-batch // SUBLANES) * SUBLANES
    u_t = jnp.transpose(su.reshape(batch, length, width), (1, 0, 2))
    if bp != batch:
        u_t = jnp.pad(u_t, ((0, 0), (0, bp - batch), (0, 0)))
        h0 = jnp.pad(h0, ((0, 0), (0, bp - batch), (0, 0)))
    y2, hf = _s5_scan(u_t, a, wb, wc, h0)
    oc_t = _s5_glu(y2.reshape(2, length * bp, width), u_t.reshape(length * bp, width), d, glu_w, glu_b)
    oc = jnp.transpose(oc_t.reshape(length, bp, width)[:, :batch], (1, 0, 2)).reshape(batch * length, width)
    return oc, hf[:, :batch]


def kernel(x_prompt, x_sample, c, cache_a_k, cache_a_v, cache_d_k, cache_d_v, state_ssm, c_ctx, ada_w, ada_b, norm_g, ffn_w_gu, ffn_w_d, w_in, a_q_norm, a_k_norm, a_sink, hy_conv_w, hy_conv_b, hy_w_in, hy_b_in, hy_w_hid, hy_b_hid, hy_w_out, hy_freq, hy_bias, s5_lam_re, s5_lam_im, s5_log_step, s5_b_re, s5_b_im, s5_c_re, s5_c_im, s5_d, s5_glu_w, s5_glu_b, d_q_norm, d_k_norm, d_lambda, d_subln, w_branch, w_out):
    bc, lc, dm = x_prompt.shape
    bl, ll, _ = x_sample.shape
    depth = ada_w.shape[0]
    mix = dm // 4
    n_ctx, n_lat = bc * lc, bl * ll
    m = n_ctx + n_lat
    assert n_ctx % ll == 0 and bl + 1 <= N_COND_ROWS
    tok = _Tokens(n_ctx, ll, m)
    past = cache_a_k.shape[2]
    a_kv = cache_a_k.shape[3]
    groups, n_state = s5_lam_re.shape[2], s5_lam_re.shape[3]
    ns = groups * n_state
    d_width = cache_d_k.shape[4]

    splits = (mix, a_kv * A_HEAD_DIM, a_kv * A_HEAD_DIM, 3 * mix, mix, mix, mix, mix, N_BRANCH * dm)
    offs = [0]
    for s in splits:
        offs.append(offs[-1] + s)
    c_aq, c_ak, c_av, c_hy, c_su, c_dq, c_dk, c_dv, c_gt = offs[:9]

    tm = 512 if (n_ctx % 1024 or n_lat % 1024) else 1024

    x = (x_prompt.reshape(n_ctx, dm), x_sample.reshape(n_lat, dm))
    cond = jnp.zeros((N_COND_ROWS, dm), F32).at[0].set(c_ctx).at[1:1 + bl].set(c)
    mod = _ada(cond, ada_w, ada_b)

    w_br_b = w_branch.astype(BF16)
    w_out_b = w_out.astype(BF16)
    glu_w_b = s5_glu_w.astype(BF16)

    tabs_c, tabs_cb = _dft_tables(lc)
    tabs_l, tabs_lb = _dft_tables(ll)
    ctx_rb = 0
    lat_rb = n_ctx // ll

    new_ak, new_av, new_dk, new_dv, new_ss = [], [], [], [], []
    for l in range(depth):
        modl = mod[l].reshape(N_COND_ROWS, N_MOD, 1, dm)

        def ffn(x, j, k0, row_ranges=(None,)):
            h = _rmsmod(x, norm_g[l, 2 * j], modl, k0, k0 + 1, tok)
            act, w_d_b = _gu(h, ffn_w_gu, ffn_w_d, (l, j), tm, 256)
            outs = [_mm_resid(act, w_d_b, (), x, modl, k0 + 2, 0.5, tok, 512, 512, "ffn_down", rows)
                    for rows in row_ranges]
            return outs[0] if len(outs) == 1 else outs

        x = ffn(x, 0, 0)

        h = _rmsmod(x, norm_g[l, 1], modl, 3, 4, tok)
        z = _mm_ws(h, w_in, (l,), (0, c_gt), F32, tm, 512, "in_proj")
        zg = _mm_ws(h, w_in, (l,), (c_gt, N_BRANCH * dm), BF16, tm, 512, "in_proj_gate")

        oa_c, ak_n, av_n = _attn_a(z, ctx_rb, bc, lc, c_aq, c_ak, c_av, a_q_norm[l], a_k_norm[l], a_sink[l])
        ctx_a = (cache_a_k[:, l].reshape(bl, past, a_kv * A_HEAD_DIM), cache_a_v[:, l].reshape(bl, past, a_kv * A_HEAD_DIM))
        oa_l = _attn_a(z, lat_rb, bl, ll, c_aq, c_ak, c_av, a_q_norm[l], a_k_norm[l], a_sink[l], ctx_a)
        new_ak.append(ak_n.reshape(bc, lc, a_kv, A_HEAD_DIM))
        new_av.append(av_n.reshape(bc, lc, a_kv, A_HEAD_DIM))

        hy_args = (hy_w_in[l], hy_b_in[l], hy_w_hid[l], hy_b_hid[l], hy_w_out[l], hy_freq[l])
        filt_c = _hy_filter(lc, *hy_args, tabs_c)
        filt_l = _hy_filter(ll, *hy_args, tabs_l)
        ob_c = _hyena(z, ctx_rb, bc, lc, c_hy, mix, hy_conv_w[l], hy_conv_b[l], filt_c, hy_bias[l], tabs_cb)
        ob_l = _hyena(z, lat_rb, bl, ll, c_hy, mix, hy_conv_w[l], hy_conv_b[l], filt_l, hy_bias[l], tabs_lb)

        s5w = _s5_weights(s5_lam_re[l], s5_lam_im[l], s5_log_step[l], s5_b_re[l], s5_b_im[l], s5_c_re[l], s5_c_im[l])
        su = z[:, c_su:c_su + mix]
        h0_c = jnp.zeros((2, bc, 2 * ns), F32)
        st = state_ssm[:, l]
        h0_l = jnp.transpose(st, (1, 0, 4, 2, 3)).reshape(2, bl, 2 * ns)
        oc_c, hf = _mixer_c(su[:n_ctx], bc, lc, s5w, s5_d[l], glu_w_b[l], s5_glu_b[l], h0_c)
        oc_l, _ = _mixer_c(su[n_ctx:], bl, ll, s5w, s5_d[l], glu_w_b[l], s5_glu_b[l], h0_l)
        new_ss.append(jnp.transpose(hf.reshape(2, bc, 2, groups, n_state), (1, 0, 3, 4, 2)))

        lam_init = 0.8 - 0.6 * math.exp(-0.3 * l)
        d_args = (d_q_norm[l], d_k_norm[l], d_lambda[l], d_subln[l], lam_init)
        od_c, dk_n, dv_n = _attn_d(z, ctx_rb, bc, lc, c_dq, c_dk, c_dv, *d_args)
        ctx_d = (cache_d_k[:, l].reshape(bl, past, D_HEADS * d_width), cache_d_v[:, l].reshape(bl, past, D_HEADS * d_width))
        od_l = _attn_d(z, lat_rb, bl, ll, c_dq, c_dk, c_dv, *d_args, ctx_d)
        new_dk.append(dk_n.reshape(bc, lc, D_HEADS, d_width))
        new_dv.append(dv_n.reshape(bc, lc, D_HEADS, d_width))

        s = _branch((oa_c, ob_c, oc_c, od_c), (oa_l, ob_l, oc_l, od_l), w_br_b, l, zg, tok, tm, 512)
        x = _mm_resid(s, w_out_b, (l,), x, modl, 5, 1.0, tok, tm, 512, "out_proj")

        if l + 1 < depth:
            x = ffn(x, 1, 6)
        else:
            y_prompt, y_sample = ffn(x, 1, 6, ((0, n_ctx), (n_ctx, n_lat)))

    y_prompt = y_prompt.reshape(bc, lc, dm)
    y_sample = y_sample.reshape(bl, ll, dm)
    return (y_prompt, y_sample, jnp.stack(new_ak, axis=1), jnp.stack(new_av, axis=1), jnp.stack(new_dk, axis=1),
            jnp.stack(new_dv, axis=1), jnp.stack(new_ss, axis=1))
```

```python
import functools
import math

import numpy as np
import jax
import jax.numpy as jnp
from jax import lax
from jax.experimental import pallas as pl
from jax.experimental.pallas import tpu as pltpu

F32 = jnp.float32
BF16 = jnp.bfloat16
HIGHEST = lax.Precision.HIGHEST

GRID_W = 64
WINDOW = 128
ROPE_BASE = 10000.0
EPS = 1e-6
NEG_INF = -1e30
N_BRANCH = 4
A_HEAD_DIM = 128
A_GROUPS = 4
HY_EMB = 33
HY_TARGET = 1e-2
HY_FAST = 0.3
HY_SLOW = 1.5
S5_CH = 16
S5_STATE = 64
D_HEADS = 8
N_MOD = 9
N_COND_ROWS = 8

V7X_VMEM_BYTES = 64 * 1024 * 1024
VMEM_LIMIT = V7X_VMEM_BYTES - 8 * 1024 * 1024
LANES = 128
SUBLANES = 8


def _cparams(n_axes):
    return pltpu.CompilerParams(dimension_semantics=("arbitrary",) * n_axes, vmem_limit_bytes=VMEM_LIMIT)


def _sigmoid(x):
    return 1.0 / (1.0 + jnp.exp(-x))


def _dot(a, b):
    return jnp.dot(a, b, preferred_element_type=F32)


def _dot_nt(a, b):
    return lax.dot_general(a, b, (((1,), (1,)), ((), ())), preferred_element_type=F32)


def _dot_hi(a, b):
    return jnp.dot(a, b, preferred_element_type=F32, precision=HIGHEST)


def _ada_kernel(c_ref, w_ref, b_ref, o_ref):
    c = c_ref[...]
    s = (c * _sigmoid(c)).astype(BF16)
    o_ref[...] = _dot(s, w_ref[...].astype(BF16)) + b_ref[...]


def _ada(cond, ada_w, ada_b, tn=512):
    depth, d, n = ada_w.shape
    return pl.pallas_call(
        _ada_kernel,
        grid=(depth, n // tn),
        in_specs=[
            pl.BlockSpec((N_COND_ROWS, d), lambda l, j: (0, 0)),
            pl.BlockSpec((None, d, tn), lambda l, j: (l, 0, j)),
            pl.BlockSpec((None, 1, tn), lambda l, j: (l, 0, j)),
        ],
        out_specs=pl.BlockSpec((None, N_COND_ROWS, tn), lambda l, j: (l, 0, j)),
        out_shape=jax.ShapeDtypeStruct((depth, N_COND_ROWS, n), F32),
        compiler_params=_cparams(2),
        name="ada_mod",
    )(cond, ada_w, ada_b.reshape(depth, 1, n))


class _Tokens:
    def __init__(self, n_ctx, l_lat, m):
        self.n_ctx, self.l_lat, self.m = n_ctx, l_lat, m

    def cond(self, i, tm):
        row = i * tm
        return jnp.where(row >= self.n_ctx, (row - self.n_ctx) // self.l_lat + 1, 0)

    def split_specs(self, tm, width, col=None):
        nct = self.n_ctx // tm
        nlt = (self.m - self.n_ctx) // tm

        def cspec(*ids):
            return (jnp.minimum(ids[0], nct - 1), col(*ids) if col else 0)

        def lspec(*ids):
            return (jnp.clip(ids[0] - nct, 0, nlt - 1), col(*ids) if col else 0)

        return pl.BlockSpec((tm, width), cspec), pl.BlockSpec((tm, width), lspec), nct


def _rmsmod_body(x_ref, g_ref, sc_ref, sh_ref, o_ref):
    x = x_ref[...]
    ms = jnp.mean(x * x, axis=-1, keepdims=True)
    y = x * lax.rsqrt(ms + EPS) * g_ref[...]
    o_ref[...] = (y * (1.0 + sc_ref[...]) + sh_ref[...]).astype(BF16)


def _by_group(n_ctx_tiles, body, src_c, src_l, *rest):
    is_ctx = pl.program_id(0) < n_ctx_tiles

    @pl.when(is_ctx)
    def _():
        body(src_c, *rest)

    @pl.when(jnp.logical_not(is_ctx))
    def _():
        body(src_l, *rest)


def _rmsmod(x, gain, modl, k_sh, k_sc, tok, tm=256):
    d = gain.shape[0]
    if isinstance(x, tuple):
        xc_spec, xl_spec, nct = tok.split_specs(tm, d)
        x_specs, x_args = [xc_spec, xl_spec], list(x)
        body = functools.partial(_by_group, nct, _rmsmod_body)
    else:
        x_specs, x_args, body = [pl.BlockSpec((tm, d), lambda i: (i, 0))], [x], _rmsmod_body
    return pl.pallas_call(
        body,
        grid=(tok.m // tm,),
        in_specs=x_specs + [
            pl.BlockSpec((1, d), lambda i: (0, 0)),
            pl.BlockSpec((None, None, 1, d), lambda i: (tok.cond(i, tm), k_sc, 0, 0)),
            pl.BlockSpec((None, None, 1, d), lambda i: (tok.cond(i, tm), k_sh, 0, 0)),
        ],
        out_specs=pl.BlockSpec((tm, d), lambda i: (i, 0)),
        out_shape=jax.ShapeDtypeStruct((tok.m, d), BF16),
        compiler_params=_cparams(1),
        name="rmsmod",
    )(*x_args, gain.reshape(1, d), modl, modl)


def _stream_weight_tiles(copies, cast):
    j = pl.program_id(0)
    nj = pl.num_programs(0)

    @pl.when(pl.program_id(1) == 0)
    def _():
        slot = j % 2

        @pl.when(j == 0)
        def _():
            for cp in copies(0, 0):
                cp.start()

        for cp in copies(j, slot):
            cp.wait()

        @pl.when(j + 1 < nj)
        def _():
            for cp in copies(j + 1, 1 - slot):
                cp.start()

        cast(slot)


def _mm_ws_kernel(a_ref, w_hbm, o_ref, wf_ref, wb_ref, sem, *, lead, cb, tn):
    def copies(jj, slot):
        cols = pl.ds(pl.multiple_of((cb + jj) * tn, tn), tn)
        return [pltpu.make_async_copy(w_hbm.at[lead + (slice(None), cols)], wf_ref.at[slot], sem.at[slot])]

    def cast(slot):
        wb_ref[...] = wf_ref[slot].astype(BF16)

    _stream_weight_tiles(copies, cast)
    o_ref[...] = _dot(a_ref[...], wb_ref[...]).astype(o_ref.dtype)


def _mm_ws(a, w, lead, cols, out_dtype, tm, tn, name):
    m, k = a.shape
    col0, n = cols
    assert col0 % tn == 0 and n % tn == 0
    return pl.pallas_call(
        functools.partial(_mm_ws_kernel, lead=lead, cb=col0 // tn, tn=tn),
        grid=(n // tn, m // tm),
        in_specs=[pl.BlockSpec((tm, k), lambda j, i: (i, 0)), pl.BlockSpec(memory_space=pl.ANY)],
        out_specs=pl.BlockSpec((tm, tn), lambda j, i: (i, j)),
        out_shape=jax.ShapeDtypeStruct((m, n), out_dtype),
        scratch_shapes=[pltpu.VMEM((2, k, tn), F32), pltpu.VMEM((k, tn), BF16), pltpu.SemaphoreType.DMA((2,))],
        compiler_params=_cparams(2),
        name=name,
    )(a, w)


def _gu_kernel(h_ref, wgu_hbm, wd_hbm, o_ref, wdb_ref, wgf_ref, wuf_ref, wdf_ref, wgb_ref, wub_ref, sem, *,
               lead, nf, tf):
    def copies(jj, slot):
        gcols = pl.ds(pl.multiple_of(jj * tf, tf), tf)
        ucols = pl.ds(pl.multiple_of((nf + jj) * tf, tf), tf)
        return [
            pltpu.make_async_copy(wgu_hbm.at[lead + (slice(None), gcols)], wgf_ref.at[slot], sem.at[0, slot]),
            pltpu.make_async_copy(wgu_hbm.at[lead + (slice(None), ucols)], wuf_ref.at[slot], sem.at[1, slot]),
            pltpu.make_async_copy(wd_hbm.at[lead + (gcols, slice(None))], wdf_ref.at[slot], sem.at[2, slot]),
        ]

    def cast(slot):
        wgb_ref[...] = wgf_ref[slot].astype(BF16)
        wub_ref[...] = wuf_ref[slot].astype(BF16)
        wdb_ref[...] = wdf_ref[slot].astype(BF16)

    _stream_weight_tiles(copies, cast)
    h = h_ref[...]
    g = _dot(h, wgb_ref[...])
    u = _dot(h, wub_ref[...])
    o_ref[...] = (g * _sigmoid(g) * u).astype(BF16)


def _gu(h, w_gu, w_d, lead, tm, tf):
    m, d = h.shape
    ff = w_gu.shape[-1] // 2
    nf = ff // tf
    return pl.pallas_call(
        functools.partial(_gu_kernel, lead=lead, nf=nf, tf=tf),
        grid=(nf, m // tm),
        in_specs=[pl.BlockSpec((tm, d), lambda j, i: (i, 0)), pl.BlockSpec(memory_space=pl.ANY),
                  pl.BlockSpec(memory_space=pl.ANY)],
        out_specs=(pl.BlockSpec((tm, tf), lambda j, i: (i, j)), pl.BlockSpec((tf, d), lambda j, i: (j, 0))),
        out_shape=(jax.ShapeDtypeStruct((m, ff), BF16), jax.ShapeDtypeStruct((ff, d), BF16)),
        scratch_shapes=[pltpu.VMEM((2, d, tf), F32), pltpu.VMEM((2, d, tf), F32), pltpu.VMEM((2, tf, d), F32),
                        pltpu.VMEM((d, tf), BF16), pltpu.VMEM((d, tf), BF16), pltpu.SemaphoreType.DMA((3, 2))],
        compiler_params=_cparams(2),
        name="ffn_gate_up",
    )(h, w_gu, w_d)


def _mm_resid_body(x_ref, a_ref, w_ref, gate_ref, o_ref, *, coef):
    y = _dot(a_ref[...], w_ref[...])
    o_ref[...] = x_ref[...] + (coef * gate_ref[...]) * y


def _mm_resid(a, w, lead, x, modl, k_gate, coef, tok, tm, tn, name, rows=None):
    m, k = a.shape
    n = w.shape[-1]
    nlead = len(lead)
    row0, m_out = (0, m) if rows is None else rows
    rb = row0 // tm
    body = functools.partial(_mm_resid_body, coef=coef)
    if isinstance(x, tuple):
        assert rows is None
        xc_spec, xl_spec, nct = tok.split_specs(tm, tn, col=lambda i, j: j)
        x_specs, x_args = [xc_spec, xl_spec], list(x)
        body = functools.partial(_by_group, nct, body)
    else:
        x_specs, x_args = [pl.BlockSpec((tm, tn), lambda i, j: (rb + i, j))], [x]
    return pl.pallas_call(
        body,
        grid=(m_out // tm, n // tn),
        in_specs=x_specs + [
            pl.BlockSpec((tm, k), lambda i, j: (rb + i, 0)),
            pl.BlockSpec((None,) * nlead + (k, tn), lambda i, j: lead + (0, j)),
            pl.BlockSpec((None, None, 1, tn), lambda i, j: (tok.cond(rb + i, tm), k_gate, 0, j)),
        ],
        out_specs=pl.BlockSpec((tm, tn), lambda i, j: (i, j)),
        out_shape=jax.ShapeDtypeStruct((m_out, n), F32),
        compiler_params=_cparams(2),
        name=name,
    )(*x_args, a, w, modl)


def _branch_kernel(*refs, n_ctx_tiles):
    o_ctx, o_lat = refs[:N_BRANCH], refs[N_BRANCH:2 * N_BRANCH]
    w_ref = refs[2 * N_BRANCH]
    gates = refs[2 * N_BRANCH + 1:3 * N_BRANCH + 1]
    o_ref = refs[3 * N_BRANCH + 1]

    def body(branches):
        acc = None
        for b in range(N_BRANCH):
            t = _sigmoid(gates[b][...].astype(F32)) * _dot(branches[b][...], w_ref[b])
            acc = t if acc is None else acc + t
        o_ref[...] = acc.astype(BF16)

    is_ctx = pl.program_id(0) < n_ctx_tiles

    @pl.when(is_ctx)
    def _():
        body(o_ctx)

    @pl.when(jnp.logical_not(is_ctx))
    def _():
        body(o_lat)


def _branch(o_ctx, o_lat, w_branch, layer, zg, tok, tm, tn):
    mix = o_ctx[0].shape[1]
    d = w_branch.shape[-1]
    oc_spec, ol_spec, nct = tok.split_specs(tm, mix)

    def gate_spec(b):
        return pl.BlockSpec((tm, tn), lambda i, j: (i, (b * d) // tn + j))

    return pl.pallas_call(
        functools.partial(_branch_kernel, n_ctx_tiles=nct),
        grid=(tok.m // tm, d // tn),
        in_specs=[oc_spec] * N_BRANCH + [ol_spec] * N_BRANCH
        + [pl.BlockSpec((None, N_BRANCH, mix, tn), lambda i, j: (layer, 0, 0, j))]
        + [gate_spec(b) for b in range(N_BRANCH)],
        out_specs=pl.BlockSpec((tm, tn), lambda i, j: (i, j)),
        out_shape=jax.ShapeDtypeStruct((tok.m, d), BF16),
        compiler_params=_cparams(2),
        name="branch_gate",
    )(*o_ctx, *o_lat, w_branch, zg, zg, zg, zg)


def _rope_tables(length, n, reps):
    quarter = n // 4
    n_rows = length // GRID_W
    rows = jnp.repeat(jnp.arange(n_rows), GRID_W).astype(F32)
    cols = jnp.tile(jnp.arange(GRID_W), n_rows).astype(F32)
    freqs = ROPE_BASE ** (-jnp.arange(quarter, dtype=F32) / quarter)
    ang_r = rows[:, None] * freqs[None, :]
    ang_c = cols[:, None] * freqs[None, :]
    cos = jnp.concatenate([jnp.cos(ang_r), jnp.cos(ang_r), jnp.cos(ang_c), jnp.cos(ang_c)], axis=-1)
    sin = jnp.concatenate([-jnp.sin(ang_r), jnp.sin(ang_r), -jnp.sin(ang_c), jnp.sin(ang_c)], axis=-1)
    return jnp.tile(cos, (1, reps)), jnp.tile(sin, (1, reps))


def _rope(x, cos, sin, quarter):
    lanes = x.shape[-1]
    lane = lax.broadcasted_iota(jnp.int32, (1, lanes), 1)
    first = (lane % (2 * quarter)) < quarter
    partner = jnp.where(first, pltpu.roll(x, lanes - quarter, 1), pltpu.roll(x, quarter, 1))
    return x * cos + partner * sin


def _rms_rows(x, gain):
    ms = jnp.mean(x * x, axis=-1, keepdims=True)
    return x * lax.rsqrt(ms + EPS) * gain


def _attn_a_kernel(*refs, latent, scale):
    if latent:
        q_ref, k_ref, v_ref, qn_ref, kn_ref, sink_ref, cos_ref, sin_ref, kc_ref, vc_ref, o_ref = refs
    else:
        q_ref, k_ref, v_ref, qn_ref, kn_ref, sink_ref, o_ref, ko_ref, vo_ref = refs
    length = k_ref.shape[0]
    k = _rms_rows(k_ref[...], kn_ref[...])
    v = v_ref[...]
    if latent:
        cos, sin = cos_ref[...], sin_ref[...]
        k = _rope(k, cos, sin, A_HEAD_DIM // 4)
        kc = kc_ref[...].astype(BF16)
        vc = vc_ref[...].astype(BF16)
        qi = lax.broadcasted_iota(jnp.int32, (length, length), 0)
        ki = lax.broadcasted_iota(jnp.int32, (length, length), 1)
        band = jnp.abs(qi - ki) <= WINDOW
    else:
        ko_ref[...] = k
        vo_ref[...] = v
    kb = k.astype(BF16)
    vb = v.astype(BF16)
    for g in range(A_GROUPS):
        sl = slice(g * A_HEAD_DIM, (g + 1) * A_HEAD_DIM)
        q = _rms_rows(q_ref[:, sl], qn_ref[...])
        if latent:
            q = _rope(q, cos, sin, A_HEAD_DIM // 4)
        qb = (q * scale).astype(BF16)
        sink = sink_ref[:, g * A_HEAD_DIM:g * A_HEAD_DIM + 1]
        s = _dot_nt(qb, kb)
        if latent:
            s = jnp.where(band, s, NEG_INF)
            s2 = _dot_nt(qb, kc)
            mx = jnp.maximum(jnp.maximum(jnp.max(s, axis=-1, keepdims=True), jnp.max(s2, axis=-1, keepdims=True)), sink)
            p2 = jnp.exp(s2 - mx)
        else:
            mx = jnp.maximum(jnp.max(s, axis=-1, keepdims=True), sink)
        p = jnp.exp(s - mx)
        den = jnp.sum(p, axis=-1, keepdims=True) + jnp.exp(sink - mx)
        o = _dot(p.astype(BF16), vb)
        if latent:
            den = den + jnp.sum(p2, axis=-1, keepdims=True)
            o = o + _dot(p2.astype(BF16), vc)
        o_ref[:, sl] = (o / den).astype(BF16)


def _attn_a(z, row_blk0, batch, length, q_col, k_col, v_col, qn, kn, sink, ctx=None):
    kv_heads = (v_col - k_col) // A_HEAD_DIM
    qw = A_GROUPS * A_HEAD_DIM
    latent = ctx is not None
    sink_b = jnp.repeat(sink.reshape(kv_heads, A_GROUPS), A_HEAD_DIM, axis=1).reshape(kv_heads, 1, qw)
    in_specs = [
        pl.BlockSpec((length, qw), lambda b, h: (row_blk0 + b, q_col // qw + h)),
        pl.BlockSpec((length, A_HEAD_DIM), lambda b, h: (row_blk0 + b, k_col // A_HEAD_DIM + h)),
        pl.BlockSpec((length, A_HEAD_DIM), lambda b, h: (row_blk0 + b, v_col // A_HEAD_DIM + h)),
        pl.BlockSpec((1, A_HEAD_DIM), lambda b, h: (0, 0)),
        pl.BlockSpec((1, A_HEAD_DIM), lambda b, h: (0, 0)),
        pl.BlockSpec((None, 1, qw), lambda b, h: (h, 0, 0)),
    ]
    args = [z, z, z, qn.reshape(1, A_HEAD_DIM), kn.reshape(1, A_HEAD_DIM), sink_b]
    o_spec = pl.BlockSpec((length, qw), lambda b, h: (b, h))
    o_shape = jax.ShapeDtypeStruct((batch * length, kv_heads * qw), BF16)
    if latent:
        k_ctx, v_ctx = ctx
        past = k_ctx.shape[1]
        cos, sin = _rope_tables(length, A_HEAD_DIM, 1)
        tab = pl.BlockSpec((length, A_HEAD_DIM), lambda b, h: (0, 0))
        cache = pl.BlockSpec((None, past, A_HEAD_DIM), lambda b, h: (b, 0, h))
        in_specs += [tab, tab, cache, cache]
        args += [cos, sin, k_ctx, v_ctx]
        out_specs, out_shape = o_spec, o_shape
    else:
        kv_spec = pl.BlockSpec((length, A_HEAD_DIM), lambda b, h: (b, h))
        kv_shape = jax.ShapeDtypeStruct((batch * length, kv_heads * A_HEAD_DIM), F32)
        out_specs, out_shape = (o_spec, kv_spec, kv_spec), (o_shape, kv_shape, kv_shape)
    return pl.pallas_call(
        functools.partial(_attn_a_kernel, latent=latent, scale=A_HEAD_DIM ** -0.5),
        grid=(batch, kv_heads),
        in_specs=in_specs,
        out_specs=out_specs,
        out_shape=out_shape,
        compiler_params=_cparams(2),
        name="attn_a_lat" if latent else "attn_a_ctx",
    )(*args)


def _rms_halves(x, gain, lo):
    half = x.shape[-1] // 2
    x2 = x * x
    s_lo = jnp.sum(jnp.where(lo, x2, 0.0), axis=-1, keepdims=True)
    s_hi = jnp.sum(jnp.where(lo, 0.0, x2), axis=-1, keepdims=True)
    ms = jnp.where(lo, s_lo, s_hi) * (1.0 / half)
    return x * lax.rsqrt(ms + EPS) * gain


def _attn_d_kernel(*refs, latent, heads, scale, lam_init):
    if latent:
        q_ref, k_ref, v_ref, qn_ref, kn_ref, lam_ref, sub_ref, cos_ref, sin_ref, kc_ref, vc_ref, o_ref = refs
    else:
        q_ref, k_ref, v_ref, qn_ref, kn_ref, lam_ref, sub_ref, o_ref, ko_ref, vo_ref = refs
    width = qn_ref.shape[1]
    dh = width // 2
    lane = lax.broadcasted_iota(jnp.int32, (1, width), 1)
    lo = lane < dh
    lv = lam_ref[...]
    lam = (jnp.exp(jnp.sum(lv[0:1] * lv[1:2], axis=-1, keepdims=True))
           - jnp.exp(jnp.sum(lv[2:3] * lv[3:4], axis=-1, keepdims=True)) + lam_init)
    for hh in range(heads):
        sl = slice(hh * width, (hh + 1) * width)
        q = _rms_halves(q_ref[:, sl], qn_ref[...], lo)
        k = _rms_halves(k_ref[:, sl], kn_ref[...], lo)
        v = v_ref[:, sl]
        if latent:
            cos, sin = cos_ref[...], sin_ref[...]
            q = _rope(q, cos, sin, dh // 4)
            k = _rope(k, cos, sin, dh // 4)
            key_parts = [k.astype(BF16), kc_ref[:, sl].astype(BF16)]
            val_parts = [v.astype(BF16), vc_ref[:, sl].astype(BF16)]
        else:
            ko_ref[:, sl] = k
            vo_ref[:, sl] = v
            key_parts = [k.astype(BF16)]
            val_parts = [v.astype(BF16)]
        q = q * scale
        o = None
        for comp in range(2):
            qc = jnp.where(lo, q, 0.0) if comp == 0 else jnp.where(lo, 0.0, q)
            qc = qc.astype(BF16)
            scores = [_dot_nt(qc, kp) for kp in key_parts]
            mx = None
            for s in scores:
                m = jnp.max(s, axis=-1, keepdims=True)
                mx = m if mx is None else jnp.maximum(mx, m)
            acc = den = None
            for s, vp in zip(scores, val_parts):
                p = jnp.exp(s - mx)
                d = jnp.sum(p, axis=-1, keepdims=True)
                t = _dot(p.astype(BF16), vp)
                den = d if den is None else den + d
                acc = t if acc is None else acc + t
            oc = acc * (1.0 / den)
            o = oc if comp == 0 else o - lam * oc
        o = _rms_rows(o, sub_ref[...]) * (1.0 - lam_init)
        o_ref[:, sl] = o.astype(BF16)


def _attn_d(z, row_blk0, batch, length, q_col, k_col, v_col, qn, kn, lam_vecs, subln, lam_init, ctx=None):
    width = (k_col - q_col) // D_HEADS
    dh = width // 2
    latent = ctx is not None
    heads = 1 if latent else 4
    bw = heads * width
    steps = D_HEADS // heads
    assert q_col % bw == 0 and k_col % bw == 0 and v_col % bw == 0

    def zspec(col):
        return pl.BlockSpec((length, bw), lambda b, h: (row_blk0 + b, col // bw + h))

    row = pl.BlockSpec((1, width), lambda b, h: (0, 0))
    in_specs = [zspec(q_col), zspec(k_col), zspec(v_col), row, row,
                pl.BlockSpec((4, dh), lambda b, h: (0, 0)), row]
    args = [z, z, z, jnp.tile(qn, 2).reshape(1, width), jnp.tile(kn, 2).reshape(1, width), lam_vecs,
            subln.reshape(1, width)]
    o_spec = pl.BlockSpec((length, bw), lambda b, h: (b, h))
    o_shape = jax.ShapeDtypeStruct((batch * length, D_HEADS * width), BF16)
    if latent:
        k_ctx, v_ctx = ctx
        past = k_ctx.shape[1]
        cos, sin = _rope_tables(length, dh, 2)
        tab = pl.BlockSpec((length, width), lambda b, h: (0, 0))
        cache = pl.BlockSpec((None, past, bw), lambda b, h: (b, 0, h))
        in_specs += [tab, tab, cache, cache]
        args += [cos, sin, k_ctx, v_ctx]
        out_specs, out_shape = o_spec, o_shape
    else:
        kv_shape = jax.ShapeDtypeStruct((batch * length, D_HEADS * width), F32)
        out_specs, out_shape = (o_spec, o_spec, o_spec), (o_shape, kv_shape, kv_shape)
    return pl.pallas_call(
        functools.partial(_attn_d_kernel, latent=latent, heads=heads, scale=dh ** -0.5, lam_init=lam_init),
        grid=(batch, steps),
        in_specs=in_specs,
        out_specs=out_specs,
        out_shape=out_shape,
        compiler_params=_cparams(2),
        name="attn_d_lat" if latent else "attn_d_ctx",
    )(*args)


def _dft_tables(length):
    m2 = 2 * length
    k = np.arange(length)
    ang = 2.0 * np.pi * ((k[:, None] * k[None, :]) % m2) / m2
    alt = np.where(k % 2 == 0, 1.0, -1.0)
    fr = np.cos(ang)
    fi = -np.sin(ang)
    fi[0, :] = alt
    ir = (2.0 / m2) * np.cos(ang)
    ir[:, 0] = 1.0 / m2
    ii = -(2.0 / m2) * np.sin(ang)
    ii[:, 0] = alt / m2
    exact = tuple(jnp.asarray(t, dtype=F32) for t in (fr, fi))
    rounded = tuple(jnp.asarray(t, dtype=F32).astype(BF16) for t in (fr, fi, ir, ii))
    return exact, rounded


def _hy_filter_kernel(z_ref, wi_ref, bi_ref, wh_ref, bh_ref, wf_ref, wb_ref, fr_ref, dl_ref, fre_ref, fim_ref,
                      kr_ref, ki_ref, kr2_ref):
    length = z_ref.shape[0]
    fr = fr_ref[...]
    h = jnp.sin(fr * (_dot_hi(z_ref[...], wi_ref[...]) + bi_ref[...]))
    for i in range(wh_ref.shape[0]):
        h = jnp.sin(fr * (_dot_hi(h, wh_ref[i]) + bh_ref[i]))
    row = lax.broadcasted_iota(jnp.int32, (length, 1), 0)
    t = row.astype(F32) * (1.0 / (length - 1))
    decay = jnp.exp(-t * dl_ref[...])
    hf = _dot_hi(h, wf_ref[...]) * decay
    hb = jnp.where(row == 0, 0.0, _dot_hi(h, wb_ref[...]) * decay)
    hs = hf + hb
    kr = _dot_hi(fre_ref[...], hs)
    ki = _dot_hi(fim_ref[...], hf - hb)
    alt = 1.0 - 2.0 * (row % 2).astype(F32)
    nyq = jnp.sum(alt * hs, axis=0, keepdims=True)
    kr_ref[...] = kr
    ki_ref[...] = jnp.where(row == 0, 0.0, ki)
    kr2_ref[...] = jnp.where(row == 0, nyq, kr)


def _hy_filter(length, w_in, b_in, w_hid, b_hid, w_out, freq, tables, ct=256):
    hid = w_in.shape[1]
    hw = w_out.shape[1] // 2
    n_inner = w_hid.shape[0]
    t = jnp.linspace(0.0, 1.0, length, dtype=F32)[:, None]
    bands = (HY_EMB - 1) // 2
    w = 2.0 * math.pi * jnp.arange(length, dtype=F32)[:, None] / length
    f = jnp.linspace(1e-4, bands - 1, bands, dtype=F32)[None, :]
    feat = jnp.concatenate([t, jnp.cos(f * w), -jnp.sin(f * w)], axis=-1)
    feat = jnp.pad(feat, ((0, 0), (0, LANES - HY_EMB)))
    w_in = jnp.pad(w_in, ((0, LANES - HY_EMB), (0, 0)))
    deltas = jnp.abs(jnp.linspace(math.log(HY_TARGET) / HY_FAST, math.log(HY_TARGET) / HY_SLOW, hw, dtype=F32))
    fre, fim = tables[0], tables[1]
    nj = hw // ct
    full = lambda shape: pl.BlockSpec(shape, lambda j: (0,) * len(shape))
    out_spec = pl.BlockSpec((length, ct), lambda j: (0, j))
    out_shape = jax.ShapeDtypeStruct((length, hw), F32)
    return pl.pallas_call(
        _hy_filter_kernel,
        grid=(nj,),
        in_specs=[
            full((length, LANES)), full((LANES, hid)), full((1, hid)), full((n_inner, hid, hid)),
            full((n_inner, 1, hid)),
            pl.BlockSpec((hid, ct), lambda j: (0, j)), pl.BlockSpec((hid, ct), lambda j: (0, nj + j)),
            full((1, hid)), pl.BlockSpec((1, ct), lambda j: (0, j)), full((length, length)), full((length, length)),
        ],
        out_specs=(out_spec, out_spec, out_spec),
        out_shape=(out_shape, out_shape, out_shape),
        compiler_params=_cparams(1),
        name="hyena_filter",
    )(feat, w_in, b_in.reshape(1, hid), w_hid, b_hid.reshape(n_inner, 1, hid), w_out, w_out,
      freq.reshape(1, hid), deltas.reshape(1, hw), fre, fim)


def _conv3(x, w, b, row, length):
    prev = jnp.where(row == 0, 0.0, pltpu.roll(x, 1, 0))
    nxt = jnp.where(row == length - 1, 0.0, pltpu.roll(x, length - 1, 0))
    return prev * w[0:1] + x * w[1:2] + nxt * w[2:3] + b


def _hyena_kernel(x0_ref, x1_ref, v_ref, w0_ref, w1_ref, wv_ref, b0_ref, b1_ref, bv_ref, kr_ref, ki_ref, kr2_ref,
                  bias_ref, fre_ref, fim_ref, ire_ref, iim_ref, o_ref):
    length = x0_ref.shape[0]
    row = lax.broadcasted_iota(jnp.int32, (length, 1), 0)
    x0 = _conv3(x0_ref[...], w0_ref[...], b0_ref[...], row, length)
    x1 = _conv3(x1_ref[...], w1_ref[...], b1_ref[...], row, length)
    v = _conv3(v_ref[...], wv_ref[...], bv_ref[...], row, length)
    g = v * x1
    gb = g.astype(BF16)
    gr = _dot(fre_ref[...], gb)
    gi = _dot(fim_ref[...], gb)
    pr = gr * kr_ref[...] - gi * ki_ref[...]
    pi = gr * ki_ref[...] + gi * kr2_ref[...]
    y = _dot(ire_ref[...], pr.astype(BF16)) + _dot(iim_ref[...], pi.astype(BF16)) + g * bias_ref[...]
    o_ref[...] = (y * x0).astype(BF16)


def _hyena(z, row_blk0, batch, length, col0, hw, conv_w, conv_b, filt, bias, tables, ct=256):
    nj = hw // ct

    def zspec(part):
        return pl.BlockSpec((length, ct), lambda j, b: (row_blk0 + b, (col0 + part * hw) // ct + j))

    def wspec(part, rows):
        return pl.BlockSpec((rows, ct), lambda j, b: (0, part * nj + j))

    kspec = pl.BlockSpec((length, ct), lambda j, b: (0, j))
    tab = pl.BlockSpec((length, length), lambda j, b: (0, 0))
    conv_b2 = conv_b.reshape(1, 3 * hw)
    return pl.pallas_call(
        _hyena_kernel,
        grid=(nj, batch),
        in_specs=[zspec(0), zspec(1), zspec(2), wspec(0, 3), wspec(1, 3), wspec(2, 3), wspec(0, 1), wspec(1, 1),
                  wspec(2, 1), kspec, kspec, kspec, pl.BlockSpec((1, ct), lambda j, b: (0, j)), tab, tab, tab, tab],
        out_specs=pl.BlockSpec((length, ct), lambda j, b: (b, j)),
        out_shape=jax.ShapeDtypeStruct((batch * length, hw), BF16),
        compiler_params=_cparams(2),
        name="hyena_conv",
    )(z, z, z, conv_w, conv_w, conv_w, conv_b2, conv_b2, conv_b2, filt[0], filt[1], filt[2], bias.reshape(1, hw),
      *tables)


S5_BCHUNK = 256
S5_CCHUNK = 256
S5_SCAN_COLS = 1024


def _s5_discretize(lam_re, lam_im, log_step, b_re, b_im):
    dt = jnp.exp(log_step.astype(F32))[..., None]
    lr, li = lam_re.astype(F32), lam_im.astype(F32)
    mag = jnp.exp(lr * dt)
    a_re, a_im = mag * jnp.cos(li * dt), mag * jnp.sin(li * dt)
    den = lr * lr + li * li
    q_re = ((a_re - 1.0) * lr + a_im * li) / den
    q_im = (a_im * lr - (a_re - 1.0) * li) / den
    qr, qi = q_re[..., None], q_im[..., None]
    br, bi = b_re.astype(F32), b_im.astype(F32)
    return a_re, a_im, qr * br - qi * bi, qr * bi + qi * br


def _s5_weights(lam_re, lam_im, log_step, b_re, b_im, c_re, c_im):
    a_re, a_im, bb_re, bb_im = _s5_discretize(lam_re, lam_im, log_step, b_re, b_im)
    n_dir, groups, n_state, ch = bb_re.shape
    ns = groups * n_state
    a = jnp.concatenate([a_re.reshape(n_dir, 1, ns), a_im.reshape(n_dir, 1, ns)], axis=-1)
    a = jnp.broadcast_to(a, (n_dir, SUBLANES, 2 * ns))
    gb = S5_BCHUNK // ch
    nb = groups // gb
    eye_b = jnp.eye(gb, dtype=F32)

    def wb_of(bb):
        t = bb.reshape(n_dir, nb, gb, n_state, ch)
        t = jnp.einsum('dkgnc,gh->dkgchn', t, eye_b)
        return t.reshape(n_dir, nb, gb * ch, gb * n_state)

    wb = jnp.concatenate([wb_of(bb_re), wb_of(bb_im)], axis=-1).astype(BF16)
    gc = S5_CCHUNK // n_state
    nc = groups // gc
    per_blk = S5_CCHUNK // (gc * ch)
    eye_c = jnp.eye(gc, dtype=F32)
    place = jax.nn.one_hot(jnp.arange(nc) % per_blk, per_blk, dtype=F32)

    def wc_of(cc):
        t = cc.astype(F32).reshape(n_dir, nc, gc, ch, n_state)
        t = jnp.einsum('dkgcn,gh->dkgnhc', t, eye_c).reshape(n_dir, nc, gc * n_state, gc * ch)
        t = jnp.einsum('dkrc,kj->dkrjc', t, place)
        return t.reshape(n_dir, nc, gc * n_state, per_blk * gc * ch)

    wc = jnp.concatenate([wc_of(c_re), -wc_of(c_im)], axis=2).astype(BF16)
    return a, wb, wc


def _s5_kernel(u_ref, wb_ref, wc_ref, a_ref, h0_ref, y_ref, hf_ref, bu_ref, h_ref, *, tc, n_chunks):
    d = pl.program_id(0)
    c = pl.program_id(2)
    ns = h_ref.shape[1] // 2
    width = u_ref.shape[2]

    @pl.when(c == 0)
    def _():
        h_ref[...] = h0_ref[...]

    u = u_ref[...].reshape(tc * SUBLANES, width).astype(BF16)
    nb = wb_ref.shape[0]
    bcols = wb_ref.shape[2] // 2
    for kc in range(nb):
        r = _dot(u[:, kc * S5_BCHUNK:(kc + 1) * S5_BCHUNK], wb_ref[kc])
        bu_ref[:, kc * bcols:(kc + 1) * bcols] = r[:, :bcols]
        bu_ref[:, ns + kc * bcols:ns + (kc + 1) * bcols] = r[:, bcols:]

    for cc in range(ns // S5_SCAN_COLS):
        re = slice(cc * S5_SCAN_COLS, (cc + 1) * S5_SCAN_COLS)
        im = slice(ns + cc * S5_SCAN_COLS, ns + (cc + 1) * S5_SCAN_COLS)
        ar, ai = a_ref[:, re], a_ref[:, im]

        def body(t, carry, re=re, im=im, ar=ar, ai=ai):
            hr, hi = carry
            te = t + d * (tc - 1 - 2 * t)
            rows = pl.ds(pl.multiple_of(te * SUBLANES, SUBLANES), SUBLANES)
            nr = ar * hr - ai * hi + bu_ref[rows, re]
            ni = ar * hi + ai * hr + bu_ref[rows, im]
            bu_ref[rows, re] = nr
            bu_ref[rows, im] = ni
            return nr, ni

        hr, hi = lax.fori_loop(0, tc, body, (h_ref[:, re], h_ref[:, im]), unroll=4)
        h_ref[:, re] = hr
        h_ref[:, im] = hi

    nc = wc_ref.shape[0]
    ow = wc_ref.shape[2]
    per_blk = nc // (width // ow)
    for blk in range(width // ow):
        acc = None
        for kk in range(per_blk):
            k = blk * per_blk + kk
            hre = bu_ref[:, k * S5_CCHUNK:(k + 1) * S5_CCHUNK].astype(BF16)
            him = bu_ref[:, ns + k * S5_CCHUNK:ns + (k + 1) * S5_CCHUNK].astype(BF16)
            t = _dot(hre, wc_ref[k, :S5_CCHUNK, :]) + _dot(him, wc_ref[k, S5_CCHUNK:, :])
            acc = t if acc is None else acc + t
        y_ref[:, :, blk * ow:(blk + 1) * ow] = acc.reshape(tc, SUBLANES, ow)

    @pl.when(c == n_chunks - 1)
    def _():
        hf_ref[...] = h_ref[...]


def _s5_scan(u_t, a, wb, wc, h0, tc=64):
    length, bp, width = u_t.shape
    n_chunks = length // tc
    ns2 = a.shape[2]

    def tmap(d, g, c):
        return c + d * (n_chunks - 1 - 2 * c)

    return pl.pallas_call(
        functools.partial(_s5_kernel, tc=tc, n_chunks=n_chunks),
        grid=(2, bp // SUBLANES, n_chunks),
        in_specs=[
            pl.BlockSpec((tc, SUBLANES, width), lambda d, g, c: (tmap(d, g, c), g, 0)),
            pl.BlockSpec((None,) + wb.shape[1:], lambda d, g, c: (d, 0, 0, 0)),
            pl.BlockSpec((None,) + wc.shape[1:], lambda d, g, c: (d, 0, 0, 0)),
            pl.BlockSpec((None, SUBLANES, ns2), lambda d, g, c: (d, 0, 0)),
            pl.BlockSpec((None, SUBLANES, ns2), lambda d, g, c: (d, g, 0)),
        ],
        out_specs=(
            pl.BlockSpec((None, tc, SUBLANES, width), lambda d, g, c: (d, tmap(d, g, c), g, 0)),
            pl.BlockSpec((None, SUBLANES, ns2), lambda d, g, c: (d, g, 0)),
        ),
        out_shape=(jax.ShapeDtypeStruct((2, length, bp, width), F32), jax.ShapeDtypeStruct((2, bp, ns2), F32)),
        scratch_shapes=[pltpu.VMEM((tc * SUBLANES, ns2), F32), pltpu.VMEM((SUBLANES, ns2), F32)],
        compiler_params=_cparams(3),
        name="s5_scan",
    )(u_t, wb, wc, a, h0)


def _s5_glu_kernel(yf_ref, yb_ref, u_ref, d_ref, w_ref, b_ref, o_ref):
    y = u_ref[...] * d_ref[...] + yf_ref[...] + yb_ref[...]
    gy = 0.5 * y * (1.0 + jnp.tanh(math.sqrt(2.0 / math.pi) * (y + 0.044715 * (y * y * y))))
    r = _dot(gy.astype(BF16), w_ref[...]) + b_ref[...]
    half = r.shape[1] // 2
    o_ref[...] = (r[:, :half] * _sigmoid(r[:, half:])).astype(BF16)


def _s5_glu(y2, u, d, glu_w, glu_b, tm=512):
    rows, width = u.shape
    row_spec = pl.BlockSpec((tm, width), lambda i: (i, 0))
    return pl.pallas_call(
        _s5_glu_kernel,
        grid=(rows // tm,),
        in_specs=[
            pl.BlockSpec((None, tm, width), lambda i: (0, i, 0)),
            pl.BlockSpec((None, tm, width), lambda i: (1, i, 0)),
            row_spec,
            pl.BlockSpec((1, width), lambda i: (0, 0)),
            pl.BlockSpec((width, 2 * width), lambda i: (0, 0)),
            pl.BlockSpec((1, 2 * width), lambda i: (0, 0)),
        ],
        out_specs=row_spec,
        out_shape=jax.ShapeDtypeStruct((rows, width), BF16),
        compiler_params=_cparams(1),
        name="s5_glu",
    )(y2, y2, u, d.reshape(1, width), glu_w, glu_b.reshape(1, 2 * width))


def _mixer_c(su, batch, length, s5w, d, glu_w, glu_b, h0):
    a, wb, wc = s5w
    width = su.shape[1]
    bp = -(-batch // SUBLANES) * SUBLANES
    u_t = jnp.transpose(su.reshape(batch, length, width), (1, 0, 2))
    if bp != batch:
        u_t = jnp.pad(u_t, ((0, 0), (0, bp - batch), (0, 0)))
        h0 = jnp.pad(h0, ((0, 0), (0, bp - batch), (0, 0)))
    y2, hf = _s5_scan(u_t, a, wb, wc, h0)
    oc_t = _s5_glu(y2.reshape(2, length * bp, width), u_t.reshape(length * bp, width), d, glu_w, glu_b)
    oc = jnp.transpose(oc_t.reshape(length, bp, width)[:, :batch], (1, 0, 2)).reshape(batch * length, width)
    return oc, hf[:, :batch]


def kernel(x_prompt, x_sample, c, cache_a_k, cache_a_v, cache_d_k, cache_d_v, state_ssm, c_ctx, ada_w, ada_b, norm_g, ffn_w_gu, ffn_w_d, w_in, a_q_norm, a_k_norm, a_sink, hy_conv_w, hy_conv_b, hy_w_in, hy_b_in, hy_w_hid, hy_b_hid, hy_w_out, hy_freq, hy_bias, s5_lam_re, s5_lam_im, s5_log_step, s5_b_re, s5_b_im, s5_c_re, s5_c_im, s5_d, s5_glu_w, s5_glu_b, d_q_norm, d_k_norm, d_lambda, d_subln, w_branch, w_out):
    bc, lc, dm = x_prompt.shape
    bl, ll, _ = x_sample.shape
    depth = ada_w.shape[0]
    mix = dm // 4
    n_ctx, n_lat = bc * lc, bl * ll
    m = n_ctx + n_lat
    assert n_ctx % ll == 0 and bl + 1 <= N_COND_ROWS
    tok = _Tokens(n_ctx, ll, m)
    past = cache_a_k.shape[2]
    a_kv = cache_a_k.shape[3]
    groups, n_state = s5_lam_re.shape[2], s5_lam_re.shape[3]
    ns = groups * n_state
    d_width = cache_d_k.shape[4]

    splits = (mix, a_kv * A_HEAD_DIM, a_kv * A_HEAD_DIM, 3 * mix, mix, mix, mix, mix, N_BRANCH * dm)
    offs = [0]
    for s in splits:
        offs.append(offs[-1] + s)
    c_aq, c_ak, c_av, c_hy, c_su, c_dq, c_dk, c_dv, c_gt = offs[:9]

    tm = 512 if (n_ctx % 1024 or n_lat % 1024) else 1024

    x = (x_prompt.reshape(n_ctx, dm), x_sample.reshape(n_lat, dm))
    cond = jnp.zeros((N_COND_ROWS, dm), F32).at[0].set(c_ctx).at[1:1 + bl].set(c)
    mod = _ada(cond, ada_w, ada_b)

    w_br_b = w_branch.astype(BF16)
    w_out_b = w_out.astype(BF16)
    glu_w_b = s5_glu_w.astype(BF16)

    tabs_c, tabs_cb = _dft_tables(lc)
    tabs_l, tabs_lb = _dft_tables(ll)
    ctx_rb = 0
    lat_rb = n_ctx // ll

    new_ak, new_av, new_dk, new_dv, new_ss = [], [], [], [], []
    for l in range(depth):
        modl = mod[l].reshape(N_COND_ROWS, N_MOD, 1, dm)

        def ffn(x, j, k0, row_ranges=(None,)):
            h = _rmsmod(x, norm_g[l, 2 * j], modl, k0, k0 + 1, tok)
            act, w_d_b = _gu(h, ffn_w_gu, ffn_w_d, (l, j), tm, 256)
            outs = [_mm_resid(act, w_d_b, (), x, modl, k0 + 2, 0.5, tok, 512, 512, "ffn_down", rows)
                    for rows in row_ranges]
            return outs[0] if len(outs) == 1 else outs

        x = ffn(x, 0, 0)

        h = _rmsmod(x, norm_g[l, 1], modl, 3, 4, tok)
        z = _mm_ws(h, w_in, (l,), (0, c_gt), F32, tm, 512, "in_proj")
        zg = _mm_ws(h, w_in, (l,), (c_gt, N_BRANCH * dm), BF16, tm, 512, "in_proj_gate")

        oa_c, ak_n, av_n = _attn_a(z, ctx_rb, bc, lc, c_aq, c_ak, c_av, a_q_norm[l], a_k_norm[l], a_sink[l])
        ctx_a = (cache_a_k[:, l].reshape(bl, past, a_kv * A_HEAD_DIM), cache_a_v[:, l].reshape(bl, past, a_kv * A_HEAD_DIM))
        oa_l = _attn_a(z, lat_rb, bl, ll, c_aq, c_ak, c_av, a_q_norm[l], a_k_norm[l], a_sink[l], ctx_a)
        new_ak.append(ak_n.reshape(bc, lc, a_kv, A_HEAD_DIM))
        new_av.append(av_n.reshape(bc, lc, a_kv, A_HEAD_DIM))

        hy_args = (hy_w_in[l], hy_b_in[l], hy_w_hid[l], hy_b_hid[l], hy_w_out[l], hy_freq[l])
        filt_c = _hy_filter(lc, *hy_args, tabs_c)
        filt_l = _hy_filter(ll, *hy_args, tabs_l)
        ob_c = _hyena(z, ctx_rb, bc, lc, c_hy, mix, hy_conv_w[l], hy_conv_b[l], filt_c, hy_bias[l], tabs_cb)
        ob_l = _hyena(z, lat_rb, bl, ll, c_hy, mix, hy_conv_w[l], hy_conv_b[l], filt_l, hy_bias[l], tabs_lb)

        s5w = _s5_weights(s5_lam_re[l], s5_lam_im[l], s5_log_step[l], s5_b_re[l], s5_b_im[l], s5_c_re[l], s5_c_im[l])
        su = z[:, c_su:c_su + mix]
        h0_c = jnp.zeros((2, bc, 2 * ns), F32)
        st = state_ssm[:, l]
        h0_l = jnp.transpose(st, (1, 0, 4, 2, 3)).reshape(2, bl, 2 * ns)
        oc_c, hf = _mixer_c(su[:n_ctx], bc, lc, s5w, s5_d[l], glu_w_b[l], s5_glu_b[l], h0_c)
        oc_l, _ = _mixer_c(su[n_ctx:], bl, ll, s5w, s5_d[l], glu_w_b[l], s5_glu_b[l], h0_l)
        new_ss.append(jnp.transpose(hf.reshape(2, bc, 2, groups, n_state), (1, 0, 3, 4, 2)))

        lam_init = 0.8 - 0.6 * math.exp(-0.3 * l)
        d_args = (d_q_norm[l], d_k_norm[l], d_lambda[l], d_subln[l], lam_init)
        od_c, dk_n, dv_n = _attn_d(z, ctx_rb, bc, lc, c_dq, c_dk, c_dv, *d_args)
        ctx_d = (cache_d_k[:, l].reshape(bl, past, D_HEADS * d_width), cache_d_v[:, l].reshape(bl, past, D_HEADS * d_width))
        od_l = _attn_d(z, lat_rb, bl, ll, c_dq, c_dk, c_dv, *d_args, ctx_d)
        new_dk.append(dk_n.reshape(bc, lc, D_HEADS, d_width))
        new_dv.append(dv_n.reshape(bc, lc, D_HEADS, d_width))

        s = _branch((oa_c, ob_c, oc_c, od_c), (oa_l, ob_l, oc_l, od_l), w_br_b, l, zg, tok, tm, 512)
        x = _mm_resid(s, w_out_b, (l,), x, modl, 5, 1.0, tok, tm, 512, "out_proj")

        if l + 1 < depth:
            x = ffn(x, 1, 6)
        else:
            y_prompt, y_sample = ffn(x, 1, 6, ((0, n_ctx), (n_ctx, n_lat)))

    y_prompt = y_prompt.reshape(bc, lc, dm)
    y_sample = y_sample.reshape(bl, ll, dm)
    return (y_prompt, y_sample, jnp.stack(new_ak, axis=1), jnp.stack(new_av, axis=1), jnp.stack(new_dk, axis=1),
            jnp.stack(new_dv, axis=1), jnp.stack(new_ss, axis=1))
```

```python
import functools
import math

import numpy as np
import jax
import jax.numpy as jnp
from jax import lax
from jax.experimental import pallas as pl
from jax.experimental.pallas import tpu as pltpu

F32 = jnp.float32
BF16 = jnp.bfloat16
HIGHEST = lax.Precision.HIGHEST

GRID_W = 64
WINDOW = 128
ROPE_BASE = 10000.0
EPS = 1e-6
NEG_INF = -1e30
N_BRANCH = 4
A_HEAD_DIM = 128
A_GROUPS = 4
HY_EMB = 33
HY_TARGET = 1e-2
HY_FAST = 0.3
HY_SLOW = 1.5
S5_CH = 16
S5_STATE = 64
D_HEADS = 8
N_MOD = 9
N_COND_ROWS = 8

V7X_VMEM_BYTES = 64 * 1024 * 1024
VMEM_LIMIT = V7X_VMEM_BYTES - 8 * 1024 * 1024
LANES = 128
SUBLANES = 8


def _cparams(n_axes):
    return pltpu.CompilerParams(dimension_semantics=("arbitrary",) * n_axes, vmem_limit_bytes=VMEM_LIMIT)


def _sigmoid(x):
    return 1.0 / (1.0 + jnp.exp(-x))


def _dot(a, b):
    return jnp.dot(a, b, preferred_element_type=F32)


def _dot_nt(a, b):
    return lax.dot_general(a, b, (((1,), (1,)), ((), ())), preferred_element_type=F32)


def _dot_hi(a, b):
    return jnp.dot(a, b, preferred_element_type=F32, precision=HIGHEST)


def _ada_kernel(c_ref, w_ref, b_ref, o_ref):
    c = c_ref[...]
    s = (c * _sigmoid(c)).astype(BF16)
    o_ref[...] = _dot(s, w_ref[...].astype(BF16)) + b_ref[...]


def _ada(cond, ada_w, ada_b, tn=512):
    depth, d, n = ada_w.shape
    return pl.pallas_call(
        _ada_kernel,
        grid=(depth, n // tn),
        in_specs=[
            pl.BlockSpec((N_COND_ROWS, d), lambda l, j: (0, 0)),
            pl.BlockSpec((None, d, tn), lambda l, j: (l, 0, j)),
            pl.BlockSpec((None, 1, tn), lambda l, j: (l, 0, j)),
        ],
        out_specs=pl.BlockSpec((None, N_COND_ROWS, tn), lambda l, j: (l, 0, j)),
        out_shape=jax.ShapeDtypeStruct((depth, N_COND_ROWS, n), F32),
        compiler_params=_cparams(2),
        name="ada_mod",
    )(cond, ada_w, ada_b.reshape(depth, 1, n))


class _Tokens:
    def __init__(self, n_ctx, l_lat, m):
        self.n_ctx, self.l_lat, self.m = n_ctx, l_lat, m

    def cond(self, i, tm):
        row = i * tm
        return jnp.where(row >= self.n_ctx, (row - self.n_ctx) // self.l_lat + 1, 0)

    def split_specs(self, tm, width, col=None):
        nct = self.n_ctx // tm
        nlt = (self.m - self.n_ctx) // tm

        def cspec(*ids):
            return (jnp.minimum(ids[0], nct - 1), col(*ids) if col else 0)

        def lspec(*ids):
            return (jnp.clip(ids[0] - nct, 0, nlt - 1), col(*ids) if col else 0)

        return pl.BlockSpec((tm, width), cspec), pl.BlockSpec((tm, width), lspec), nct


def _rmsmod_body(x_ref, g_ref, sc_ref, sh_ref, o_ref):
    x = x_ref[...]
    ms = jnp.mean(x * x, axis=-1, keepdims=True)
    y = x * lax.rsqrt(ms + EPS) * g_ref[...]
    o_ref[...] = (y * (1.0 + sc_ref[...]) + sh_ref[...]).astype(BF16)


def _by_group(n_ctx_tiles, body, src_c, src_l, *rest):
    is_ctx = pl.program_id(0) < n_ctx_tiles

    @pl.when(is_ctx)
    def _():
        body(src_c, *rest)

    @pl.when(jnp.logical_not(is_ctx))
    def _():
        body(src_l, *rest)


def _rmsmod(x, gain, modl, k_sh, k_sc, tok, tm=512):
    d = gain.shape[0]
    if isinstance(x, tuple):
        xc_spec, xl_spec, nct = tok.split_specs(tm, d)
        x_specs, x_args = [xc_spec, xl_spec], list(x)
        body = functools.partial(_by_group, nct, _rmsmod_body)
    else:
        x_specs, x_args, body = [pl.BlockSpec((tm, d), lambda i: (i, 0))], [x], _rmsmod_body
    return pl.pallas_call(
        body,
        grid=(tok.m // tm,),
        in_specs=x_specs + [
            pl.BlockSpec((1, d), lambda i: (0, 0)),
            pl.BlockSpec((None, None, 1, d), lambda i: (tok.cond(i, tm), k_sc, 0, 0)),
            pl.BlockSpec((None, None, 1, d), lambda i: (tok.cond(i, tm), k_sh, 0, 0)),
        ],
        out_specs=pl.BlockSpec((tm, d), lambda i: (i, 0)),
        out_shape=jax.ShapeDtypeStruct((tok.m, d), BF16),
        compiler_params=_cparams(1),
        name="rmsmod",
    )(*x_args, gain.reshape(1, d), modl, modl)


def _stream_weight_tiles(copies, cast):
    j = pl.program_id(0)
    nj = pl.num_programs(0)

    @pl.when(pl.program_id(1) == 0)
    def _():
        @pl.when(j == 0)
        def _():
            for cp in copies(0):
                cp.start()

        for cp in copies(j):
            cp.wait()
        cast()

        @pl.when(j + 1 < nj)
        def _():
            for cp in copies(j + 1):
                cp.start()


def _mm_ws_kernel(a_ref, w_hbm, o_ref, wf_ref, wb_ref, sem, *, lead, cb, tn):
    def copies(jj):
        cols = pl.ds(pl.multiple_of((cb + jj) * tn, tn), tn)
        return [pltpu.make_async_copy(w_hbm.at[lead + (slice(None), cols)], wf_ref, sem.at[0])]

    def cast():
        wb_ref[...] = wf_ref[...].astype(BF16)

    _stream_weight_tiles(copies, cast)
    o_ref[...] = _dot(a_ref[...], wb_ref[...]).astype(o_ref.dtype)


def _mm_ws(a, w, lead, cols, out_dtype, tm, tn, name):
    m, k = a.shape
    col0, n = cols
    assert col0 % tn == 0 and n % tn == 0 and m % tm == 0
    return pl.pallas_call(
        functools.partial(_mm_ws_kernel, lead=lead, cb=col0 // tn, tn=tn),
        grid=(n // tn, m // tm),
        in_specs=[pl.BlockSpec((tm, k), lambda j, i: (i, 0)), pl.BlockSpec(memory_space=pl.ANY)],
        out_specs=pl.BlockSpec((tm, tn), lambda j, i: (i, j)),
        out_shape=jax.ShapeDtypeStruct((m, n), out_dtype),
        scratch_shapes=[pltpu.VMEM((k, tn), F32), pltpu.VMEM((k, tn), BF16), pltpu.SemaphoreType.DMA((1,))],
        compiler_params=_cparams(2),
        name=name,
    )(a, w)


def _gu_kernel(h_ref, wgu_hbm, wd_hbm, o_ref, wdb_ref, wgf_ref, wuf_ref, wdf_ref, wgb_ref, wub_ref, sem, *,
               lead, nf, tf):
    def copies(jj):
        gcols = pl.ds(pl.multiple_of(jj * tf, tf), tf)
        ucols = pl.ds(pl.multiple_of((nf + jj) * tf, tf), tf)
        return [
            pltpu.make_async_copy(wgu_hbm.at[lead + (slice(None), gcols)], wgf_ref, sem.at[0]),
            pltpu.make_async_copy(wgu_hbm.at[lead + (slice(None), ucols)], wuf_ref, sem.at[1]),
            pltpu.make_async_copy(wd_hbm.at[lead + (gcols, slice(None))], wdf_ref, sem.at[2]),
        ]

    def cast():
        wgb_ref[...] = wgf_ref[...].astype(BF16)
        wub_ref[...] = wuf_ref[...].astype(BF16)
        wdb_ref[...] = wdf_ref[...].astype(BF16)

    _stream_weight_tiles(copies, cast)
    h = h_ref[...]
    g = _dot(h, wgb_ref[...])
    u = _dot(h, wub_ref[...])
    o_ref[...] = (g * _sigmoid(g) * u).astype(BF16)


def _gu(h, w_gu, w_d, lead, tm, tf):
    m, d = h.shape
    ff = w_gu.shape[-1] // 2
    nf = ff // tf
    assert m % tm == 0
    return pl.pallas_call(
        functools.partial(_gu_kernel, lead=lead, nf=nf, tf=tf),
        grid=(nf, m // tm),
        in_specs=[pl.BlockSpec((tm, d), lambda j, i: (i, 0)), pl.BlockSpec(memory_space=pl.ANY),
                  pl.BlockSpec(memory_space=pl.ANY)],
        out_specs=(pl.BlockSpec((tm, tf), lambda j, i: (i, j)), pl.BlockSpec((tf, d), lambda j, i: (j, 0))),
        out_shape=(jax.ShapeDtypeStruct((m, ff), BF16), jax.ShapeDtypeStruct((ff, d), BF16)),
        scratch_shapes=[pltpu.VMEM((d, tf), F32), pltpu.VMEM((d, tf), F32), pltpu.VMEM((tf, d), F32),
                        pltpu.VMEM((d, tf), BF16), pltpu.VMEM((d, tf), BF16), pltpu.SemaphoreType.DMA((3,))],
        compiler_params=_cparams(2),
        name="ffn_gate_up",
    )(h, w_gu, w_d)


def _mm_resid_body(x_ref, a_ref, w_ref, gate_ref, o_ref, *, coef):
    y = _dot(a_ref[...], w_ref[...])
    o_ref[...] = x_ref[...] + (coef * gate_ref[...]) * y


def _mm_resid(a, w, lead, x, modl, k_gate, coef, tok, tm, tn, name, rows=None):
    m, k = a.shape
    n = w.shape[-1]
    nlead = len(lead)
    row0, m_out = (0, m) if rows is None else rows
    rb = row0 // tm
    body = functools.partial(_mm_resid_body, coef=coef)
    if isinstance(x, tuple):
        assert rows is None
        xc_spec, xl_spec, nct = tok.split_specs(tm, tn, col=lambda i, j: j)
        x_specs, x_args = [xc_spec, xl_spec], list(x)
        body = functools.partial(_by_group, nct, body)
    else:
        x_specs, x_args = [pl.BlockSpec((tm, tn), lambda i, j: (rb + i, j))], [x]
    return pl.pallas_call(
        body,
        grid=(m_out // tm, n // tn),
        in_specs=x_specs + [
            pl.BlockSpec((tm, k), lambda i, j: (rb + i, 0)),
            pl.BlockSpec((None,) * nlead + (k, tn), lambda i, j: lead + (0, j)),
            pl.BlockSpec((None, None, 1, tn), lambda i, j: (tok.cond(rb + i, tm), k_gate, 0, j)),
        ],
        out_specs=pl.BlockSpec((tm, tn), lambda i, j: (i, j)),
        out_shape=jax.ShapeDtypeStruct((m_out, n), F32),
        compiler_params=_cparams(2),
        name=name,
    )(*x_args, a, w, modl)


def _branch_kernel(*refs, n_ctx_tiles):
    o_ctx, o_lat = refs[:N_BRANCH], refs[N_BRANCH:2 * N_BRANCH]
    w_ref = refs[2 * N_BRANCH]
    gates = refs[2 * N_BRANCH + 1:3 * N_BRANCH + 1]
    o_ref = refs[3 * N_BRANCH + 1]

    def body(branches):
        acc = None
        for b in range(N_BRANCH):
            t = _sigmoid(gates[b][...].astype(F32)) * _dot(branches[b][...], w_ref[b])
            acc = t if acc is None else acc + t
        o_ref[...] = acc.astype(BF16)

    is_ctx = pl.program_id(0) < n_ctx_tiles

    @pl.when(is_ctx)
    def _():
        body(o_ctx)

    @pl.when(jnp.logical_not(is_ctx))
    def _():
        body(o_lat)


def _branch(o_ctx, o_lat, w_branch, layer, zg, tok, tm, tn):
    mix = o_ctx[0].shape[1]
    d = w_branch.shape[-1]
    oc_spec, ol_spec, nct = tok.split_specs(tm, mix)

    def gate_spec(b):
        return pl.BlockSpec((tm, tn), lambda i, j: (i, (b * d) // tn + j))

    return pl.pallas_call(
        functools.partial(_branch_kernel, n_ctx_tiles=nct),
        grid=(tok.m // tm, d // tn),
        in_specs=[oc_spec] * N_BRANCH + [ol_spec] * N_BRANCH
        + [pl.BlockSpec((None, N_BRANCH, mix, tn), lambda i, j: (layer, 0, 0, j))]
        + [gate_spec(b) for b in range(N_BRANCH)],
        out_specs=pl.BlockSpec((tm, tn), lambda i, j: (i, j)),
        out_shape=jax.ShapeDtypeStruct((tok.m, d), BF16),
        compiler_params=_cparams(2),
        name="branch_gate",
    )(*o_ctx, *o_lat, w_branch, zg, zg, zg, zg)


def _rope_tables(length, n, reps):
    quarter = n // 4
    n_rows = length // GRID_W
    rows = jnp.repeat(jnp.arange(n_rows), GRID_W).astype(F32)
    cols = jnp.tile(jnp.arange(GRID_W), n_rows).astype(F32)
    freqs = ROPE_BASE ** (-jnp.arange(quarter, dtype=F32) / quarter)
    ang_r = rows[:, None] * freqs[None, :]
    ang_c = cols[:, None] * freqs[None, :]
    cos = jnp.concatenate([jnp.cos(ang_r), jnp.cos(ang_r), jnp.cos(ang_c), jnp.cos(ang_c)], axis=-1)
    sin = jnp.concatenate([-jnp.sin(ang_r), jnp.sin(ang_r), -jnp.sin(ang_c), jnp.sin(ang_c)], axis=-1)
    return jnp.tile(cos, (1, reps)), jnp.tile(sin, (1, reps))


def _rope(x, cos, sin, quarter):
    lanes = x.shape[-1]
    lane = lax.broadcasted_iota(jnp.int32, (1, lanes), 1)
    first = (lane % (2 * quarter)) < quarter
    partner = jnp.where(first, pltpu.roll(x, lanes - quarter, 1), pltpu.roll(x, quarter, 1))
    return x * cos + partner * sin


def _rms_rows(x, gain):
    ms = jnp.mean(x * x, axis=-1, keepdims=True)
    return x * lax.rsqrt(ms + EPS) * gain


def _attn_a_kernel(*refs, latent, scale):
    if latent:
        q_ref, k_ref, v_ref, qn_ref, kn_ref, sink_ref, cos_ref, sin_ref, kc_ref, vc_ref, o_ref = refs
    else:
        q_ref, k_ref, v_ref, qn_ref, kn_ref, sink_ref, o_ref, ko_ref, vo_ref = refs
    length = k_ref.shape[0]
    k = _rms_rows(k_ref[...], kn_ref[...])
    v = v_ref[...]
    if latent:
        cos, sin = cos_ref[...], sin_ref[...]
        k = _rope(k, cos, sin, A_HEAD_DIM // 4)
        kc = kc_ref[...].astype(BF16)
        vc = vc_ref[...].astype(BF16)
        qi = lax.broadcasted_iota(jnp.int32, (length, length), 0)
        ki = lax.broadcasted_iota(jnp.int32, (length, length), 1)
        band = jnp.abs(qi - ki) <= WINDOW
    else:
        ko_ref[...] = k
        vo_ref[...] = v
    kb = k.astype(BF16)
    vb = v.astype(BF16)
    for g in range(A_GROUPS):
        sl = slice(g * A_HEAD_DIM, (g + 1) * A_HEAD_DIM)
        q = _rms_rows(q_ref[:, sl], qn_ref[...])
        if latent:
            q = _rope(q, cos, sin, A_HEAD_DIM // 4)
        qb = (q * scale).astype(BF16)
        sink = sink_ref[:, g * A_HEAD_DIM:g * A_HEAD_DIM + 1]
        s = _dot_nt(qb, kb)
        if latent:
            s = jnp.where(band, s, NEG_INF)
            s2 = _dot_nt(qb, kc)
            mx = jnp.maximum(jnp.maximum(jnp.max(s, axis=-1, keepdims=True), jnp.max(s2, axis=-1, keepdims=True)), sink)
            p2 = jnp.exp(s2 - mx)
        else:
            mx = jnp.maximum(jnp.max(s, axis=-1, keepdims=True), sink)
        p = jnp.exp(s - mx)
        den = jnp.sum(p, axis=-1, keepdims=True) + jnp.exp(sink - mx)
        o = _dot(p.astype(BF16), vb)
        if latent:
            den = den + jnp.sum(p2, axis=-1, keepdims=True)
            o = o + _dot(p2.astype(BF16), vc)
        o_ref[:, sl] = (o / den).astype(BF16)


def _attn_a(z, row_blk0, batch, length, q_col, k_col, v_col, qn, kn, sink, ctx=None):
    kv_heads = (v_col - k_col) // A_HEAD_DIM
    qw = A_GROUPS * A_HEAD_DIM
    latent = ctx is not None
    sink_b = jnp.repeat(sink.reshape(kv_heads, A_GROUPS), A_HEAD_DIM, axis=1).reshape(kv_heads, 1, qw)
    in_specs = [
        pl.BlockSpec((length, qw), lambda b, h: (row_blk0 + b, q_col // qw + h)),
        pl.BlockSpec((length, A_HEAD_DIM), lambda b, h: (row_blk0 + b, k_col // A_HEAD_DIM + h)),
        pl.BlockSpec((length, A_HEAD_DIM), lambda b, h: (row_blk0 + b, v_col // A_HEAD_DIM + h)),
        pl.BlockSpec((1, A_HEAD_DIM), lambda b, h: (0, 0)),
        pl.BlockSpec((1, A_HEAD_DIM), lambda b, h: (0, 0)),
        pl.BlockSpec((None, 1, qw), lambda b, h: (h, 0, 0)),
    ]
    args = [z, z, z, qn.reshape(1, A_HEAD_DIM), kn.reshape(1, A_HEAD_DIM), sink_b]
    o_spec = pl.BlockSpec((length, qw), lambda b, h: (b, h))
    o_shape = jax.ShapeDtypeStruct((batch * length, kv_heads * qw), BF16)
    if latent:
        k_ctx, v_ctx = ctx
        past = k_ctx.shape[1]
        cos, sin = _rope_tables(length, A_HEAD_DIM, 1)
        tab = pl.BlockSpec((length, A_HEAD_DIM), lambda b, h: (0, 0))
        cache = pl.BlockSpec((None, past, A_HEAD_DIM), lambda b, h: (b, 0, h))
        in_specs += [tab, tab, cache, cache]
        args += [cos, sin, k_ctx, v_ctx]
        out_specs, out_shape = o_spec, o_shape
    else:
        kv_spec = pl.BlockSpec((length, A_HEAD_DIM), lambda b, h: (b, h))
        kv_shape = jax.ShapeDtypeStruct((batch * length, kv_heads * A_HEAD_DIM), F32)
        out_specs, out_shape = (o_spec, kv_spec, kv_spec), (o_shape, kv_shape, kv_shape)
    return pl.pallas_call(
        functools.partial(_attn_a_kernel, latent=latent, scale=A_HEAD_DIM ** -0.5),
        grid=(batch, kv_heads),
        in_specs=in_specs,
        out_specs=out_specs,
        out_shape=out_shape,
        compiler_params=_cparams(2),
        name="attn_a_lat" if latent else "attn_a_ctx",
    )(*args)


def _rms_halves(x, gain, lo):
    half = x.shape[-1] // 2
    x2 = x * x
    s_lo = jnp.sum(jnp.where(lo, x2, 0.0), axis=-1, keepdims=True)
    s_hi = jnp.sum(jnp.where(lo, 0.0, x2), axis=-1, keepdims=True)
    ms = jnp.where(lo, s_lo, s_hi) * (1.0 / half)
    return x * lax.rsqrt(ms + EPS) * gain


def _attn_d_kernel(*refs, latent, heads, scale, lam_init):
    if latent:
        q_ref, k_ref, v_ref, qn_ref, kn_ref, lam_ref, sub_ref, cos_ref, sin_ref, kc_ref, vc_ref, o_ref = refs
    else:
        q_ref, k_ref, v_ref, qn_ref, kn_ref, lam_ref, sub_ref, o_ref, ko_ref, vo_ref = refs
    width = qn_ref.shape[1]
    dh = width // 2
    lane = lax.broadcasted_iota(jnp.int32, (1, width), 1)
    lo = lane < dh
    lv = lam_ref[...]
    lam = (jnp.exp(jnp.sum(lv[0:1] * lv[1:2], axis=-1, keepdims=True))
           - jnp.exp(jnp.sum(lv[2:3] * lv[3:4], axis=-1, keepdims=True)) + lam_init)
    for hh in range(heads):
        sl = slice(hh * width, (hh + 1) * width)
        q = _rms_halves(q_ref[:, sl], qn_ref[...], lo)
        k = _rms_halves(k_ref[:, sl], kn_ref[...], lo)
        v = v_ref[:, sl]
        if latent:
            cos, sin = cos_ref[...], sin_ref[...]
            q = _rope(q, cos, sin, dh // 4)
            k = _rope(k, cos, sin, dh // 4)
            key_parts = [k.astype(BF16), kc_ref[:, sl].astype(BF16)]
            val_parts = [v.astype(BF16), vc_ref[:, sl].astype(BF16)]
        else:
            ko_ref[:, sl] = k
            vo_ref[:, sl] = v
            key_parts = [k.astype(BF16)]
            val_parts = [v.astype(BF16)]
        q = q * scale
        o = None
        for comp in range(2):
            qc = jnp.where(lo, q, 0.0) if comp == 0 else jnp.where(lo, 0.0, q)
            qc = qc.astype(BF16)
            scores = [_dot_nt(qc, kp) for kp in key_parts]
            mx = None
            for s in scores:
                m = jnp.max(s, axis=-1, keepdims=True)
                mx = m if mx is None else jnp.maximum(mx, m)
            acc = den = None
            for s, vp in zip(scores, val_parts):
                p = jnp.exp(s - mx)
                d = jnp.sum(p, axis=-1, keepdims=True)
                t = _dot(p.astype(BF16), vp)
                den = d if den is None else den + d
                acc = t if acc is None else acc + t
            oc = acc * (1.0 / den)
            o = oc if comp == 0 else o - lam * oc
        o = _rms_rows(o, sub_ref[...]) * (1.0 - lam_init)
        o_ref[:, sl] = o.astype(BF16)


def _attn_d(z, row_blk0, batch, length, q_col, k_col, v_col, qn, kn, lam_vecs, subln, lam_init, ctx=None):
    width = (k_col - q_col) // D_HEADS
    dh = width // 2
    latent = ctx is not None
    heads = 1 if latent else 4
    bw = heads * width
    steps = D_HEADS // heads
    assert q_col % bw == 0 and k_col % bw == 0 and v_col % bw == 0

    def zspec(col):
        return pl.BlockSpec((length, bw), lambda b, h: (row_blk0 + b, col // bw + h))

    row = pl.BlockSpec((1, width), lambda b, h: (0, 0))
    in_specs = [zspec(q_col), zspec(k_col), zspec(v_col), row, row,
                pl.BlockSpec((4, dh), lambda b, h: (0, 0)), row]
    args = [z, z, z, jnp.tile(qn, 2).reshape(1, width), jnp.tile(kn, 2).reshape(1, width), lam_vecs,
            subln.reshape(1, width)]
    o_spec = pl.BlockSpec((length, bw), lambda b, h: (b, h))
    o_shape = jax.ShapeDtypeStruct((batch * length, D_HEADS * width), BF16)
    if latent:
        k_ctx, v_ctx = ctx
        past = k_ctx.shape[1]
        cos, sin = _rope_tables(length, dh, 2)
        tab = pl.BlockSpec((length, width), lambda b, h: (0, 0))
        cache = pl.BlockSpec((None, past, bw), lambda b, h: (b, 0, h))
        in_specs += [tab, tab, cache, cache]
        args += [cos, sin, k_ctx, v_ctx]
        out_specs, out_shape = o_spec, o_shape
    else:
        kv_shape = jax.ShapeDtypeStruct((batch * length, D_HEADS * width), F32)
        out_specs, out_shape = (o_spec, o_spec, o_spec), (o_shape, kv_shape, kv_shape)
    return pl.pallas_call(
        functools.partial(_attn_d_kernel, latent=latent, heads=heads, scale=dh ** -0.5, lam_init=lam_init),
        grid=(batch, steps),
        in_specs=in_specs,
        out_specs=out_specs,
        out_shape=out_shape,
        compiler_params=_cparams(2),
        name="attn_d_lat" if latent else "attn_d_ctx",
    )(*args)


def _dft_tables(length):
    m2 = 2 * length
    k = np.arange(length)
    ang = 2.0 * np.pi * ((k[:, None] * k[None, :]) % m2) / m2
    alt = np.where(k % 2 == 0, 1.0, -1.0)
    fr = np.cos(ang)
    fi = -np.sin(ang)
    fi[0, :] = alt
    ir = (2.0 / m2) * np.cos(ang)
    ir[:, 0] = 1.0 / m2
    ii = -(2.0 / m2) * np.sin(ang)
    ii[:, 0] = alt / m2
    exact = tuple(jnp.asarray(t, dtype=F32) for t in (fr, fi))
    rounded = tuple(jnp.asarray(t, dtype=F32).astype(BF16) for t in (fr, fi, ir, ii))
    return exact, rounded


def _hy_filter_kernel(z_ref, wi_ref, bi_ref, wh_ref, bh_ref, wf_ref, wb_ref, fr_ref, dl_ref, fre_ref, fim_ref,
                      kr_ref, ki_ref, kr2_ref):
    length = z_ref.shape[0]
    fr = fr_ref[...]
    h = jnp.sin(fr * (_dot_hi(z_ref[...], wi_ref[...]) + bi_ref[...]))
    for i in range(wh_ref.shape[0]):
        h = jnp.sin(fr * (_dot_hi(h, wh_ref[i]) + bh_ref[i]))
    row = lax.broadcasted_iota(jnp.int32, (length, 1), 0)
    t = row.astype(F32) * (1.0 / (length - 1))
    decay = jnp.exp(-t * dl_ref[...])
    hf = _dot_hi(h, wf_ref[...]) * decay
    hb = jnp.where(row == 0, 0.0, _dot_hi(h, wb_ref[...]) * decay)
    hs = hf + hb
    kr = _dot_hi(fre_ref[...], hs)
    ki = _dot_hi(fim_ref[...], hf - hb)
    alt = 1.0 - 2.0 * (row % 2).astype(F32)
    nyq = jnp.sum(alt * hs, axis=0, keepdims=True)
    kr_ref[...] = kr
    ki_ref[...] = jnp.where(row == 0, 0.0, ki)
    kr2_ref[...] = jnp.where(row == 0, nyq, kr)


def _hy_filter(length, w_in, b_in, w_hid, b_hid, w_out, freq, tables, ct=256):
    hid = w_in.shape[1]
    hw = w_out.shape[1] // 2
    n_inner = w_hid.shape[0]
    t = jnp.linspace(0.0, 1.0, length, dtype=F32)[:, None]
    bands = (HY_EMB - 1) // 2
    w = 2.0 * math.pi * jnp.arange(length, dtype=F32)[:, None] / length
    f = jnp.linspace(1e-4, bands - 1, bands, dtype=F32)[None, :]
    feat = jnp.concatenate([t, jnp.cos(f * w), -jnp.sin(f * w)], axis=-1)
    feat = jnp.pad(feat, ((0, 0), (0, LANES - HY_EMB)))
    w_in = jnp.pad(w_in, ((0, LANES - HY_EMB), (0, 0)))
    deltas = jnp.abs(jnp.linspace(math.log(HY_TARGET) / HY_FAST, math.log(HY_TARGET) / HY_SLOW, hw, dtype=F32))
    fre, fim = tables[0], tables[1]
    nj = hw // ct
    full = lambda shape: pl.BlockSpec(shape, lambda j: (0,) * len(shape))
    out_spec = pl.BlockSpec((length, ct), lambda j: (0, j))
    out_shape = jax.ShapeDtypeStruct((length, hw), F32)
    return pl.pallas_call(
        _hy_filter_kernel,
        grid=(nj,),
        in_specs=[
            full((length, LANES)), full((LANES, hid)), full((1, hid)), full((n_inner, hid, hid)),
            full((n_inner, 1, hid)),
            pl.BlockSpec((hid, ct), lambda j: (0, j)), pl.BlockSpec((hid, ct), lambda j: (0, nj + j)),
            full((1, hid)), pl.BlockSpec((1, ct), lambda j: (0, j)), full((length, length)), full((length, length)),
        ],
        out_specs=(out_spec, out_spec, out_spec),
        out_shape=(out_shape, out_shape, out_shape),
        compiler_params=_cparams(1),
        name="hyena_filter",
    )(feat, w_in, b_in.reshape(1, hid), w_hid, b_hid.reshape(n_inner, 1, hid), w_out, w_out,
      freq.reshape(1, hid), deltas.reshape(1, hw), fre, fim)


def _conv3(x, w, b, row, length):
    prev = jnp.where(row == 0, 0.0, pltpu.roll(x, 1, 0))
    nxt = jnp.where(row == length - 1, 0.0, pltpu.roll(x, length - 1, 0))
    return prev * w[0:1] + x * w[1:2] + nxt * w[2:3] + b


def _hyena_kernel(x0_ref, x1_ref, v_ref, w0_ref, w1_ref, wv_ref, b0_ref, b1_ref, bv_ref, kr_ref, ki_ref, kr2_ref,
                  bias_ref, fre_ref, fim_ref, ire_ref, iim_ref, o_ref):
    length = x0_ref.shape[0]
    row = lax.broadcasted_iota(jnp.int32, (length, 1), 0)
    x0 = _conv3(x0_ref[...], w0_ref[...], b0_ref[...], row, length)
    x1 = _conv3(x1_ref[...], w1_ref[...], b1_ref[...], row, length)
    v = _conv3(v_ref[...], wv_ref[...], bv_ref[...], row, length)
    g = v * x1
    gb = g.astype(BF16)
    gr = _dot(fre_ref[...], gb)
    gi = _dot(fim_ref[...], gb)
    pr = gr * kr_ref[...] - gi * ki_ref[...]
    pi = gr * ki_ref[...] + gi * kr2_ref[...]
    y = _dot(ire_ref[...], pr.astype(BF16)) + _dot(iim_ref[...], pi.astype(BF16)) + g * bias_ref[...]
    o_ref[...] = (y * x0).astype(BF16)


def _hyena(z, row_blk0, batch, length, col0, hw, conv_w, conv_b, filt, bias, tables, ct=256):
    nj = hw // ct

    def zspec(part):
        return pl.BlockSpec((length, ct), lambda j, b: (row_blk0 + b, (col0 + part * hw) // ct + j))

    def wspec(part, rows):
        return pl.BlockSpec((rows, ct), lambda j, b: (0, part * nj + j))

    kspec = pl.BlockSpec((length, ct), lambda j, b: (0, j))
    tab = pl.BlockSpec((length, length), lambda j, b: (0, 0))
    conv_b2 = conv_b.reshape(1, 3 * hw)
    return pl.pallas_call(
        _hyena_kernel,
        grid=(nj, batch),
        in_specs=[zspec(0), zspec(1), zspec(2), wspec(0, 3), wspec(1, 3), wspec(2, 3), wspec(0, 1), wspec(1, 1),
                  wspec(2, 1), kspec, kspec, kspec, pl.BlockSpec((1, ct), lambda j, b: (0, j)), tab, tab, tab, tab],
        out_specs=pl.BlockSpec((length, ct), lambda j, b: (b, j)),
        out_shape=jax.ShapeDtypeStruct((batch * length, hw), BF16),
        compiler_params=_cparams(2),
        name="hyena_conv",
    )(z, z, z, conv_w, conv_w, conv_w, conv_b2, conv_b2, conv_b2, filt[0], filt[1], filt[2], bias.reshape(1, hw),
      *tables)


S5_BCHUNK = 256
S5_CCHUNK = 256
S5_SCAN_COLS = 1024


def _s5_discretize(lam_re, lam_im, log_step, b_re, b_im):
    dt = jnp.exp(log_step.astype(F32))[..., None]
    lr, li = lam_re.astype(F32), lam_im.astype(F32)
    mag = jnp.exp(lr * dt)
    a_re, a_im = mag * jnp.cos(li * dt), mag * jnp.sin(li * dt)
    den = lr * lr + li * li
    q_re = ((a_re - 1.0) * lr + a_im * li) / den
    q_im = (a_im * lr - (a_re - 1.0) * li) / den
    qr, qi = q_re[..., None], q_im[..., None]
    br, bi = b_re.astype(F32), b_im.astype(F32)
    return a_re, a_im, qr * br - qi * bi, qr * bi + qi * br


def _s5_weights(lam_re, lam_im, log_step, b_re, b_im, c_re, c_im):
    a_re, a_im, bb_re, bb_im = _s5_discretize(lam_re, lam_im, log_step, b_re, b_im)
    n_dir, groups, n_state, ch = bb_re.shape
    ns = groups * n_state
    a = jnp.concatenate([a_re.reshape(n_dir, 1, ns), a_im.reshape(n_dir, 1, ns)], axis=-1)
    a = jnp.broadcast_to(a, (n_dir, SUBLANES, 2 * ns))
    gb = S5_BCHUNK // ch
    nb = groups // gb
    eye_b = jnp.eye(gb, dtype=F32)

    def wb_of(bb):
        t = bb.reshape(n_dir, nb, gb, n_state, ch)
        t = jnp.einsum('dkgnc,gh->dkgchn', t, eye_b)
        return t.reshape(n_dir, nb, gb * ch, gb * n_state)

    wb = jnp.concatenate([wb_of(bb_re), wb_of(bb_im)], axis=-1).astype(BF16)
    gc = S5_CCHUNK // n_state
    nc = groups // gc
    per_blk = S5_CCHUNK // (gc * ch)
    eye_c = jnp.eye(gc, dtype=F32)
    place = jax.nn.one_hot(jnp.arange(nc) % per_blk, per_blk, dtype=F32)

    def wc_of(cc):
        t = cc.astype(F32).reshape(n_dir, nc, gc, ch, n_state)
        t = jnp.einsum('dkgcn,gh->dkgnhc', t, eye_c).reshape(n_dir, nc, gc * n_state, gc * ch)
        t = jnp.einsum('dkrc,kj->dkrjc', t, place)
        return t.reshape(n_dir, nc, gc * n_state, per_blk * gc * ch)

    wc = jnp.concatenate([wc_of(c_re), -wc_of(c_im)], axis=2).astype(BF16)
    return a, wb, wc


def _s5_kernel(u_ref, wb_ref, wc_ref, a_ref, h0_ref, y_ref, hf_ref, bu_ref, h_ref, *, tc, n_chunks):
    d = pl.program_id(0)
    c = pl.program_id(2)
    ns = h_ref.shape[1] // 2
    width = u_ref.shape[2]

    @pl.when(c == 0)
    def _():
        h_ref[...] = h0_ref[...]

    u = u_ref[...].reshape(tc * SUBLANES, width).astype(BF16)
    nb = wb_ref.shape[0]
    bcols = wb_ref.shape[2] // 2
    for kc in range(nb):
        r = _dot(u[:, kc * S5_BCHUNK:(kc + 1) * S5_BCHUNK], wb_ref[kc])
        bu_ref[:, kc * bcols:(kc + 1) * bcols] = r[:, :bcols]
        bu_ref[:, ns + kc * bcols:ns + (kc + 1) * bcols] = r[:, bcols:]

    for cc in range(ns // S5_SCAN_COLS):
        re = slice(cc * S5_SCAN_COLS, (cc + 1) * S5_SCAN_COLS)
        im = slice(ns + cc * S5_SCAN_COLS, ns + (cc + 1) * S5_SCAN_COLS)
        ar, ai = a_ref[:, re], a_ref[:, im]

        def body(t, carry, re=re, im=im, ar=ar, ai=ai):
            hr, hi = carry
            te = t + d * (tc - 1 - 2 * t)
            rows = pl.ds(pl.multiple_of(te * SUBLANES, SUBLANES), SUBLANES)
            nr = ar * hr - ai * hi + bu_ref[rows, re]
            ni = ar * hi + ai * hr + bu_ref[rows, im]
            bu_ref[rows, re] = nr
            bu_ref[rows, im] = ni
            return nr, ni

        hr, hi = lax.fori_loop(0, tc, body, (h_ref[:, re], h_ref[:, im]), unroll=4)
        h_ref[:, re] = hr
        h_ref[:, im] = hi

    nc = wc_ref.shape[0]
    ow = wc_ref.shape[2]
    per_blk = nc // (width // ow)
    for blk in range(width // ow):
        acc = None
        for kk in range(per_blk):
            k = blk * per_blk + kk
            hre = bu_ref[:, k * S5_CCHUNK:(k + 1) * S5_CCHUNK].astype(BF16)
            him = bu_ref[:, ns + k * S5_CCHUNK:ns + (k + 1) * S5_CCHUNK].astype(BF16)
            t = _dot(hre, wc_ref[k, :S5_CCHUNK, :]) + _dot(him, wc_ref[k, S5_CCHUNK:, :])
            acc = t if acc is None else acc + t
        y_ref[:, :, blk * ow:(blk + 1) * ow] = acc.reshape(tc, SUBLANES, ow)

    @pl.when(c == n_chunks - 1)
    def _():
        hf_ref[...] = h_ref[...]


def _s5_scan(u_t, a, wb, wc, h0, tc=64):
    length, bp, width = u_t.shape
    n_chunks = length // tc
    ns2 = a.shape[2]

    def tmap(d, g, c):
        return c + d * (n_chunks - 1 - 2 * c)

    return pl.pallas_call(
        functools.partial(_s5_kernel, tc=tc, n_chunks=n_chunks),
        grid=(2, bp // SUBLANES, n_chunks),
        in_specs=[
            pl.BlockSpec((tc, SUBLANES, width), lambda d, g, c: (tmap(d, g, c), g, 0)),
            pl.BlockSpec((None,) + wb.shape[1:], lambda d, g, c: (d, 0, 0, 0)),
            pl.BlockSpec((None,) + wc.shape[1:], lambda d, g, c: (d, 0, 0, 0)),
            pl.BlockSpec((None, SUBLANES, ns2), lambda d, g, c: (d, 0, 0)),
            pl.BlockSpec((None, SUBLANES, ns2), lambda d, g, c: (d, g, 0)),
        ],
        out_specs=(
            pl.BlockSpec((None, tc, SUBLANES, width), lambda d, g, c: (d, tmap(d, g, c), g, 0)),
            pl.BlockSpec((None, SUBLANES, ns2), lambda d, g, c: (d, g, 0)),
        ),
        out_shape=(jax.ShapeDtypeStruct((2, length, bp, width), F32), jax.ShapeDtypeStruct((2, bp, ns2), F32)),
        scratch_shapes=[pltpu.VMEM((tc * SUBLANES, ns2), F32), pltpu.VMEM((SUBLANES, ns2), F32)],
        compiler_params=_cparams(3),
        name="s5_scan",
    )(u_t, wb, wc, a, h0)


def _s5_glu_kernel(yf_ref, yb_ref, u_ref, d_ref, w_ref, b_ref, o_ref):
    y = u_ref[...] * d_ref[...] + yf_ref[...] + yb_ref[...]
    gy = 0.5 * y * (1.0 + jnp.tanh(math.sqrt(2.0 / math.pi) * (y + 0.044715 * (y * y * y))))
    r = _dot(gy.astype(BF16), w_ref[...]) + b_ref[...]
    half = r.shape[1] // 2
    o_ref[...] = (r[:, :half] * _sigmoid(r[:, half:])).astype(BF16)


def _s5_glu(y2, u, d, glu_w, glu_b, tm=512):
    rows, width = u.shape
    row_spec = pl.BlockSpec((tm, width), lambda i: (i, 0))
    return pl.pallas_call(
        _s5_glu_kernel,
        grid=(rows // tm,),
        in_specs=[
            pl.BlockSpec((None, tm, width), lambda i: (0, i, 0)),
            pl.BlockSpec((None, tm, width), lambda i: (1, i, 0)),
            row_spec,
            pl.BlockSpec((1, width), lambda i: (0, 0)),
            pl.BlockSpec((width, 2 * width), lambda i: (0, 0)),
            pl.BlockSpec((1, 2 * width), lambda i: (0, 0)),
        ],
        out_specs=row_spec,
        out_shape=jax.ShapeDtypeStruct((rows, width), BF16),
        compiler_params=_cparams(1),
        name="s5_glu",
    )(y2, y2, u, d.reshape(1, width), glu_w, glu_b.reshape(1, 2 * width))


def _mixer_c(su, batch, length, s5w, d, glu_w, glu_b, h0):
    a, wb, wc = s5w
    width = su.shape[1]
    bp = -(-batch // SUBLANES) * SUBLANES
    u_t = jnp.transpose(su.reshape(batch, length, width), (1, 0, 2))
    if bp != batch:
        u_t = jnp.pad(u_t, ((0, 0), (0, bp - batch), (0, 0)))
        h0 = jnp.pad(h0, ((0, 0), (0, bp - batch), (0, 0)))
    y2, hf = _s5_scan(u_t, a, wb, wc, h0)
    oc_t = _s5_glu(y2.reshape(2, length * bp, width), u_t.reshape(length * bp, width), d, glu_w, glu_b)
    oc = jnp.transpose(oc_t.reshape(length, bp, width)[:, :batch], (1, 0, 2)).reshape(batch * length, width)
    return oc, hf[:, :batch]


def kernel(x_prompt, x_sample, c, cache_a_k, cache_a_v, cache_d_k, cache_d_v, state_ssm, c_ctx, ada_w, ada_b, norm_g, ffn_w_gu, ffn_w_d, w_in, a_q_norm, a_k_norm, a_sink, hy_conv_w, hy_conv_b, hy_w_in, hy_b_in, hy_w_hid, hy_b_hid, hy_w_out, hy_freq, hy_bias, s5_lam_re, s5_lam_im, s5_log_step, s5_b_re, s5_b_im, s5_c_re, s5_c_im, s5_d, s5_glu_w, s5_glu_b, d_q_norm, d_k_norm, d_lambda, d_subln, w_branch, w_out):
    bc, lc, dm = x_prompt.shape
    bl, ll, _ = x_sample.shape
    depth = ada_w.shape[0]
    mix = dm // 4
    n_ctx, n_lat = bc * lc, bl * ll
    m = n_ctx + n_lat
    assert n_ctx % ll == 0 and bl + 1 <= N_COND_ROWS
    tok = _Tokens(n_ctx, ll, m)
    past = cache_a_k.shape[2]
    a_kv = cache_a_k.shape[3]
    groups, n_state = s5_lam_re.shape[2], s5_lam_re.shape[3]
    ns = groups * n_state
    d_width = cache_d_k.shape[4]

    splits = (mix, a_kv * A_HEAD_DIM, a_kv * A_HEAD_DIM, 3 * mix, mix, mix, mix, mix, N_BRANCH * dm)
    offs = [0]
    for s in splits:
        offs.append(offs[-1] + s)
    c_aq, c_ak, c_av, c_hy, c_su, c_dq, c_dk, c_dv, c_gt = offs[:9]

    tm = 512 if (n_ctx % 1024 or n_lat % 1024) else 1024
    tm_in = 2048 if m % 2048 == 0 else tm

    x = (x_prompt.reshape(n_ctx, dm), x_sample.reshape(n_lat, dm))
    cond = jnp.zeros((N_COND_ROWS, dm), F32).at[0].set(c_ctx).at[1:1 + bl].set(c)
    mod = _ada(cond, ada_w, ada_b)

    w_br_b = w_branch.astype(BF16)
    w_out_b = w_out.astype(BF16)
    glu_w_b = s5_glu_w.astype(BF16)

    tabs_c, tabs_cb = _dft_tables(lc)
    tabs_l, tabs_lb = _dft_tables(ll)
    ctx_rb = 0
    lat_rb = n_ctx // ll

    new_ak, new_av, new_dk, new_dv, new_ss = [], [], [], [], []
    for l in range(depth):
        modl = mod[l].reshape(N_COND_ROWS, N_MOD, 1, dm)

        def ffn(x, j, k0, row_ranges=(None,)):
            h = _rmsmod(x, norm_g[l, 2 * j], modl, k0, k0 + 1, tok)
            act, w_d_b = _gu(h, ffn_w_gu, ffn_w_d, (l, j), tm, 256)
            outs = [_mm_resid(act, w_d_b, (), x, modl, k0 + 2, 0.5, tok, 512, 512, "ffn_down", rows)
                    for rows in row_ranges]
            return outs[0] if len(outs) == 1 else outs

        x = ffn(x, 0, 0)

        h = _rmsmod(x, norm_g[l, 1], modl, 3, 4, tok)
        z = _mm_ws(h, w_in, (l,), (0, c_gt), F32, tm_in, 512, "in_proj")
        zg = _mm_ws(h, w_in, (l,), (c_gt, N_BRANCH * dm), BF16, tm_in, 512, "in_proj_gate")

        oa_c, ak_n, av_n = _attn_a(z, ctx_rb, bc, lc, c_aq, c_ak, c_av, a_q_norm[l], a_k_norm[l], a_sink[l])
        ctx_a = (cache_a_k[:, l].reshape(bl, past, a_kv * A_HEAD_DIM), cache_a_v[:, l].reshape(bl, past, a_kv * A_HEAD_DIM))
        oa_l = _attn_a(z, lat_rb, bl, ll, c_aq, c_ak, c_av, a_q_norm[l], a_k_norm[l], a_sink[l], ctx_a)
        new_ak.append(ak_n.reshape(bc, lc, a_kv, A_HEAD_DIM))
        new_av.append(av_n.reshape(bc, lc, a_kv, A_HEAD_DIM))

        hy_args = (hy_w_in[l], hy_b_in[l], hy_w_hid[l], hy_b_hid[l], hy_w_out[l], hy_freq[l])
        filt_c = _hy_filter(lc, *hy_args, tabs_c)
        filt_l = _hy_filter(ll, *hy_args, tabs_l)
        ob_c = _hyena(z, ctx_rb, bc, lc, c_hy, mix, hy_conv_w[l], hy_conv_b[l], filt_c, hy_bias[l], tabs_cb)
        ob_l = _hyena(z, lat_rb, bl, ll, c_hy, mix, hy_conv_w[l], hy_conv_b[l], filt_l, hy_bias[l], tabs_lb)

        s5w = _s5_weights(s5_lam_re[l], s5_lam_im[l], s5_log_step[l], s5_b_re[l], s5_b_im[l], s5_c_re[l], s5_c_im[l])
        su = z[:, c_su:c_su + mix]
        h0_c = jnp.zeros((2, bc, 2 * ns), F32)
        st = state_ssm[:, l]
        h0_l = jnp.transpose(st, (1, 0, 4, 2, 3)).reshape(2, bl, 2 * ns)
        oc_c, hf = _mixer_c(su[:n_ctx], bc, lc, s5w, s5_d[l], glu_w_b[l], s5_glu_b[l], h0_c)
        oc_l, _ = _mixer_c(su[n_ctx:], bl, ll, s5w, s5_d[l], glu_w_b[l], s5_glu_b[l], h0_l)
        new_ss.append(jnp.transpose(hf.reshape(2, bc, 2, groups, n_state), (1, 0, 3, 4, 2)))

        lam_init = 0.8 - 0.6 * math.exp(-0.3 * l)
        d_args = (d_q_norm[l], d_k_norm[l], d_lambda[l], d_subln[l], lam_init)
        od_c, dk_n, dv_n = _attn_d(z, ctx_rb, bc, lc, c_dq, c_dk, c_dv, *d_args)
        ctx_d = (cache_d_k[:, l].reshape(bl, past, D_HEADS * d_width), cache_d_v[:, l].reshape(bl, past, D_HEADS * d_width))
        od_l = _attn_d(z, lat_rb, bl, ll, c_dq, c_dk, c_dv, *d_args, ctx_d)
        new_dk.append(dk_n.reshape(bc, lc, D_HEADS, d_width))
        new_dv.append(dv_n.reshape(bc, lc, D_HEADS, d_width))

        s = _branch((oa_c, ob_c, oc_c, od_c), (oa_l, ob_l, oc_l, od_l), w_br_b, l, zg, tok, tm, 512)
        x = _mm_resid(s, w_out_b, (l,), x, modl, 5, 1.0, tok, tm, 512, "out_proj")

        if l + 1 < depth:
            x = ffn(x, 1, 6)
        else:
            y_prompt, y_sample = ffn(x, 1, 6, ((0, n_ctx), (n_ctx, n_lat)))

    y_prompt = y_prompt.reshape(bc, lc, dm)
    y_sample = y_sample.reshape(bl, ll, dm)
    return (y_prompt, y_sample, jnp.stack(new_ak, axis=1), jnp.stack(new_av, axis=1), jnp.stack(new_dk, axis=1),
            jnp.stack(new_dv, axis=1), jnp.stack(new_ss, axis=1))
```

```python
import functools
import math

import numpy as np
import jax
import jax.numpy as jnp
from jax import lax
from jax.experimental import pallas as pl
from jax.experimental.pallas import tpu as pltpu

F32 = jnp.float32
BF16 = jnp.bfloat16
HIGHEST = lax.Precision.HIGHEST

GRID_W = 64
WINDOW = 128
ROPE_BASE = 10000.0
EPS = 1e-6
NEG_INF = -1e30
N_BRANCH = 4
A_HEAD_DIM = 128
A_GROUPS = 4
HY_EMB = 33
HY_TARGET = 1e-2
HY_FAST = 0.3
HY_SLOW = 1.5
S5_CH = 16
S5_STATE = 64
D_HEADS = 8
N_MOD = 9
N_COND_ROWS = 8

V7X_VMEM_BYTES = 64 * 1024 * 1024
VMEM_LIMIT = V7X_VMEM_BYTES - 8 * 1024 * 1024
LANES = 128
SUBLANES = 8


def _cparams(n_axes):
    return pltpu.CompilerParams(dimension_semantics=("arbitrary",) * n_axes, vmem_limit_bytes=VMEM_LIMIT)


def _sigmoid(x):
    return 1.0 / (1.0 + jnp.exp(-x))


def _dot(a, b):
    return jnp.dot(a, b, preferred_element_type=F32)


def _dot_nt(a, b):
    return lax.dot_general(a, b, (((1,), (1,)), ((), ())), preferred_element_type=F32)


def _dot_hi(a, b):
    return jnp.dot(a, b, preferred_element_type=F32, precision=HIGHEST)


def _ada_kernel(c_ref, w_ref, b_ref, o_ref):
    c = c_ref[...]
    s = (c * _sigmoid(c)).astype(BF16)
    o_ref[...] = _dot(s, w_ref[...].astype(BF16)) + b_ref[...]


def _ada(cond, ada_w, ada_b, tn=512):
    depth, d, n = ada_w.shape
    return pl.pallas_call(
        _ada_kernel,
        grid=(depth, n // tn),
        in_specs=[
            pl.BlockSpec((N_COND_ROWS, d), lambda l, j: (0, 0)),
            pl.BlockSpec((None, d, tn), lambda l, j: (l, 0, j)),
            pl.BlockSpec((None, 1, tn), lambda l, j: (l, 0, j)),
        ],
        out_specs=pl.BlockSpec((None, N_COND_ROWS, tn), lambda l, j: (l, 0, j)),
        out_shape=jax.ShapeDtypeStruct((depth, N_COND_ROWS, n), F32),
        compiler_params=_cparams(2),
        name="ada_mod",
    )(cond, ada_w, ada_b.reshape(depth, 1, n))


class _Tokens:
    def __init__(self, n_ctx, l_lat, m):
        self.n_ctx, self.l_lat, self.m = n_ctx, l_lat, m

    def cond(self, i, tm):
        row = i * tm
        return jnp.where(row >= self.n_ctx, (row - self.n_ctx) // self.l_lat + 1, 0)

    def split_specs(self, tm, width, col=None):
        nct = self.n_ctx // tm
        nlt = (self.m - self.n_ctx) // tm

        def cspec(*ids):
            return (jnp.minimum(ids[0], nct - 1), col(*ids) if col else 0)

        def lspec(*ids):
            return (jnp.clip(ids[0] - nct, 0, nlt - 1), col(*ids) if col else 0)

        return pl.BlockSpec((tm, width), cspec), pl.BlockSpec((tm, width), lspec), nct


def _rmsmod_body(x_ref, g_ref, sc_ref, sh_ref, o_ref):
    x = x_ref[...]
    ms = jnp.mean(x * x, axis=-1, keepdims=True)
    y = x * lax.rsqrt(ms + EPS) * g_ref[...]
    o_ref[...] = (y * (1.0 + sc_ref[...]) + sh_ref[...]).astype(BF16)


def _by_group(n_ctx_tiles, body, src_c, src_l, *rest):
    is_ctx = pl.program_id(0) < n_ctx_tiles

    @pl.when(is_ctx)
    def _():
        body(src_c, *rest)

    @pl.when(jnp.logical_not(is_ctx))
    def _():
        body(src_l, *rest)


def _rmsmod(x, gain, modl, k_sh, k_sc, tok, tm=512):
    d = gain.shape[0]
    if isinstance(x, tuple):
        xc_spec, xl_spec, nct = tok.split_specs(tm, d)
        x_specs, x_args = [xc_spec, xl_spec], list(x)
        body = functools.partial(_by_group, nct, _rmsmod_body)
    else:
        x_specs, x_args, body = [pl.BlockSpec((tm, d), lambda i: (i, 0))], [x], _rmsmod_body
    return pl.pallas_call(
        body,
        grid=(tok.m // tm,),
        in_specs=x_specs + [
            pl.BlockSpec((1, d), lambda i: (0, 0)),
            pl.BlockSpec((None, None, 1, d), lambda i: (tok.cond(i, tm), k_sc, 0, 0)),
            pl.BlockSpec((None, None, 1, d), lambda i: (tok.cond(i, tm), k_sh, 0, 0)),
        ],
        out_specs=pl.BlockSpec((tm, d), lambda i: (i, 0)),
        out_shape=jax.ShapeDtypeStruct((tok.m, d), BF16),
        compiler_params=_cparams(1),
        name="rmsmod",
    )(*x_args, gain.reshape(1, d), modl, modl)


def _stream_weight_tiles(copies, cast):
    j = pl.program_id(0)
    nj = pl.num_programs(0)

    @pl.when(pl.program_id(1) == 0)
    def _():
        @pl.when(j == 0)
        def _():
            for cp in copies(0):
                cp.start()

        for cp in copies(j):
            cp.wait()
        cast()

        @pl.when(j + 1 < nj)
        def _():
            for cp in copies(j + 1):
                cp.start()


def _mm_ws_kernel(a_ref, w_hbm, o_ref, wf_ref, wb_ref, sem, *, lead, cb, tn):
    def copies(jj):
        cols = pl.ds(pl.multiple_of((cb + jj) * tn, tn), tn)
        return [pltpu.make_async_copy(w_hbm.at[lead + (slice(None), cols)], wf_ref, sem.at[0])]

    def cast():
        wb_ref[...] = wf_ref[...].astype(BF16)

    _stream_weight_tiles(copies, cast)
    o_ref[...] = _dot(a_ref[...], wb_ref[...]).astype(o_ref.dtype)


def _mm_ws(a, w, lead, cols, out_dtype, tm, tn, name):
    m, k = a.shape
    col0, n = cols
    assert col0 % tn == 0 and n % tn == 0 and m % tm == 0
    return pl.pallas_call(
        functools.partial(_mm_ws_kernel, lead=lead, cb=col0 // tn, tn=tn),
        grid=(n // tn, m // tm),
        in_specs=[pl.BlockSpec((tm, k), lambda j, i: (i, 0)), pl.BlockSpec(memory_space=pl.ANY)],
        out_specs=pl.BlockSpec((tm, tn), lambda j, i: (i, j)),
        out_shape=jax.ShapeDtypeStruct((m, n), out_dtype),
        scratch_shapes=[pltpu.VMEM((k, tn), F32), pltpu.VMEM((k, tn), BF16), pltpu.SemaphoreType.DMA((1,))],
        compiler_params=_cparams(2),
        name=name,
    )(a, w)


def _gu_kernel(h_ref, wgu_hbm, wd_hbm, o_ref, wdb_ref, wgf_ref, wuf_ref, wdf_ref, wgb_ref, wub_ref, sem, *,
               lead, nf, tf):
    def copies(jj):
        gcols = pl.ds(pl.multiple_of(jj * tf, tf), tf)
        ucols = pl.ds(pl.multiple_of((nf + jj) * tf, tf), tf)
        return [
            pltpu.make_async_copy(wgu_hbm.at[lead + (slice(None), gcols)], wgf_ref, sem.at[0]),
            pltpu.make_async_copy(wgu_hbm.at[lead + (slice(None), ucols)], wuf_ref, sem.at[1]),
            pltpu.make_async_copy(wd_hbm.at[lead + (gcols, slice(None))], wdf_ref, sem.at[2]),
        ]

    def cast():
        wgb_ref[...] = wgf_ref[...].astype(BF16)
        wub_ref[...] = wuf_ref[...].astype(BF16)
        wdb_ref[...] = wdf_ref[...].astype(BF16)

    _stream_weight_tiles(copies, cast)
    h = h_ref[...]
    g = _dot(h, wgb_ref[...])
    u = _dot(h, wub_ref[...])
    o_ref[...] = (g * _sigmoid(g) * u).astype(BF16)


def _gu(h, w_gu, w_d, lead, tm, tf):
    m, d = h.shape
    ff = w_gu.shape[-1] // 2
    nf = ff // tf
    assert m % tm == 0
    return pl.pallas_call(
        functools.partial(_gu_kernel, lead=lead, nf=nf, tf=tf),
        grid=(nf, m // tm),
        in_specs=[pl.BlockSpec((tm, d), lambda j, i: (i, 0)), pl.BlockSpec(memory_space=pl.ANY),
                  pl.BlockSpec(memory_space=pl.ANY)],
        out_specs=(pl.BlockSpec((tm, tf), lambda j, i: (i, j)), pl.BlockSpec((tf, d), lambda j, i: (j, 0))),
        out_shape=(jax.ShapeDtypeStruct((m, ff), BF16), jax.ShapeDtypeStruct((ff, d), BF16)),
        scratch_shapes=[pltpu.VMEM((d, tf), F32), pltpu.VMEM((d, tf), F32), pltpu.VMEM((tf, d), F32),
                        pltpu.VMEM((d, tf), BF16), pltpu.VMEM((d, tf), BF16), pltpu.SemaphoreType.DMA((3,))],
        compiler_params=_cparams(2),
        name="ffn_gate_up",
    )(h, w_gu, w_d)


def _mm_resid_body(x_ref, a_ref, w_ref, gate_ref, o_ref, *, coef):
    y = _dot(a_ref[...], w_ref[...])
    o_ref[...] = x_ref[...] + (coef * gate_ref[...]) * y


def _mm_resid(a, w, lead, x, modl, k_gate, coef, tok, tm, tn, name, rows=None):
    m, k = a.shape
    n = w.shape[-1]
    nlead = len(lead)
    row0, m_out = (0, m) if rows is None else rows
    rb = row0 // tm
    body = functools.partial(_mm_resid_body, coef=coef)
    if isinstance(x, tuple):
        assert rows is None
        xc_spec, xl_spec, nct = tok.split_specs(tm, tn, col=lambda i, j: j)
        x_specs, x_args = [xc_spec, xl_spec], list(x)
        body = functools.partial(_by_group, nct, body)
    else:
        x_specs, x_args = [pl.BlockSpec((tm, tn), lambda i, j: (rb + i, j))], [x]
    return pl.pallas_call(
        body,
        grid=(m_out // tm, n // tn),
        in_specs=x_specs + [
            pl.BlockSpec((tm, k), lambda i, j: (rb + i, 0)),
            pl.BlockSpec((None,) * nlead + (k, tn), lambda i, j: lead + (0, j)),
            pl.BlockSpec((None, None, 1, tn), lambda i, j: (tok.cond(rb + i, tm), k_gate, 0, j)),
        ],
        out_specs=pl.BlockSpec((tm, tn), lambda i, j: (i, j)),
        out_shape=jax.ShapeDtypeStruct((m_out, n), F32),
        compiler_params=_cparams(2),
        name=name,
    )(*x_args, a, w, modl)


def _branch_kernel(*refs, n_ctx_tiles):
    o_ctx, o_lat = refs[:N_BRANCH], refs[N_BRANCH:2 * N_BRANCH]
    w_ref = refs[2 * N_BRANCH]
    gates = refs[2 * N_BRANCH + 1:3 * N_BRANCH + 1]
    o_ref = refs[3 * N_BRANCH + 1]

    def body(branches):
        acc = None
        for b in range(N_BRANCH):
            t = _sigmoid(gates[b][...].astype(F32)) * _dot(branches[b][...], w_ref[b])
            acc = t if acc is None else acc + t
        o_ref[...] = acc.astype(BF16)

    is_ctx = pl.program_id(0) < n_ctx_tiles

    @pl.when(is_ctx)
    def _():
        body(o_ctx)

    @pl.when(jnp.logical_not(is_ctx))
    def _():
        body(o_lat)


def _branch(o_ctx, o_lat, w_branch, layer, zg, tok, tm, tn):
    mix = o_ctx[0].shape[1]
    d = w_branch.shape[-1]
    oc_spec, ol_spec, nct = tok.split_specs(tm, mix)

    def gate_spec(b):
        return pl.BlockSpec((tm, tn), lambda i, j: (i, (b * d) // tn + j))

    return pl.pallas_call(
        functools.partial(_branch_kernel, n_ctx_tiles=nct),
        grid=(tok.m // tm, d // tn),
        in_specs=[oc_spec] * N_BRANCH + [ol_spec] * N_BRANCH
        + [pl.BlockSpec((None, N_BRANCH, mix, tn), lambda i, j: (layer, 0, 0, j))]
        + [gate_spec(b) for b in range(N_BRANCH)],
        out_specs=pl.BlockSpec((tm, tn), lambda i, j: (i, j)),
        out_shape=jax.ShapeDtypeStruct((tok.m, d), BF16),
        compiler_params=_cparams(2),
        name="branch_gate",
    )(*o_ctx, *o_lat, w_branch, zg, zg, zg, zg)


def _rope_tables(length, n, reps):
    quarter = n // 4
    n_rows = length // GRID_W
    rows = jnp.repeat(jnp.arange(n_rows), GRID_W).astype(F32)
    cols = jnp.tile(jnp.arange(GRID_W), n_rows).astype(F32)
    freqs = ROPE_BASE ** (-jnp.arange(quarter, dtype=F32) / quarter)
    ang_r = rows[:, None] * freqs[None, :]
    ang_c = cols[:, None] * freqs[None, :]
    cos = jnp.concatenate([jnp.cos(ang_r), jnp.cos(ang_r), jnp.cos(ang_c), jnp.cos(ang_c)], axis=-1)
    sin = jnp.concatenate([-jnp.sin(ang_r), jnp.sin(ang_r), -jnp.sin(ang_c), jnp.sin(ang_c)], axis=-1)
    return jnp.tile(cos, (1, reps)), jnp.tile(sin, (1, reps))


def _rope(x, cos, sin, quarter):
    lanes = x.shape[-1]
    lane = lax.broadcasted_iota(jnp.int32, (1, lanes), 1)
    first = (lane % (2 * quarter)) < quarter
    partner = jnp.where(first, pltpu.roll(x, lanes - quarter, 1), pltpu.roll(x, quarter, 1))
    return x * cos + partner * sin


def _rms_rows(x, gain):
    ms = jnp.mean(x * x, axis=-1, keepdims=True)
    return x * lax.rsqrt(ms + EPS) * gain


A_QBLOCK = 256


def _attn_a_kernel(*refs, latent, scale):
    if latent:
        q_ref, k_ref, v_ref, qn_ref, kn_ref, sink_ref, cos_ref, sin_ref, kc_ref, vc_ref, o_ref = refs
    else:
        q_ref, k_ref, v_ref, qn_ref, kn_ref, sink_ref, o_ref, ko_ref, vo_ref = refs
    length = k_ref.shape[0]
    k = _rms_rows(k_ref[...], kn_ref[...])
    v = v_ref[...]
    if latent:
        cos, sin = cos_ref[...], sin_ref[...]
        k = _rope(k, cos, sin, A_HEAD_DIM // 4)
        kc = kc_ref[...].astype(BF16)
        vc = vc_ref[...].astype(BF16)
        qblock = A_QBLOCK
    else:
        ko_ref[...] = k
        vo_ref[...] = v
        qblock = length
    kb = k.astype(BF16)
    vb = v.astype(BF16)
    for g in range(A_GROUPS):
        sl = slice(g * A_HEAD_DIM, (g + 1) * A_HEAD_DIM)
        q = _rms_rows(q_ref[:, sl], qn_ref[...])
        if latent:
            q = _rope(q, cos, sin, A_HEAD_DIM // 4)
        qb_all = (q * scale).astype(BF16)
        sink = sink_ref[:, g * A_HEAD_DIM:g * A_HEAD_DIM + 1]
        for q0 in range(0, length, qblock):
            qb = qb_all[q0:q0 + qblock]
            if latent:
                k0, k1 = max(0, q0 - WINDOW), min(length, q0 + qblock + WINDOW)
                s = _dot_nt(qb, kb[k0:k1])
                qi = q0 + lax.broadcasted_iota(jnp.int32, (qblock, k1 - k0), 0)
                ki = k0 + lax.broadcasted_iota(jnp.int32, (qblock, k1 - k0), 1)
                s = jnp.where(jnp.abs(qi - ki) <= WINDOW, s, NEG_INF)
                s2 = _dot_nt(qb, kc)
                mx = jnp.maximum(jnp.maximum(jnp.max(s, axis=-1, keepdims=True),
                                             jnp.max(s2, axis=-1, keepdims=True)), sink)
                p2 = jnp.exp(s2 - mx)
                vals = vb[k0:k1]
            else:
                s = _dot_nt(qb, kb)
                mx = jnp.maximum(jnp.max(s, axis=-1, keepdims=True), sink)
                vals = vb
            p = jnp.exp(s - mx)
            den = jnp.sum(p, axis=-1, keepdims=True) + jnp.exp(sink - mx)
            o = _dot(p.astype(BF16), vals)
            if latent:
                den = den + jnp.sum(p2, axis=-1, keepdims=True)
                o = o + _dot(p2.astype(BF16), vc)
            o_ref[q0:q0 + qblock, sl] = (o / den).astype(BF16)


def _attn_a(z, row_blk0, batch, length, q_col, k_col, v_col, qn, kn, sink, ctx=None):
    kv_heads = (v_col - k_col) // A_HEAD_DIM
    qw = A_GROUPS * A_HEAD_DIM
    latent = ctx is not None
    sink_b = jnp.repeat(sink.reshape(kv_heads, A_GROUPS), A_HEAD_DIM, axis=1).reshape(kv_heads, 1, qw)
    in_specs = [
        pl.BlockSpec((length, qw), lambda b, h: (row_blk0 + b, q_col // qw + h)),
        pl.BlockSpec((length, A_HEAD_DIM), lambda b, h: (row_blk0 + b, k_col // A_HEAD_DIM + h)),
        pl.BlockSpec((length, A_HEAD_DIM), lambda b, h: (row_blk0 + b, v_col // A_HEAD_DIM + h)),
        pl.BlockSpec((1, A_HEAD_DIM), lambda b, h: (0, 0)),
        pl.BlockSpec((1, A_HEAD_DIM), lambda b, h: (0, 0)),
        pl.BlockSpec((None, 1, qw), lambda b, h: (h, 0, 0)),
    ]
    args = [z, z, z, qn.reshape(1, A_HEAD_DIM), kn.reshape(1, A_HEAD_DIM), sink_b]
    o_spec = pl.BlockSpec((length, qw), lambda b, h: (b, h))
    o_shape = jax.ShapeDtypeStruct((batch * length, kv_heads * qw), BF16)
    if latent:
        k_ctx, v_ctx = ctx
        past = k_ctx.shape[1]
        cos, sin = _rope_tables(length, A_HEAD_DIM, 1)
        tab = pl.BlockSpec((length, A_HEAD_DIM), lambda b, h: (0, 0))
        cache = pl.BlockSpec((None, past, A_HEAD_DIM), lambda b, h: (b, 0, h))
        in_specs += [tab, tab, cache, cache]
        args += [cos, sin, k_ctx, v_ctx]
        out_specs, out_shape = o_spec, o_shape
    else:
        kv_spec = pl.BlockSpec((length, A_HEAD_DIM), lambda b, h: (b, h))
        kv_shape = jax.ShapeDtypeStruct((batch * length, kv_heads * A_HEAD_DIM), F32)
        out_specs, out_shape = (o_spec, kv_spec, kv_spec), (o_shape, kv_shape, kv_shape)
    return pl.pallas_call(
        functools.partial(_attn_a_kernel, latent=latent, scale=A_HEAD_DIM ** -0.5),
        grid=(batch, kv_heads),
        in_specs=in_specs,
        out_specs=out_specs,
        out_shape=out_shape,
        compiler_params=_cparams(2),
        name="attn_a_lat" if latent else "attn_a_ctx",
    )(*args)


def _rms_halves(x, gain, lo):
    half = x.shape[-1] // 2
    x2 = x * x
    s_lo = jnp.sum(jnp.where(lo, x2, 0.0), axis=-1, keepdims=True)
    s_hi = jnp.sum(jnp.where(lo, 0.0, x2), axis=-1, keepdims=True)
    ms = jnp.where(lo, s_lo, s_hi) * (1.0 / half)
    return x * lax.rsqrt(ms + EPS) * gain


D_QBLOCK = 256


def _attn_d_kernel(*refs, latent, heads, scale, lam_init):
    if latent:
        q_ref, k_ref, v_ref, qn_ref, kn_ref, lam_ref, sub_ref, cos_ref, sin_ref, kc_ref, vc_ref, o_ref = refs
    else:
        q_ref, k_ref, v_ref, qn_ref, kn_ref, lam_ref, sub_ref, o_ref, ko_ref, vo_ref = refs
    width = qn_ref.shape[1]
    dh = width // 2
    lane = lax.broadcasted_iota(jnp.int32, (1, width), 1)
    lo = lane < dh
    lv = lam_ref[...]
    lam = (jnp.exp(jnp.sum(lv[0:1] * lv[1:2], axis=-1, keepdims=True))
           - jnp.exp(jnp.sum(lv[2:3] * lv[3:4], axis=-1, keepdims=True)) + lam_init)
    for hh in range(heads):
        sl = slice(hh * width, (hh + 1) * width)
        q = _rms_halves(q_ref[:, sl], qn_ref[...], lo)
        k = _rms_halves(k_ref[:, sl], kn_ref[...], lo)
        v = v_ref[:, sl]
        if latent:
            cos, sin = cos_ref[...], sin_ref[...]
            q = _rope(q, cos, sin, dh // 4)
            k = _rope(k, cos, sin, dh // 4)
            key_parts = [k.astype(BF16), kc_ref[:, sl].astype(BF16)]
            val_parts = [v.astype(BF16), vc_ref[:, sl].astype(BF16)]
        else:
            ko_ref[:, sl] = k
            vo_ref[:, sl] = v
            key_parts = [k.astype(BF16)]
            val_parts = [v.astype(BF16)]
        q = q * scale
        q01 = (jnp.where(lo, q, 0.0).astype(BF16), jnp.where(lo, 0.0, q).astype(BF16))
        length = q.shape[0]
        qblock = min(length, D_QBLOCK)
        for q0 in range(0, length, qblock):
            o = None
            for comp in range(2):
                qc = q01[comp][q0:q0 + qblock]
                scores = [_dot_nt(qc, kp) for kp in key_parts]
                mx = None
                for s in scores:
                    m = jnp.max(s, axis=-1, keepdims=True)
                    mx = m if mx is None else jnp.maximum(mx, m)
                acc = den = None
                for s, vp in zip(scores, val_parts):
                    p = jnp.exp(s - mx)
                    d = jnp.sum(p, axis=-1, keepdims=True)
                    t = _dot(p.astype(BF16), vp)
                    den = d if den is None else den + d
                    acc = t if acc is None else acc + t
                oc = acc * (1.0 / den)
                o = oc if comp == 0 else o - lam * oc
            o = _rms_rows(o, sub_ref[...]) * (1.0 - lam_init)
            o_ref[q0:q0 + qblock, sl] = o.astype(BF16)


def _attn_d(z, row_blk0, batch, length, q_col, k_col, v_col, qn, kn, lam_vecs, subln, lam_init, ctx=None):
    width = (k_col - q_col) // D_HEADS
    dh = width // 2
    latent = ctx is not None
    heads = 1 if latent else 4
    bw = heads * width
    steps = D_HEADS // heads
    assert q_col % bw == 0 and k_col % bw == 0 and v_col % bw == 0

    def zspec(col):
        return pl.BlockSpec((length, bw), lambda b, h: (row_blk0 + b, col // bw + h))

    row = pl.BlockSpec((1, width), lambda b, h: (0, 0))
    in_specs = [zspec(q_col), zspec(k_col), zspec(v_col), row, row,
                pl.BlockSpec((4, dh), lambda b, h: (0, 0)), row]
    args = [z, z, z, jnp.tile(qn, 2).reshape(1, width), jnp.tile(kn, 2).reshape(1, width), lam_vecs,
            subln.reshape(1, width)]
    o_spec = pl.BlockSpec((length, bw), lambda b, h: (b, h))
    o_shape = jax.ShapeDtypeStruct((batch * length, D_HEADS * width), BF16)
    if latent:
        k_ctx, v_ctx = ctx
        past = k_ctx.shape[1]
        cos, sin = _rope_tables(length, dh, 2)
        tab = pl.BlockSpec((length, width), lambda b, h: (0, 0))
        cache = pl.BlockSpec((None, past, bw), lambda b, h: (b, 0, h))
        in_specs += [tab, tab, cache, cache]
        args += [cos, sin, k_ctx, v_ctx]
        out_specs, out_shape = o_spec, o_shape
    else:
        kv_shape = jax.ShapeDtypeStruct((batch * length, D_HEADS * width), F32)
        out_specs, out_shape = (o_spec, o_spec, o_spec), (o_shape, kv_shape, kv_shape)
    return pl.pallas_call(
        functools.partial(_attn_d_kernel, latent=latent, heads=heads, scale=dh ** -0.5, lam_init=lam_init),
        grid=(batch, steps),
        in_specs=in_specs,
        out_specs=out_specs,
        out_shape=out_shape,
        compiler_params=_cparams(2),
        name="attn_d_lat" if latent else "attn_d_ctx",
    )(*args)


def _dft_tables(length):
    m2 = 2 * length
    k = np.arange(length)
    ang = 2.0 * np.pi * ((k[:, None] * k[None, :]) % m2) / m2
    alt = np.where(k % 2 == 0, 1.0, -1.0)
    fr = np.cos(ang)
    fi = -np.sin(ang)
    fi[0, :] = alt
    ir = (2.0 / m2) * np.cos(ang)
    ir[:, 0] = 1.0 / m2
    ii = -(2.0 / m2) * np.sin(ang)
    ii[:, 0] = alt / m2
    exact = tuple(jnp.asarray(t, dtype=F32) for t in (fr, fi))
    rounded = tuple(jnp.asarray(t, dtype=F32).astype(BF16) for t in (fr, fi, ir, ii))
    return exact, rounded


def _hy_filter_kernel(z_ref, wi_ref, bi_ref, wh_ref, bh_ref, wf_ref, wb_ref, fr_ref, dl_ref, fre_ref, fim_ref,
                      kr_ref, ki_ref, kr2_ref):
    length = z_ref.shape[0]
    fr = fr_ref[...]
    h = jnp.sin(fr * (_dot_hi(z_ref[...], wi_ref[...]) + bi_ref[...]))
    for i in range(wh_ref.shape[0]):
        h = jnp.sin(fr * (_dot_hi(h, wh_ref[i]) + bh_ref[i]))
    row = lax.broadcasted_iota(jnp.int32, (length, 1), 0)
    t = row.astype(F32) * (1.0 / (length - 1))
    decay = jnp.exp(-t * dl_ref[...])
    hf = _dot_hi(h, wf_ref[...]) * decay
    hb = jnp.where(row == 0, 0.0, _dot_hi(h, wb_ref[...]) * decay)
    hs = hf + hb
    kr = _dot_hi(fre_ref[...], hs)
    ki = _dot_hi(fim_ref[...], hf - hb)
    alt = 1.0 - 2.0 * (row % 2).astype(F32)
    nyq = jnp.sum(alt * hs, axis=0, keepdims=True)
    kr_ref[...] = kr
    ki_ref[...] = jnp.where(row == 0, 0.0, ki)
    kr2_ref[...] = jnp.where(row == 0, nyq, kr)


def _hy_filter(length, w_in, b_in, w_hid, b_hid, w_out, freq, tables, ct=256):
    hid = w_in.shape[1]
    hw = w_out.shape[1] // 2
    n_inner = w_hid.shape[0]
    t = jnp.linspace(0.0, 1.0, length, dtype=F32)[:, None]
    bands = (HY_EMB - 1) // 2
    w = 2.0 * math.pi * jnp.arange(length, dtype=F32)[:, None] / length
    f = jnp.linspace(1e-4, bands - 1, bands, dtype=F32)[None, :]
    feat = jnp.concatenate([t, jnp.cos(f * w), -jnp.sin(f * w)], axis=-1)
    feat = jnp.pad(feat, ((0, 0), (0, LANES - HY_EMB)))
    w_in = jnp.pad(w_in, ((0, LANES - HY_EMB), (0, 0)))
    deltas = jnp.abs(jnp.linspace(math.log(HY_TARGET) / HY_FAST, math.log(HY_TARGET) / HY_SLOW, hw, dtype=F32))
    fre, fim = tables[0], tables[1]
    nj = hw // ct
    full = lambda shape: pl.BlockSpec(shape, lambda j: (0,) * len(shape))
    out_spec = pl.BlockSpec((length, ct), lambda j: (0, j))
    out_shape = jax.ShapeDtypeStruct((length, hw), F32)
    return pl.pallas_call(
        _hy_filter_kernel,
        grid=(nj,),
        in_specs=[
            full((length, LANES)), full((LANES, hid)), full((1, hid)), full((n_inner, hid, hid)),
            full((n_inner, 1, hid)),
            pl.BlockSpec((hid, ct), lambda j: (0, j)), pl.BlockSpec((hid, ct), lambda j: (0, nj + j)),
            full((1, hid)), pl.BlockSpec((1, ct), lambda j: (0, j)), full((length, length)), full((length, length)),
        ],
        out_specs=(out_spec, out_spec, out_spec),
        out_shape=(out_shape, out_shape, out_shape),
        compiler_params=_cparams(1),
        name="hyena_filter",
    )(feat, w_in, b_in.reshape(1, hid), w_hid, b_hid.reshape(n_inner, 1, hid), w_out, w_out,
      freq.reshape(1, hid), deltas.reshape(1, hw), fre, fim)


def _conv3(x, w, b, row, length):
    prev = jnp.where(row == 0, 0.0, pltpu.roll(x, 1, 0))
    nxt = jnp.where(row == length - 1, 0.0, pltpu.roll(x, length - 1, 0))
    return prev * w[0:1] + x * w[1:2] + nxt * w[2:3] + b


def _hyena_kernel(x0_ref, x1_ref, v_ref, w0_ref, w1_ref, wv_ref, b0_ref, b1_ref, bv_ref, kr_ref, ki_ref, kr2_ref,
                  bias_ref, fre_ref, fim_ref, ire_ref, iim_ref, o_ref):
    length = x0_ref.shape[0]
    row = lax.broadcasted_iota(jnp.int32, (length, 1), 0)
    x0 = _conv3(x0_ref[...], w0_ref[...], b0_ref[...], row, length)
    x1 = _conv3(x1_ref[...], w1_ref[...], b1_ref[...], row, length)
    v = _conv3(v_ref[...], wv_ref[...], bv_ref[...], row, length)
    g = v * x1
    gb = g.astype(BF16)
    gr = _dot(fre_ref[...], gb)
    gi = _dot(fim_ref[...], gb)
    pr = gr * kr_ref[...] - gi * ki_ref[...]
    pi = gr * ki_ref[...] + gi * kr2_ref[...]
    y = _dot(ire_ref[...], pr.astype(BF16)) + _dot(iim_ref[...], pi.astype(BF16)) + g * bias_ref[...]
    o_ref[...] = (y * x0).astype(BF16)


def _hyena(z, row_blk0, batch, length, col0, hw, conv_w, conv_b, filt, bias, tables, ct=256):
    nj = hw // ct

    def zspec(part):
        return pl.BlockSpec((length, ct), lambda j, b: (row_blk0 + b, (col0 + part * hw) // ct + j))

    def wspec(part, rows):
        return pl.BlockSpec((rows, ct), lambda j, b: (0, part * nj + j))

    kspec = pl.BlockSpec((length, ct), lambda j, b: (0, j))
    tab = pl.BlockSpec((length, length), lambda j, b: (0, 0))
    conv_b2 = conv_b.reshape(1, 3 * hw)
    return pl.pallas_call(
        _hyena_kernel,
        grid=(nj, batch),
        in_specs=[zspec(0), zspec(1), zspec(2), wspec(0, 3), wspec(1, 3), wspec(2, 3), wspec(0, 1), wspec(1, 1),
                  wspec(2, 1), kspec, kspec, kspec, pl.BlockSpec((1, ct), lambda j, b: (0, j)), tab, tab, tab, tab],
        out_specs=pl.BlockSpec((length, ct), lambda j, b: (b, j)),
        out_shape=jax.ShapeDtypeStruct((batch * length, hw), BF16),
        compiler_params=_cparams(2),
        name="hyena_conv",
    )(z, z, z, conv_w, conv_w, conv_w, conv_b2, conv_b2, conv_b2, filt[0], filt[1], filt[2], bias.reshape(1, hw),
      *tables)


S5_BCHUNK = 256
S5_CCHUNK = 256
S5_SCAN_COLS = 1024


def _s5_discretize(lam_re, lam_im, log_step, b_re, b_im):
    dt = jnp.exp(log_step.astype(F32))[..., None]
    lr, li = lam_re.astype(F32), lam_im.astype(F32)
    mag = jnp.exp(lr * dt)
    a_re, a_im = mag * jnp.cos(li * dt), mag * jnp.sin(li * dt)
    den = lr * lr + li * li
    q_re = ((a_re - 1.0) * lr + a_im * li) / den
    q_im = (a_im * lr - (a_re - 1.0) * li) / den
    qr, qi = q_re[..., None], q_im[..., None]
    br, bi = b_re.astype(F32), b_im.astype(F32)
    return a_re, a_im, qr * br - qi * bi, qr * bi + qi * br


def _s5_weights(lam_re, lam_im, log_step, b_re, b_im, c_re, c_im):
    a_re, a_im, bb_re, bb_im = _s5_discretize(lam_re, lam_im, log_step, b_re, b_im)
    n_dir, groups, n_state, ch = bb_re.shape
    ns = groups * n_state
    a = jnp.concatenate([a_re.reshape(n_dir, 1, ns), a_im.reshape(n_dir, 1, ns)], axis=-1)
    a = jnp.broadcast_to(a, (n_dir, SUBLANES, 2 * ns))
    gb = S5_BCHUNK // ch
    nb = groups // gb
    eye_b = jnp.eye(gb, dtype=F32)

    def wb_of(bb):
        t = bb.reshape(n_dir, nb, gb, n_state, ch)
        t = jnp.einsum('dkgnc,gh->dkgchn', t, eye_b)
        return t.reshape(n_dir, nb, gb * ch, gb * n_state)

    wb = jnp.concatenate([wb_of(bb_re), wb_of(bb_im)], axis=-1).astype(BF16)
    gc = S5_CCHUNK // n_state
    nc = groups // gc
    per_blk = S5_CCHUNK // (gc * ch)
    eye_c = jnp.eye(gc, dtype=F32)
    place = jax.nn.one_hot(jnp.arange(nc) % per_blk, per_blk, dtype=F32)

    def wc_of(cc):
        t = cc.astype(F32).reshape(n_dir, nc, gc, ch, n_state)
        t = jnp.einsum('dkgcn,gh->dkgnhc', t, eye_c).reshape(n_dir, nc, gc * n_state, gc * ch)
        t = jnp.einsum('dkrc,kj->dkrjc', t, place)
        return t.reshape(n_dir, nc, gc * n_state, per_blk * gc * ch)

    wc = jnp.concatenate([wc_of(c_re), -wc_of(c_im)], axis=2).astype(BF16)
    return a, wb, wc


S5_PARTS = 4


def _s5_kernel(u_ref, wb_ref, wc_ref, a_ref, h0_ref, y_ref, hf_ref, h_ref, *bu_refs, tc, n_chunks):
    d = pl.program_id(0)
    c = pl.program_id(2)
    ns = h_ref.shape[1] // 2
    width = u_ref.shape[2]
    pr = tc // S5_PARTS
    prow = pr * SUBLANES
    nb = wb_ref.shape[0]
    bcols = wb_ref.shape[2] // 2
    nc = wc_ref.shape[0]
    ow = wc_ref.shape[2]
    per_blk = nc // (width // ow)

    @pl.when(c == 0)
    def _():
        h_ref[...] = h0_ref[...]

    def part_index(p):
        return p + d * (S5_PARTS - 1 - 2 * p)

    def b_proj(p):
        bu_ref = bu_refs[p]
        u = u_ref[pl.ds(part_index(p) * pr, pr)].reshape(prow, width).astype(BF16)
        for kc in range(nb):
            r = _dot(u[:, kc * S5_BCHUNK:(kc + 1) * S5_BCHUNK], wb_ref[kc])
            bu_ref[:, kc * bcols:(kc + 1) * bcols] = r[:, :bcols]
            bu_ref[:, ns + kc * bcols:ns + (kc + 1) * bcols] = r[:, bcols:]

    def recur(p):
        bu_ref = bu_refs[p]
        for cc in range(ns // S5_SCAN_COLS):
            re = slice(cc * S5_SCAN_COLS, (cc + 1) * S5_SCAN_COLS)
            im = slice(ns + cc * S5_SCAN_COLS, ns + (cc + 1) * S5_SCAN_COLS)
            ar, ai = a_ref[:, re], a_ref[:, im]
            hr, hi = h_ref[:, re], h_ref[:, im]
            for t in range(pr):
                te = t + d * (pr - 1 - 2 * t)
                rows = pl.ds(pl.multiple_of(te * SUBLANES, SUBLANES), SUBLANES)
                hr, hi = ar * hr - ai * hi + bu_ref[rows, re], ar * hi + ai * hr + bu_ref[rows, im]
                bu_ref[rows, re] = hr
                bu_ref[rows, im] = hi
            h_ref[:, re] = hr
            h_ref[:, im] = hi

    def c_proj(p):
        bu_ref = bu_refs[p]
        pe = part_index(p)
        for blk in range(width // ow):
            acc = None
            for kk in range(per_blk):
                k = blk * per_blk + kk
                hre = bu_ref[:, k * S5_CCHUNK:(k + 1) * S5_CCHUNK].astype(BF16)
                him = bu_ref[:, ns + k * S5_CCHUNK:ns + (k + 1) * S5_CCHUNK].astype(BF16)
                t = _dot(hre, wc_ref[k, :S5_CCHUNK, :]) + _dot(him, wc_ref[k, S5_CCHUNK:, :])
                acc = t if acc is None else acc + t
            y_ref[pl.ds(pe * pr, pr), :, blk * ow:(blk + 1) * ow] = acc.reshape(pr, SUBLANES, ow)

    b_proj(0)
    for p in range(S5_PARTS):
        if p + 1 < S5_PARTS:
            b_proj(p + 1)
        recur(p)
        c_proj(p)

    @pl.when(c == n_chunks - 1)
    def _():
        hf_ref[...] = h_ref[...]


def _s5_scan(u_t, a, wb, wc, h0, tc=64):
    length, bp, width = u_t.shape
    n_chunks = length // tc
    ns2 = a.shape[2]

    def tmap(d, g, c):
        return c + d * (n_chunks - 1 - 2 * c)

    return pl.pallas_call(
        functools.partial(_s5_kernel, tc=tc, n_chunks=n_chunks),
        grid=(2, bp // SUBLANES, n_chunks),
        in_specs=[
            pl.BlockSpec((tc, SUBLANES, width), lambda d, g, c: (tmap(d, g, c), g, 0)),
            pl.BlockSpec((None,) + wb.shape[1:], lambda d, g, c: (d, 0, 0, 0)),
            pl.BlockSpec((None,) + wc.shape[1:], lambda d, g, c: (d, 0, 0, 0)),
            pl.BlockSpec((None, SUBLANES, ns2), lambda d, g, c: (d, 0, 0)),
            pl.BlockSpec((None, SUBLANES, ns2), lambda d, g, c: (d, g, 0)),
        ],
        out_specs=(
            pl.BlockSpec((None, tc, SUBLANES, width), lambda d, g, c: (d, tmap(d, g, c), g, 0)),
            pl.BlockSpec((None, SUBLANES, ns2), lambda d, g, c: (d, g, 0)),
        ),
        out_shape=(jax.ShapeDtypeStruct((2, length, bp, width), F32), jax.ShapeDtypeStruct((2, bp, ns2), F32)),
        scratch_shapes=[pltpu.VMEM((SUBLANES, ns2), F32)]
        + [pltpu.VMEM((tc // S5_PARTS * SUBLANES, ns2), F32) for _ in range(S5_PARTS)],
        compiler_params=_cparams(3),
        name="s5_scan",
    )(u_t, wb, wc, a, h0)


def _s5_glu_kernel(yf_ref, yb_ref, u_ref, d_ref, w_ref, b_ref, o_ref):
    y = u_ref[...] * d_ref[...] + yf_ref[...] + yb_ref[...]
    gy = 0.5 * y * (1.0 + jnp.tanh(math.sqrt(2.0 / math.pi) * (y + 0.044715 * (y * y * y))))
    r = _dot(gy.astype(BF16), w_ref[...]) + b_ref[...]
    half = r.shape[1] // 2
    o_ref[...] = (r[:, :half] * _sigmoid(r[:, half:])).astype(BF16)


def _s5_glu(y2, u, d, glu_w, glu_b, tm=512):
    rows, width = u.shape
    row_spec = pl.BlockSpec((tm, width), lambda i: (i, 0))
    return pl.pallas_call(
        _s5_glu_kernel,
        grid=(rows // tm,),
        in_specs=[
            pl.BlockSpec((None, tm, width), lambda i: (0, i, 0)),
            pl.BlockSpec((None, tm, width), lambda i: (1, i, 0)),
            row_spec,
            pl.BlockSpec((1, width), lambda i: (0, 0)),
            pl.BlockSpec((width, 2 * width), lambda i: (0, 0)),
            pl.BlockSpec((1, 2 * width), lambda i: (0, 0)),
        ],
        out_specs=row_spec,
        out_shape=jax.ShapeDtypeStruct((rows, width), BF16),
        compiler_params=_cparams(1),
        name="s5_glu",
    )(y2, y2, u, d.reshape(1, width), glu_w, glu_b.reshape(1, 2 * width))


def _mixer_c(su, batch, length, s5w, d, glu_w, glu_b, h0):
    a, wb, wc = s5w
    width = su.shape[1]
    bp = -(-batch // SUBLANES) * SUBLANES
    u_t = jnp.transpose(su.reshape(batch, length, width), (1, 0, 2))
    if bp != batch:
        u_t = jnp.pad(u_t, ((0, 0), (0, bp - batch), (0, 0)))
        h0 = jnp.pad(h0, ((0, 0), (0, bp - batch), (0, 0)))
    y2, hf = _s5_scan(u_t, a, wb, wc, h0)
    oc_t = _s5_glu(y2.reshape(2, length * bp, width), u_t.reshape(length * bp, width), d, glu_w, glu_b)
    oc = jnp.transpose(oc_t.reshape(length, bp, width)[:, :batch], (1, 0, 2)).reshape(batch * length, width)
    return oc, hf[:, :batch]


def kernel(x_prompt, x_sample, c, cache_a_k, cache_a_v, cache_d_k, cache_d_v, state_ssm, c_ctx, ada_w, ada_b, norm_g, ffn_w_gu, ffn_w_d, w_in, a_q_norm, a_k_norm, a_sink, hy_conv_w, hy_conv_b, hy_w_in, hy_b_in, hy_w_hid, hy_b_hid, hy_w_out, hy_freq, hy_bias, s5_lam_re, s5_lam_im, s5_log_step, s5_b_re, s5_b_im, s5_c_re, s5_c_im, s5_d, s5_glu_w, s5_glu_b, d_q_norm, d_k_norm, d_lambda, d_subln, w_branch, w_out):
    bc, lc, dm = x_prompt.shape
    bl, ll, _ = x_sample.shape
    depth = ada_w.shape[0]
    mix = dm // 4
    n_ctx, n_lat = bc * lc, bl * ll
    m = n_ctx + n_lat
    assert n_ctx % ll == 0 and bl + 1 <= N_COND_ROWS
    tok = _Tokens(n_ctx, ll, m)
    past = cache_a_k.shape[2]
    a_kv = cache_a_k.shape[3]
    groups, n_state = s5_lam_re.shape[2], s5_lam_re.shape[3]
    ns = groups * n_state
    d_width = cache_d_k.shape[4]

    splits = (mix, a_kv * A_HEAD_DIM, a_kv * A_HEAD_DIM, 3 * mix, mix, mix, mix, mix, N_BRANCH * dm)
    offs = [0]
    for s in splits:
        offs.append(offs[-1] + s)
    c_aq, c_ak, c_av, c_hy, c_su, c_dq, c_dk, c_dv, c_gt = offs[:9]

    tm = 512 if (n_ctx % 1024 or n_lat % 1024) else 1024
    tm_in = 2048 if m % 2048 == 0 else tm

    x = (x_prompt.reshape(n_ctx, dm), x_sample.reshape(n_lat, dm))
    cond = jnp.zeros((N_COND_ROWS, dm), F32).at[0].set(c_ctx).at[1:1 + bl].set(c)
    mod = _ada(cond, ada_w, ada_b)

    w_br_b = w_branch.astype(BF16)
    w_out_b = w_out.astype(BF16)
    glu_w_b = s5_glu_w.astype(BF16)

    tabs_c, tabs_cb = _dft_tables(lc)
    tabs_l, tabs_lb = _dft_tables(ll)
    ctx_rb = 0
    lat_rb = n_ctx // ll

    new_ak, new_av, new_dk, new_dv, new_ss = [], [], [], [], []
    for l in range(depth):
        modl = mod[l].reshape(N_COND_ROWS, N_MOD, 1, dm)

        def ffn(x, j, k0, row_ranges=(None,)):
            h = _rmsmod(x, norm_g[l, 2 * j], modl, k0, k0 + 1, tok)
            act, w_d_b = _gu(h, ffn_w_gu, ffn_w_d, (l, j), tm, 256)
            outs = [_mm_resid(act, w_d_b, (), x, modl, k0 + 2, 0.5, tok, 512, 512, "ffn_down", rows)
                    for rows in row_ranges]
            return outs[0] if len(outs) == 1 else outs

        x = ffn(x, 0, 0)

        h = _rmsmod(x, norm_g[l, 1], modl, 3, 4, tok)
        z = _mm_ws(h, w_in, (l,), (0, c_gt), F32, tm_in, 512, "in_proj")
        zg = _mm_ws(h, w_in, (l,), (c_gt, N_BRANCH * dm), BF16, tm_in, 512, "in_proj_gate")

        oa_c, ak_n, av_n = _attn_a(z, ctx_rb, bc, lc, c_aq, c_ak, c_av, a_q_norm[l], a_k_norm[l], a_sink[l])
        ctx_a = (cache_a_k[:, l].reshape(bl, past, a_kv * A_HEAD_DIM), cache_a_v[:, l].reshape(bl, past, a_kv * A_HEAD_DIM))
        oa_l = _attn_a(z, lat_rb, bl, ll, c_aq, c_ak, c_av, a_q_norm[l], a_k_norm[l], a_sink[l], ctx_a)
        new_ak.append(ak_n.reshape(bc, lc, a_kv, A_HEAD_DIM))
        new_av.append(av_n.reshape(bc, lc, a_kv, A_HEAD_DIM))

        hy_args = (hy_w_in[l], hy_b_in[l], hy_w_hid[l], hy_b_hid[l], hy_w_out[l], hy_freq[l])
        filt_c = _hy_filter(lc, *hy_args, tabs_c)
        filt_l = _hy_filter(ll, *hy_args, tabs_l)
        ob_c = _hyena(z, ctx_rb, bc, lc, c_hy, mix, hy_conv_w[l], hy_conv_b[l], filt_c, hy_bias[l], tabs_cb)
        ob_l = _hyena(z, lat_rb, bl, ll, c_hy, mix, hy_conv_w[l], hy_conv_b[l], filt_l, hy_bias[l], tabs_lb)

        s5w = _s5_weights(s5_lam_re[l], s5_lam_im[l], s5_log_step[l], s5_b_re[l], s5_b_im[l], s5_c_re[l], s5_c_im[l])
        su = z[:, c_su:c_su + mix]
        h0_c = jnp.zeros((2, bc, 2 * ns), F32)
        st = state_ssm[:, l]
        h0_l = jnp.transpose(st, (1, 0, 4, 2, 3)).reshape(2, bl, 2 * ns)
        oc_c, hf = _mixer_c(su[:n_ctx], bc, lc, s5w, s5_d[l], glu_w_b[l], s5_glu_b[l], h0_c)
        oc_l, _ = _mixer_c(su[n_ctx:], bl, ll, s5w, s5_d[l], glu_w_b[l], s5_glu_b[l], h0_l)
        new_ss.append(jnp.transpose(hf.reshape(2, bc, 2, groups, n_state), (1, 0, 3, 4, 2)))

        lam_init = 0.8 - 0.6 * math.exp(-0.3 * l)
        d_args = (d_q_norm[l], d_k_norm[l], d_lambda[l], d_subln[l], lam_init)
        od_c, dk_n, dv_n = _attn_d(z, ctx_rb, bc, lc, c_dq, c_dk, c_dv, *d_args)
        ctx_d = (cache_d_k[:, l].reshape(bl, past, D_HEADS * d_width), cache_d_v[:, l].reshape(bl, past, D_HEADS * d_width))
        od_l = _attn_d(z, lat_rb, bl, ll, c_dq, c_dk, c_dv, *d_args, ctx_d)
        new_dk.append(dk_n.reshape(bc, lc, D_HEADS, d_width))
        new_dv.append(dv_n.reshape(bc, lc, D_HEADS, d_width))

        s = _branch((oa_c, ob_c, oc_c, od_c), (oa_l, ob_l, oc_l, od_l), w_br_b, l, zg, tok, tm, 512)
        x = _mm_resid(s, w_out_b, (l,), x, modl, 5, 1.0, tok, tm, 512, "out_proj")

        if l + 1 < depth:
            x = ffn(x, 1, 6)
        else:
            y_prompt, y_sample = ffn(x, 1, 6, ((0, n_ctx), (n_ctx, n_lat)))

    y_prompt = y_prompt.reshape(bc, lc, dm)
    y_sample = y_sample.reshape(bl, ll, dm)
    return (y_prompt, y_sample, jnp.stack(new_ak, axis=1), jnp.stack(new_av, axis=1), jnp.stack(new_dk, axis=1),
            jnp.stack(new_dv, axis=1), jnp.stack(new_ss, axis=1))
```

```python
import functools
import math

import numpy as np
import jax
import jax.numpy as jnp
from jax import lax
from jax.experimental import pallas as pl
from jax.experimental.pallas import tpu as pltpu

F32 = jnp.float32
BF16 = jnp.bfloat16
HIGHEST = lax.Precision.HIGHEST

GRID_W = 64
WINDOW = 128
ROPE_BASE = 10000.0
EPS = 1e-6
NEG_INF = -1e30
N_BRANCH = 4
A_HEAD_DIM = 128
A_GROUPS = 4
HY_EMB = 33
HY_TARGET = 1e-2
HY_FAST = 0.3
HY_SLOW = 1.5
S5_CH = 16
S5_STATE = 64
D_HEADS = 8
N_MOD = 9
N_COND_ROWS = 8

V7X_VMEM_BYTES = 64 * 1024 * 1024
VMEM_LIMIT = V7X_VMEM_BYTES - 8 * 1024 * 1024
LANES = 128
SUBLANES = 8


def _cparams(n_axes):
    return pltpu.CompilerParams(dimension_semantics=("arbitrary",) * n_axes, vmem_limit_bytes=VMEM_LIMIT)


def _sigmoid(x):
    return 1.0 / (1.0 + jnp.exp(-x))


def _dot(a, b):
    return jnp.dot(a, b, preferred_element_type=F32)


def _dot_nt(a, b):
    return lax.dot_general(a, b, (((1,), (1,)), ((), ())), preferred_element_type=F32)


def _dot_hi(a, b):
    return jnp.dot(a, b, preferred_element_type=F32, precision=HIGHEST)


def _ada_kernel(c_ref, w_ref, b_ref, o_ref):
    c = c_ref[...]
    s = (c * _sigmoid(c)).astype(BF16)
    o_ref[...] = _dot(s, w_ref[...].astype(BF16)) + b_ref[...]


def _ada(cond, ada_w, ada_b, tn=512):
    depth, d, n = ada_w.shape
    return pl.pallas_call(
        _ada_kernel,
        grid=(depth, n // tn),
        in_specs=[
            pl.BlockSpec((N_COND_ROWS, d), lambda l, j: (0, 0)),
            pl.BlockSpec((None, d, tn), lambda l, j: (l, 0, j)),
            pl.BlockSpec((None, 1, tn), lambda l, j: (l, 0, j)),
        ],
        out_specs=pl.BlockSpec((None, N_COND_ROWS, tn), lambda l, j: (l, 0, j)),
        out_shape=jax.ShapeDtypeStruct((depth, N_COND_ROWS, n), F32),
        compiler_params=_cparams(2),
        name="ada_mod",
    )(cond, ada_w, ada_b.reshape(depth, 1, n))


class _Tokens:
    def __init__(self, n_ctx, l_lat, m):
        self.n_ctx, self.l_lat, self.m = n_ctx, l_lat, m

    def cond(self, i, tm):
        row = i * tm
        return jnp.where(row >= self.n_ctx, (row - self.n_ctx) // self.l_lat + 1, 0)

    def split_specs(self, tm, width, col=None):
        nct = self.n_ctx // tm
        nlt = (self.m - self.n_ctx) // tm

        def cspec(*ids):
            return (jnp.minimum(ids[0], nct - 1), col(*ids) if col else 0)

        def lspec(*ids):
            return (jnp.clip(ids[0] - nct, 0, nlt - 1), col(*ids) if col else 0)

        return pl.BlockSpec((tm, width), cspec), pl.BlockSpec((tm, width), lspec), nct


def _rmsmod_body(x_ref, g_ref, sc_ref, sh_ref, o_ref):
    x = x_ref[...]
    ms = jnp.mean(x * x, axis=-1, keepdims=True)
    y = x * lax.rsqrt(ms + EPS) * g_ref[...]
    o_ref[...] = (y * (1.0 + sc_ref[...]) + sh_ref[...]).astype(BF16)


def _by_group(n_ctx_tiles, body, src_c, src_l, *rest):
    is_ctx = pl.program_id(0) < n_ctx_tiles

    @pl.when(is_ctx)
    def _():
        body(src_c, *rest)

    @pl.when(jnp.logical_not(is_ctx))
    def _():
        body(src_l, *rest)


def _rmsmod(x, gain, modl, k_sh, k_sc, tok, tm=512):
    d = gain.shape[0]
    if isinstance(x, tuple):
        xc_spec, xl_spec, nct = tok.split_specs(tm, d)
        x_specs, x_args = [xc_spec, xl_spec], list(x)
        body = functools.partial(_by_group, nct, _rmsmod_body)
    else:
        x_specs, x_args, body = [pl.BlockSpec((tm, d), lambda i: (i, 0))], [x], _rmsmod_body
    return pl.pallas_call(
        body,
        grid=(tok.m // tm,),
        in_specs=x_specs + [
            pl.BlockSpec((1, d), lambda i: (0, 0)),
            pl.BlockSpec((None, None, 1, d), lambda i: (tok.cond(i, tm), k_sc, 0, 0)),
            pl.BlockSpec((None, None, 1, d), lambda i: (tok.cond(i, tm), k_sh, 0, 0)),
        ],
        out_specs=pl.BlockSpec((tm, d), lambda i: (i, 0)),
        out_shape=jax.ShapeDtypeStruct((tok.m, d), BF16),
        compiler_params=_cparams(1),
        name="rmsmod",
    )(*x_args, gain.reshape(1, d), modl, modl)


def _stream_weight_tiles(copies, cast):
    j = pl.program_id(0)
    nj = pl.num_programs(0)

    @pl.when(pl.program_id(1) == 0)
    def _():
        @pl.when(j == 0)
        def _():
            for cp in copies(0):
                cp.start()

        for cp in copies(j):
            cp.wait()
        cast()

        @pl.when(j + 1 < nj)
        def _():
            for cp in copies(j + 1):
                cp.start()


def _mm_ws_kernel(a_ref, w_hbm, o_ref, wf_ref, wb_ref, sem, *, lead, cb, tn):
    def copies(jj):
        cols = pl.ds(pl.multiple_of((cb + jj) * tn, tn), tn)
        return [pltpu.make_async_copy(w_hbm.at[lead + (slice(None), cols)], wf_ref, sem.at[0])]

    def cast():
        wb_ref[...] = wf_ref[...].astype(BF16)

    _stream_weight_tiles(copies, cast)
    o_ref[...] = _dot(a_ref[...], wb_ref[...]).astype(o_ref.dtype)


def _mm_ws(a, w, lead, cols, out_dtype, tm, tn, name):
    m, k = a.shape
    col0, n = cols
    assert col0 % tn == 0 and n % tn == 0 and m % tm == 0
    return pl.pallas_call(
        functools.partial(_mm_ws_kernel, lead=lead, cb=col0 // tn, tn=tn),
        grid=(n // tn, m // tm),
        in_specs=[pl.BlockSpec((tm, k), lambda j, i: (i, 0)), pl.BlockSpec(memory_space=pl.ANY)],
        out_specs=pl.BlockSpec((tm, tn), lambda j, i: (i, j)),
        out_shape=jax.ShapeDtypeStruct((m, n), out_dtype),
        scratch_shapes=[pltpu.VMEM((k, tn), F32), pltpu.VMEM((k, tn), BF16), pltpu.SemaphoreType.DMA((1,))],
        compiler_params=_cparams(2),
        name=name,
    )(a, w)


def _gu_kernel(h_ref, wgu_hbm, wd_hbm, o_ref, wdb_ref, wgf_ref, wuf_ref, wdf_ref, wgb_ref, wub_ref, sem, *,
               lead, nf, tf):
    def copies(jj):
        gcols = pl.ds(pl.multiple_of(jj * tf, tf), tf)
        ucols = pl.ds(pl.multiple_of((nf + jj) * tf, tf), tf)
        return [
            pltpu.make_async_copy(wgu_hbm.at[lead + (slice(None), gcols)], wgf_ref, sem.at[0]),
            pltpu.make_async_copy(wgu_hbm.at[lead + (slice(None), ucols)], wuf_ref, sem.at[1]),
            pltpu.make_async_copy(wd_hbm.at[lead + (gcols, slice(None))], wdf_ref, sem.at[2]),
        ]

    def cast():
        wgb_ref[...] = wgf_ref[...].astype(BF16)
        wub_ref[...] = wuf_ref[...].astype(BF16)
        wdb_ref[...] = wdf_ref[...].astype(BF16)

    _stream_weight_tiles(copies, cast)
    h = h_ref[...]
    g = _dot(h, wgb_ref[...])
    u = _dot(h, wub_ref[...])
    o_ref[...] = (g * _sigmoid(g) * u).astype(BF16)


def _gu(h, w_gu, w_d, lead, tm, tf):
    m, d = h.shape
    ff = w_gu.shape[-1] // 2
    nf = ff // tf
    assert m % tm == 0
    return pl.pallas_call(
        functools.partial(_gu_kernel, lead=lead, nf=nf, tf=tf),
        grid=(nf, m // tm),
        in_specs=[pl.BlockSpec((tm, d), lambda j, i: (i, 0)), pl.BlockSpec(memory_space=pl.ANY),
                  pl.BlockSpec(memory_space=pl.ANY)],
        out_specs=(pl.BlockSpec((tm, tf), lambda j, i: (i, j)), pl.BlockSpec((tf, d), lambda j, i: (j, 0))),
        out_shape=(jax.ShapeDtypeStruct((m, ff), BF16), jax.ShapeDtypeStruct((ff, d), BF16)),
        scratch_shapes=[pltpu.VMEM((d, tf), F32), pltpu.VMEM((d, tf), F32), pltpu.VMEM((tf, d), F32),
                        pltpu.VMEM((d, tf), BF16), pltpu.VMEM((d, tf), BF16), pltpu.SemaphoreType.DMA((3,))],
        compiler_params=_cparams(2),
        name="ffn_gate_up",
    )(h, w_gu, w_d)


def _mm_resid_body(x_ref, a_ref, w_ref, gate_ref, o_ref, *, coef):
    y = _dot(a_ref[...], w_ref[...])
    o_ref[...] = x_ref[...] + (coef * gate_ref[...]) * y


def _mm_resid(a, w, lead, x, modl, k_gate, coef, tok, tm, tn, name, rows=None):
    m, k = a.shape
    n = w.shape[-1]
    nlead = len(lead)
    row0, m_out = (0, m) if rows is None else rows
    rb = row0 // tm
    body = functools.partial(_mm_resid_body, coef=coef)
    if isinstance(x, tuple):
        assert rows is None
        xc_spec, xl_spec, nct = tok.split_specs(tm, tn, col=lambda i, j: j)
        x_specs, x_args = [xc_spec, xl_spec], list(x)
        body = functools.partial(_by_group, nct, body)
    else:
        x_specs, x_args = [pl.BlockSpec((tm, tn), lambda i, j: (rb + i, j))], [x]
    return pl.pallas_call(
        body,
        grid=(m_out // tm, n // tn),
        in_specs=x_specs + [
            pl.BlockSpec((tm, k), lambda i, j: (rb + i, 0)),
            pl.BlockSpec((None,) * nlead + (k, tn), lambda i, j: lead + (0, j)),
            pl.BlockSpec((None, None, 1, tn), lambda i, j: (tok.cond(rb + i, tm), k_gate, 0, j)),
        ],
        out_specs=pl.BlockSpec((tm, tn), lambda i, j: (i, j)),
        out_shape=jax.ShapeDtypeStruct((m_out, n), F32),
        compiler_params=_cparams(2),
        name=name,
    )(*x_args, a, w, modl)


def _branch_kernel(*refs, n_ctx_tiles):
    o_ctx, o_lat = refs[:N_BRANCH], refs[N_BRANCH:2 * N_BRANCH]
    w_ref = refs[2 * N_BRANCH]
    gates = refs[2 * N_BRANCH + 1:3 * N_BRANCH + 1]
    o_ref = refs[3 * N_BRANCH + 1]

    def body(branches):
        acc = None
        for b in range(N_BRANCH):
            t = _sigmoid(gates[b][...].astype(F32)) * _dot(branches[b][...], w_ref[b])
            acc = t if acc is None else acc + t
        o_ref[...] = acc.astype(BF16)

    is_ctx = pl.program_id(0) < n_ctx_tiles

    @pl.when(is_ctx)
    def _():
        body(o_ctx)

    @pl.when(jnp.logical_not(is_ctx))
    def _():
        body(o_lat)


def _branch(o_ctx, o_lat, w_branch, layer, zg, tok, tm, tn):
    mix = o_ctx[0].shape[1]
    d = w_branch.shape[-1]
    oc_spec, ol_spec, nct = tok.split_specs(tm, mix)

    def gate_spec(b):
        return pl.BlockSpec((tm, tn), lambda i, j: (i, (b * d) // tn + j))

    return pl.pallas_call(
        functools.partial(_branch_kernel, n_ctx_tiles=nct),
        grid=(tok.m // tm, d // tn),
        in_specs=[oc_spec] * N_BRANCH + [ol_spec] * N_BRANCH
        + [pl.BlockSpec((None, N_BRANCH, mix, tn), lambda i, j: (layer, 0, 0, j))]
        + [gate_spec(b) for b in range(N_BRANCH)],
        out_specs=pl.BlockSpec((tm, tn), lambda i, j: (i, j)),
        out_shape=jax.ShapeDtypeStruct((tok.m, d), BF16),
        compiler_params=_cparams(2),
        name="branch_gate",
    )(*o_ctx, *o_lat, w_branch, zg, zg, zg, zg)


def _rope_tables(length, n, reps):
    quarter = n // 4
    n_rows = length // GRID_W
    rows = jnp.repeat(jnp.arange(n_rows), GRID_W).astype(F32)
    cols = jnp.tile(jnp.arange(GRID_W), n_rows).astype(F32)
    freqs = ROPE_BASE ** (-jnp.arange(quarter, dtype=F32) / quarter)
    ang_r = rows[:, None] * freqs[None, :]
    ang_c = cols[:, None] * freqs[None, :]
    cos = jnp.concatenate([jnp.cos(ang_r), jnp.cos(ang_r), jnp.cos(ang_c), jnp.cos(ang_c)], axis=-1)
    sin = jnp.concatenate([-jnp.sin(ang_r), jnp.sin(ang_r), -jnp.sin(ang_c), jnp.sin(ang_c)], axis=-1)
    return jnp.tile(cos, (1, reps)), jnp.tile(sin, (1, reps))


def _rope(x, cos, sin, quarter):
    lanes = x.shape[-1]
    lane = lax.broadcasted_iota(jnp.int32, (1, lanes), 1)
    first = (lane % (2 * quarter)) < quarter
    partner = jnp.where(first, pltpu.roll(x, lanes - quarter, 1), pltpu.roll(x, quarter, 1))
    return x * cos + partner * sin


def _rms_rows(x, gain):
    ms = jnp.mean(x * x, axis=-1, keepdims=True)
    return x * lax.rsqrt(ms + EPS) * gain


A_QBLOCK = 256


def _attn_a_kernel(*refs, latent, scale):
    if latent:
        q_ref, k_ref, v_ref, qn_ref, kn_ref, sink_ref, cos_ref, sin_ref, kc_ref, vc_ref, o_ref = refs
    else:
        q_ref, k_ref, v_ref, qn_ref, kn_ref, sink_ref, o_ref, ko_ref, vo_ref = refs
    length = k_ref.shape[0]
    k = _rms_rows(k_ref[...], kn_ref[...])
    v = v_ref[...]
    if latent:
        cos, sin = cos_ref[...], sin_ref[...]
        k = _rope(k, cos, sin, A_HEAD_DIM // 4)
        kc = kc_ref[...].astype(BF16)
        vc = vc_ref[...].astype(BF16)
        qblock = A_QBLOCK
    else:
        ko_ref[...] = k
        vo_ref[...] = v
        qblock = length
    kb = k.astype(BF16)
    vb = v.astype(BF16)
    for g in range(A_GROUPS):
        sl = slice(g * A_HEAD_DIM, (g + 1) * A_HEAD_DIM)
        q = _rms_rows(q_ref[:, sl], qn_ref[...])
        if latent:
            q = _rope(q, cos, sin, A_HEAD_DIM // 4)
        qb_all = (q * scale).astype(BF16)
        sink = sink_ref[:, g * A_HEAD_DIM:g * A_HEAD_DIM + 1]
        for q0 in range(0, length, qblock):
            qb = qb_all[q0:q0 + qblock]
            if latent:
                k0, k1 = max(0, q0 - WINDOW), min(length, q0 + qblock + WINDOW)
                s = _dot_nt(qb, kb[k0:k1])
                qi = q0 + lax.broadcasted_iota(jnp.int32, (qblock, k1 - k0), 0)
                ki = k0 + lax.broadcasted_iota(jnp.int32, (qblock, k1 - k0), 1)
                s = jnp.where(jnp.abs(qi - ki) <= WINDOW, s, NEG_INF)
                s2 = _dot_nt(qb, kc)
                mx = jnp.maximum(jnp.maximum(jnp.max(s, axis=-1, keepdims=True),
                                             jnp.max(s2, axis=-1, keepdims=True)), sink)
                p2 = jnp.exp(s2 - mx)
                vals = vb[k0:k1]
            else:
                s = _dot_nt(qb, kb)
                mx = jnp.maximum(jnp.max(s, axis=-1, keepdims=True), sink)
                vals = vb
            p = jnp.exp(s - mx)
            den = jnp.sum(p, axis=-1, keepdims=True) + jnp.exp(sink - mx)
            o = _dot(p.astype(BF16), vals)
            if latent:
                den = den + jnp.sum(p2, axis=-1, keepdims=True)
                o = o + _dot(p2.astype(BF16), vc)
            o_ref[q0:q0 + qblock, sl] = (o / den).astype(BF16)


def _attn_a(z, row_blk0, batch, length, q_col, k_col, v_col, qn, kn, sink, ctx=None):
    kv_heads = (v_col - k_col) // A_HEAD_DIM
    qw = A_GROUPS * A_HEAD_DIM
    latent = ctx is not None
    sink_b = jnp.repeat(sink.reshape(kv_heads, A_GROUPS), A_HEAD_DIM, axis=1).reshape(kv_heads, 1, qw)
    in_specs = [
        pl.BlockSpec((length, qw), lambda b, h: (row_blk0 + b, q_col // qw + h)),
        pl.BlockSpec((length, A_HEAD_DIM), lambda b, h: (row_blk0 + b, k_col // A_HEAD_DIM + h)),
        pl.BlockSpec((length, A_HEAD_DIM), lambda b, h: (row_blk0 + b, v_col // A_HEAD_DIM + h)),
        pl.BlockSpec((1, A_HEAD_DIM), lambda b, h: (0, 0)),
        pl.BlockSpec((1, A_HEAD_DIM), lambda b, h: (0, 0)),
        pl.BlockSpec((None, 1, qw), lambda b, h: (h, 0, 0)),
    ]
    args = [z, z, z, qn.reshape(1, A_HEAD_DIM), kn.reshape(1, A_HEAD_DIM), sink_b]
    o_spec = pl.BlockSpec((length, qw), lambda b, h: (b, h))
    o_shape = jax.ShapeDtypeStruct((batch * length, kv_heads * qw), BF16)
    if latent:
        k_ctx, v_ctx = ctx
        past = k_ctx.shape[1]
        cos, sin = _rope_tables(length, A_HEAD_DIM, 1)
        tab = pl.BlockSpec((length, A_HEAD_DIM), lambda b, h: (0, 0))
        cache = pl.BlockSpec((None, past, A_HEAD_DIM), lambda b, h: (b, 0, h))
        in_specs += [tab, tab, cache, cache]
        args += [cos, sin, k_ctx, v_ctx]
        out_specs, out_shape = o_spec, o_shape
    else:
        kv_spec = pl.BlockSpec((length, A_HEAD_DIM), lambda b, h: (b, h))
        kv_shape = jax.ShapeDtypeStruct((batch * length, kv_heads * A_HEAD_DIM), F32)
        out_specs, out_shape = (o_spec, kv_spec, kv_spec), (o_shape, kv_shape, kv_shape)
    return pl.pallas_call(
        functools.partial(_attn_a_kernel, latent=latent, scale=A_HEAD_DIM ** -0.5),
        grid=(batch, kv_heads),
        in_specs=in_specs,
        out_specs=out_specs,
        out_shape=out_shape,
        compiler_params=_cparams(2),
        name="attn_a_lat" if latent else "attn_a_ctx",
    )(*args)


def _rms_halves(x, gain, lo):
    half = x.shape[-1] // 2
    x2 = x * x
    s_lo = jnp.sum(jnp.where(lo, x2, 0.0), axis=-1, keepdims=True)
    s_hi = jnp.sum(jnp.where(lo, 0.0, x2), axis=-1, keepdims=True)
    ms = jnp.where(lo, s_lo, s_hi) * (1.0 / half)
    return x * lax.rsqrt(ms + EPS) * gain


D_QBLOCK = 256


def _attn_d_kernel(*refs, latent, heads, scale, lam_init):
    if latent:
        q_ref, k_ref, v_ref, qn_ref, kn_ref, lam_ref, sub_ref, cos_ref, sin_ref, kc_ref, vc_ref, o_ref = refs
    else:
        q_ref, k_ref, v_ref, qn_ref, kn_ref, lam_ref, sub_ref, o_ref, ko_ref, vo_ref = refs
    width = qn_ref.shape[1]
    dh = width // 2
    lane = lax.broadcasted_iota(jnp.int32, (1, width), 1)
    lo = lane < dh
    lv = lam_ref[...]
    lam = (jnp.exp(jnp.sum(lv[0:1] * lv[1:2], axis=-1, keepdims=True))
           - jnp.exp(jnp.sum(lv[2:3] * lv[3:4], axis=-1, keepdims=True)) + lam_init)
    for hh in range(heads):
        sl = slice(hh * width, (hh + 1) * width)
        q = _rms_halves(q_ref[:, sl], qn_ref[...], lo)
        k = _rms_halves(k_ref[:, sl], kn_ref[...], lo)
        v = v_ref[:, sl]
        if latent:
            cos, sin = cos_ref[...], sin_ref[...]
            q = _rope(q, cos, sin, dh // 4)
            k = _rope(k, cos, sin, dh // 4)
            key_parts = [k.astype(BF16), kc_ref[:, sl].astype(BF16)]
            val_parts = [v.astype(BF16), vc_ref[:, sl].astype(BF16)]
        else:
            ko_ref[:, sl] = k
            vo_ref[:, sl] = v
            key_parts = [k.astype(BF16)]
            val_parts = [v.astype(BF16)]
        q = q * scale
        q01 = (jnp.where(lo, q, 0.0).astype(BF16), jnp.where(lo, 0.0, q).astype(BF16))
        length = q.shape[0]
        qblock = min(length, D_QBLOCK)
        for q0 in range(0, length, qblock):
            o = None
            for comp in range(2):
                qc = q01[comp][q0:q0 + qblock]
                scores = [_dot_nt(qc, kp) for kp in key_parts]
                mx = None
                for s in scores:
                    m = jnp.max(s, axis=-1, keepdims=True)
                    mx = m if mx is None else jnp.maximum(mx, m)
                acc = den = None
                for s, vp in zip(scores, val_parts):
                    p = jnp.exp(s - mx)
                    d = jnp.sum(p, axis=-1, keepdims=True)
                    t = _dot(p.astype(BF16), vp)
                    den = d if den is None else den + d
                    acc = t if acc is None else acc + t
                oc = acc * (1.0 / den)
                o = oc if comp == 0 else o - lam * oc
            o = _rms_rows(o, sub_ref[...]) * (1.0 - lam_init)
            o_ref[q0:q0 + qblock, sl] = o.astype(BF16)


def _attn_d(z, row_blk0, batch, length, q_col, k_col, v_col, qn, kn, lam_vecs, subln, lam_init, ctx=None):
    width = (k_col - q_col) // D_HEADS
    dh = width // 2
    latent = ctx is not None
    heads = 1 if latent else 4
    bw = heads * width
    steps = D_HEADS // heads
    assert q_col % bw == 0 and k_col % bw == 0 and v_col % bw == 0

    def zspec(col):
        return pl.BlockSpec((length, bw), lambda b, h: (row_blk0 + b, col // bw + h))

    row = pl.BlockSpec((1, width), lambda b, h: (0, 0))
    in_specs = [zspec(q_col), zspec(k_col), zspec(v_col), row, row,
                pl.BlockSpec((4, dh), lambda b, h: (0, 0)), row]
    args = [z, z, z, jnp.tile(qn, 2).reshape(1, width), jnp.tile(kn, 2).reshape(1, width), lam_vecs,
            subln.reshape(1, width)]
    o_spec = pl.BlockSpec((length, bw), lambda b, h: (b, h))
    o_shape = jax.ShapeDtypeStruct((batch * length, D_HEADS * width), BF16)
    if latent:
        k_ctx, v_ctx = ctx
        past = k_ctx.shape[1]
        cos, sin = _rope_tables(length, dh, 2)
        tab = pl.BlockSpec((length, width), lambda b, h: (0, 0))
        cache = pl.BlockSpec((None, past, bw), lambda b, h: (b, 0, h))
        in_specs += [tab, tab, cache, cache]
        args += [cos, sin, k_ctx, v_ctx]
        out_specs, out_shape = o_spec, o_shape
    else:
        kv_shape = jax.ShapeDtypeStruct((batch * length, D_HEADS * width), F32)
        out_specs, out_shape = (o_spec, o_spec, o_spec), (o_shape, kv_shape, kv_shape)
    return pl.pallas_call(
        functools.partial(_attn_d_kernel, latent=latent, heads=heads, scale=dh ** -0.5, lam_init=lam_init),
        grid=(batch, steps),
        in_specs=in_specs,
        out_specs=out_specs,
        out_shape=out_shape,
        compiler_params=_cparams(2),
        name="attn_d_lat" if latent else "attn_d_ctx",
    )(*args)


def _dft_tables(length):
    m2 = 2 * length
    k = np.arange(length)
    ang = 2.0 * np.pi * ((k[:, None] * k[None, :]) % m2) / m2
    alt = np.where(k % 2 == 0, 1.0, -1.0)
    fr = np.cos(ang)
    fi = -np.sin(ang)
    fi[0, :] = alt
    ir = (2.0 / m2) * np.cos(ang)
    ir[:, 0] = 1.0 / m2
    ii = -(2.0 / m2) * np.sin(ang)
    ii[:, 0] = alt / m2
    exact = tuple(jnp.asarray(t, dtype=F32) for t in (fr, fi))
    rounded = tuple(jnp.asarray(t, dtype=F32).astype(BF16) for t in (fr, fi, ir, ii))
    return exact, rounded


def _hy_filter_kernel(z_ref, wi_ref, bi_ref, wh_ref, bh_ref, wf_ref, wb_ref, fr_ref, dl_ref, fre_ref, fim_ref,
                      kr_ref, ki_ref, kr2_ref):
    length = z_ref.shape[0]
    fr = fr_ref[...]
    h = jnp.sin(fr * (_dot_hi(z_ref[...], wi_ref[...]) + bi_ref[...]))
    for i in range(wh_ref.shape[0]):
        h = jnp.sin(fr * (_dot_hi(h, wh_ref[i]) + bh_ref[i]))
    row = lax.broadcasted_iota(jnp.int32, (length, 1), 0)
    t = row.astype(F32) * (1.0 / (length - 1))
    decay = jnp.exp(-t * dl_ref[...])
    hf = _dot_hi(h, wf_ref[...]) * decay
    hb = jnp.where(row == 0, 0.0, _dot_hi(h, wb_ref[...]) * decay)
    hs = hf + hb
    kr = _dot_hi(fre_ref[...], hs)
    ki = _dot_hi(fim_ref[...], hf - hb)
    alt = 1.0 - 2.0 * (row % 2).astype(F32)
    nyq = jnp.sum(alt * hs, axis=0, keepdims=True)
    kr_ref[...] = kr
    ki_ref[...] = jnp.where(row == 0, 0.0, ki)
    kr2_ref[...] = jnp.where(row == 0, nyq, kr)


def _hy_filter(length, w_in, b_in, w_hid, b_hid, w_out, freq, tables, ct=256):
    hid = w_in.shape[1]
    hw = w_out.shape[1] // 2
    n_inner = w_hid.shape[0]
    t = jnp.linspace(0.0, 1.0, length, dtype=F32)[:, None]
    bands = (HY_EMB - 1) // 2
    w = 2.0 * math.pi * jnp.arange(length, dtype=F32)[:, None] / length
    f = jnp.linspace(1e-4, bands - 1, bands, dtype=F32)[None, :]
    feat = jnp.concatenate([t, jnp.cos(f * w), -jnp.sin(f * w)], axis=-1)
    feat = jnp.pad(feat, ((0, 0), (0, LANES - HY_EMB)))
    w_in = jnp.pad(w_in, ((0, LANES - HY_EMB), (0, 0)))
    deltas = jnp.abs(jnp.linspace(math.log(HY_TARGET) / HY_FAST, math.log(HY_TARGET) / HY_SLOW, hw, dtype=F32))
    fre, fim = tables[0], tables[1]
    nj = hw // ct
    full = lambda shape: pl.BlockSpec(shape, lambda j: (0,) * len(shape))
    out_spec = pl.BlockSpec((length, ct), lambda j: (0, j))
    out_shape = jax.ShapeDtypeStruct((length, hw), F32)
    return pl.pallas_call(
        _hy_filter_kernel,
        grid=(nj,),
        in_specs=[
            full((length, LANES)), full((LANES, hid)), full((1, hid)), full((n_inner, hid, hid)),
            full((n_inner, 1, hid)),
            pl.BlockSpec((hid, ct), lambda j: (0, j)), pl.BlockSpec((hid, ct), lambda j: (0, nj + j)),
            full((1, hid)), pl.BlockSpec((1, ct), lambda j: (0, j)), full((length, length)), full((length, length)),
        ],
        out_specs=(out_spec, out_spec, out_spec),
        out_shape=(out_shape, out_shape, out_shape),
        compiler_params=_cparams(1),
        name="hyena_filter",
    )(feat, w_in, b_in.reshape(1, hid), w_hid, b_hid.reshape(n_inner, 1, hid), w_out, w_out,
      freq.reshape(1, hid), deltas.reshape(1, hw), fre, fim)


def _conv3(x, w, b, row, length):
    prev = jnp.where(row == 0, 0.0, pltpu.roll(x, 1, 0))
    nxt = jnp.where(row == length - 1, 0.0, pltpu.roll(x, length - 1, 0))
    return prev * w[0:1] + x * w[1:2] + nxt * w[2:3] + b


def _hyena_kernel(x0_ref, x1_ref, v_ref, w0_ref, w1_ref, wv_ref, b0_ref, b1_ref, bv_ref, kr_ref, ki_ref, kr2_ref,
                  bias_ref, fre_ref, fim_ref, ire_ref, iim_ref, o_ref):
    length = x0_ref.shape[0]
    row = lax.broadcasted_iota(jnp.int32, (length, 1), 0)
    x0 = _conv3(x0_ref[...], w0_ref[...], b0_ref[...], row, length)
    x1 = _conv3(x1_ref[...], w1_ref[...], b1_ref[...], row, length)
    v = _conv3(v_ref[...], wv_ref[...], bv_ref[...], row, length)
    g = v * x1
    gb = g.astype(BF16)
    gr = _dot(fre_ref[...], gb)
    gi = _dot(fim_ref[...], gb)
    pr = gr * kr_ref[...] - gi * ki_ref[...]
    pi = gr * ki_ref[...] + gi * kr2_ref[...]
    y = _dot(ire_ref[...], pr.astype(BF16)) + _dot(iim_ref[...], pi.astype(BF16)) + g * bias_ref[...]
    o_ref[...] = (y * x0).astype(BF16)


def _hyena(z, row_blk0, batch, length, col0, hw, conv_w, conv_b, filt, bias, tables, ct=512):
    nj = hw // ct

    def zspec(part):
        return pl.BlockSpec((length, ct), lambda j, b: (row_blk0 + b, (col0 + part * hw) // ct + j))

    def wspec(part, rows):
        return pl.BlockSpec((rows, ct), lambda j, b: (0, part * nj + j))

    kspec = pl.BlockSpec((length, ct), lambda j, b: (0, j))
    tab = pl.BlockSpec((length, length), lambda j, b: (0, 0))
    conv_b2 = conv_b.reshape(1, 3 * hw)
    return pl.pallas_call(
        _hyena_kernel,
        grid=(nj, batch),
        in_specs=[zspec(0), zspec(1), zspec(2), wspec(0, 3), wspec(1, 3), wspec(2, 3), wspec(0, 1), wspec(1, 1),
                  wspec(2, 1), kspec, kspec, kspec, pl.BlockSpec((1, ct), lambda j, b: (0, j)), tab, tab, tab, tab],
        out_specs=pl.BlockSpec((length, ct), lambda j, b: (b, j)),
        out_shape=jax.ShapeDtypeStruct((batch * length, hw), BF16),
        compiler_params=_cparams(2),
        name="hyena_conv",
    )(z, z, z, conv_w, conv_w, conv_w, conv_b2, conv_b2, conv_b2, filt[0], filt[1], filt[2], bias.reshape(1, hw),
      *tables)


S5_BCHUNK = 256
S5_CCHUNK = 256
S5_SCAN_COLS = 1024


def _s5_discretize(lam_re, lam_im, log_step, b_re, b_im):
    dt = jnp.exp(log_step.astype(F32))[..., None]
    lr, li = lam_re.astype(F32), lam_im.astype(F32)
    mag = jnp.exp(lr * dt)
    a_re, a_im = mag * jnp.cos(li * dt), mag * jnp.sin(li * dt)
    den = lr * lr + li * li
    q_re = ((a_re - 1.0) * lr + a_im * li) / den
    q_im = (a_im * lr - (a_re - 1.0) * li) / den
    qr, qi = q_re[..., None], q_im[..., None]
    br, bi = b_re.astype(F32), b_im.astype(F32)
    return a_re, a_im, qr * br - qi * bi, qr * bi + qi * br


def _s5_weights(lam_re, lam_im, log_step, b_re, b_im, c_re, c_im):
    a_re, a_im, bb_re, bb_im = _s5_discretize(lam_re, lam_im, log_step, b_re, b_im)
    n_dir, groups, n_state, ch = bb_re.shape
    ns = groups * n_state
    a = jnp.concatenate([a_re.reshape(n_dir, 1, ns), a_im.reshape(n_dir, 1, ns)], axis=-1)
    a = jnp.broadcast_to(a, (n_dir, SUBLANES, 2 * ns))
    gb = S5_BCHUNK // ch
    nb = groups // gb
    eye_b = jnp.eye(gb, dtype=F32)

    def wb_of(bb):
        t = bb.reshape(n_dir, nb, gb, n_state, ch)
        t = jnp.einsum('dkgnc,gh->dkgchn', t, eye_b)
        return t.reshape(n_dir, nb, gb * ch, gb * n_state)

    wb = jnp.concatenate([wb_of(bb_re), wb_of(bb_im)], axis=-1).astype(BF16)
    gc = S5_CCHUNK // n_state
    nc = groups // gc
    per_blk = S5_CCHUNK // (gc * ch)
    eye_c = jnp.eye(gc, dtype=F32)
    place = jax.nn.one_hot(jnp.arange(nc) % per_blk, per_blk, dtype=F32)

    def wc_of(cc):
        t = cc.astype(F32).reshape(n_dir, nc, gc, ch, n_state)
        t = jnp.einsum('dkgcn,gh->dkgnhc', t, eye_c).reshape(n_dir, nc, gc * n_state, gc * ch)
        t = jnp.einsum('dkrc,kj->dkrjc', t, place)
        return t.reshape(n_dir, nc, gc * n_state, per_blk * gc * ch)

    wc = jnp.concatenate([wc_of(c_re), -wc_of(c_im)], axis=2).astype(BF16)
    return a, wb, wc


S5_PARTS = 4


def _s5_kernel(u_ref, wb_ref, wc_ref, a_ref, h0_ref, y_ref, hf_ref, h_ref, *bu_refs, tc, n_chunks):
    d = pl.program_id(0)
    c = pl.program_id(2)
    ns = h_ref.shape[1] // 2
    width = u_ref.shape[2]
    pr = tc // S5_PARTS
    prow = pr * SUBLANES
    nb = wb_ref.shape[0]
    bcols = wb_ref.shape[2] // 2
    nc = wc_ref.shape[0]
    ow = wc_ref.shape[2]
    per_blk = nc // (width // ow)

    @pl.when(c == 0)
    def _():
        h_ref[...] = h0_ref[...]

    def part_index(p):
        return p + d * (S5_PARTS - 1 - 2 * p)

    def b_proj(p):
        bu_ref = bu_refs[p]
        u = u_ref[pl.ds(part_index(p) * pr, pr)].reshape(prow, width).astype(BF16)
        for kc in range(nb):
            r = _dot(u[:, kc * S5_BCHUNK:(kc + 1) * S5_BCHUNK], wb_ref[kc])
            bu_ref[:, kc * bcols:(kc + 1) * bcols] = r[:, :bcols]
            bu_ref[:, ns + kc * bcols:ns + (kc + 1) * bcols] = r[:, bcols:]

    def recur(p):
        bu_ref = bu_refs[p]
        for cc in range(ns // S5_SCAN_COLS):
            re = slice(cc * S5_SCAN_COLS, (cc + 1) * S5_SCAN_COLS)
            im = slice(ns + cc * S5_SCAN_COLS, ns + (cc + 1) * S5_SCAN_COLS)
            ar, ai = a_ref[:, re], a_ref[:, im]
            hr, hi = h_ref[:, re], h_ref[:, im]
            for t in range(pr):
                te = t + d * (pr - 1 - 2 * t)
                rows = pl.ds(pl.multiple_of(te * SUBLANES, SUBLANES), SUBLANES)
                hr, hi = ar * hr - ai * hi + bu_ref[rows, re], ar * hi + ai * hr + bu_ref[rows, im]
                bu_ref[rows, re] = hr
                bu_ref[rows, im] = hi
            h_ref[:, re] = hr
            h_ref[:, im] = hi

    def c_proj(p):
        bu_ref = bu_refs[p]
        pe = part_index(p)
        for blk in range(width // ow):
            acc = None
            for kk in range(per_blk):
                k = blk * per_blk + kk
                hre = bu_ref[:, k * S5_CCHUNK:(k + 1) * S5_CCHUNK].astype(BF16)
                him = bu_ref[:, ns + k * S5_CCHUNK:ns + (k + 1) * S5_CCHUNK].astype(BF16)
                t = _dot(hre, wc_ref[k, :S5_CCHUNK, :]) + _dot(him, wc_ref[k, S5_CCHUNK:, :])
                acc = t if acc is None else acc + t
            y_ref[pl.ds(pe * pr, pr), :, blk * ow:(blk + 1) * ow] = acc.reshape(pr, SUBLANES, ow)

    b_proj(0)
    for p in range(S5_PARTS):
        if p + 1 < S5_PARTS:
            b_proj(p + 1)
        recur(p)
        c_proj(p)

    @pl.when(c == n_chunks - 1)
    def _():
        hf_ref[...] = h_ref[...]


def _s5_scan(u_t, a, wb, wc, h0, tc=64):
    length, bp, width = u_t.shape
    n_chunks = length // tc
    ns2 = a.shape[2]

    def tmap(d, g, c):
        return c + d * (n_chunks - 1 - 2 * c)

    return pl.pallas_call(
        functools.partial(_s5_kernel, tc=tc, n_chunks=n_chunks),
        grid=(2, bp // SUBLANES, n_chunks),
        in_specs=[
            pl.BlockSpec((tc, SUBLANES, width), lambda d, g, c: (tmap(d, g, c), g, 0)),
            pl.BlockSpec((None,) + wb.shape[1:], lambda d, g, c: (d, 0, 0, 0)),
            pl.BlockSpec((None,) + wc.shape[1:], lambda d, g, c: (d, 0, 0, 0)),
            pl.BlockSpec((None, SUBLANES, ns2), lambda d, g, c: (d, 0, 0)),
            pl.BlockSpec((None, SUBLANES, ns2), lambda d, g, c: (d, g, 0)),
        ],
        out_specs=(
            pl.BlockSpec((None, tc, SUBLANES, width), lambda d, g, c: (d, tmap(d, g, c), g, 0)),
            pl.BlockSpec((None, SUBLANES, ns2), lambda d, g, c: (d, g, 0)),
        ),
        out_shape=(jax.ShapeDtypeStruct((2, length, bp, width), F32), jax.ShapeDtypeStruct((2, bp, ns2), F32)),
        scratch_shapes=[pltpu.VMEM((SUBLANES, ns2), F32)]
        + [pltpu.VMEM((tc // S5_PARTS * SUBLANES, ns2), F32) for _ in range(S5_PARTS)],
        compiler_params=_cparams(3),
        name="s5_scan",
    )(u_t, wb, wc, a, h0)


def _s5_glu_kernel(yf_ref, yb_ref, u_ref, d_ref, w_ref, b_ref, o_ref):
    y = u_ref[...] * d_ref[...] + yf_ref[...] + yb_ref[...]
    gy = 0.5 * y * (1.0 + jnp.tanh(math.sqrt(2.0 / math.pi) * (y + 0.044715 * (y * y * y))))
    r = _dot(gy.astype(BF16), w_ref[...]) + b_ref[...]
    half = r.shape[1] // 2
    o_ref[...] = (r[:, :half] * _sigmoid(r[:, half:])).astype(BF16)


def _s5_glu(y2, u, d, glu_w, glu_b, tm=512):
    rows, width = u.shape
    row_spec = pl.BlockSpec((tm, width), lambda i: (i, 0))
    return pl.pallas_call(
        _s5_glu_kernel,
        grid=(rows // tm,),
        in_specs=[
            pl.BlockSpec((None, tm, width), lambda i: (0, i, 0)),
            pl.BlockSpec((None, tm, width), lambda i: (1, i, 0)),
            row_spec,
            pl.BlockSpec((1, width), lambda i: (0, 0)),
            pl.BlockSpec((width, 2 * width), lambda i: (0, 0)),
            pl.BlockSpec((1, 2 * width), lambda i: (0, 0)),
        ],
        out_specs=row_spec,
        out_shape=jax.ShapeDtypeStruct((rows, width), BF16),
        compiler_params=_cparams(1),
        name="s5_glu",
    )(y2, y2, u, d.reshape(1, width), glu_w, glu_b.reshape(1, 2 * width))


def _mixer_c(su, batch, length, s5w, d, glu_w, glu_b, h0):
    a, wb, wc = s5w
    width = su.shape[1]
    bp = -(-batch // SUBLANES) * SUBLANES
    u_t = jnp.transpose(su.reshape(batch, length, width), (1, 0, 2))
    if bp != batch:
        u_t = jnp.pad(u_t, ((0, 0), (0, bp - batch), (0, 0)))
        h0 = jnp.pad(h0, ((0, 0), (0, bp - batch), (0, 0)))
    y2, hf = _s5_scan(u_t, a, wb, wc, h0)
    oc_t = _s5_glu(y2.reshape(2, length * bp, width), u_t.reshape(length * bp, width), d, glu_w, glu_b)
    oc = jnp.transpose(oc_t.reshape(length, bp, width)[:, :batch], (1, 0, 2)).reshape(batch * length, width)
    return oc, hf[:, :batch]


def kernel(x_prompt, x_sample, c, cache_a_k, cache_a_v, cache_d_k, cache_d_v, state_ssm, c_ctx, ada_w, ada_b, norm_g, ffn_w_gu, ffn_w_d, w_in, a_q_norm, a_k_norm, a_sink, hy_conv_w, hy_conv_b, hy_w_in, hy_b_in, hy_w_hid, hy_b_hid, hy_w_out, hy_freq, hy_bias, s5_lam_re, s5_lam_im, s5_log_step, s5_b_re, s5_b_im, s5_c_re, s5_c_im, s5_d, s5_glu_w, s5_glu_b, d_q_norm, d_k_norm, d_lambda, d_subln, w_branch, w_out):
    bc, lc, dm = x_prompt.shape
    bl, ll, _ = x_sample.shape
    depth = ada_w.shape[0]
    mix = dm // 4
    n_ctx, n_lat = bc * lc, bl * ll
    m = n_ctx + n_lat
    assert n_ctx % ll == 0 and bl + 1 <= N_COND_ROWS
    tok = _Tokens(n_ctx, ll, m)
    past = cache_a_k.shape[2]
    a_kv = cache_a_k.shape[3]
    groups, n_state = s5_lam_re.shape[2], s5_lam_re.shape[3]
    ns = groups * n_state
    d_width = cache_d_k.shape[4]

    splits = (mix, a_kv * A_HEAD_DIM, a_kv * A_HEAD_DIM, 3 * mix, mix, mix, mix, mix, N_BRANCH * dm)
    offs = [0]
    for s in splits:
        offs.append(offs[-1] + s)
    c_aq, c_ak, c_av, c_hy, c_su, c_dq, c_dk, c_dv, c_gt = offs[:9]

    tm = 512 if (n_ctx % 1024 or n_lat % 1024) else 1024
    tm_in = 2048 if m % 2048 == 0 else tm

    x = (x_prompt.reshape(n_ctx, dm), x_sample.reshape(n_lat, dm))
    cond = jnp.zeros((N_COND_ROWS, dm), F32).at[0].set(c_ctx).at[1:1 + bl].set(c)
    mod = _ada(cond, ada_w, ada_b)

    w_br_b = w_branch.astype(BF16)
    w_out_b = w_out.astype(BF16)
    glu_w_b = s5_glu_w.astype(BF16)

    tabs_c, tabs_cb = _dft_tables(lc)
    tabs_l, tabs_lb = _dft_tables(ll)
    ctx_rb = 0
    lat_rb = n_ctx // ll

    new_ak, new_av, new_dk, new_dv, new_ss = [], [], [], [], []
    for l in range(depth):
        modl = mod[l].reshape(N_COND_ROWS, N_MOD, 1, dm)

        def ffn(x, j, k0, row_ranges=(None,)):
            h = _rmsmod(x, norm_g[l, 2 * j], modl, k0, k0 + 1, tok)
            act, w_d_b = _gu(h, ffn_w_gu, ffn_w_d, (l, j), tm, 256)
            outs = [_mm_resid(act, w_d_b, (), x, modl, k0 + 2, 0.5, tok, 512, 512, "ffn_down", rows)
                    for rows in row_ranges]
            return outs[0] if len(outs) == 1 else outs

        x = ffn(x, 0, 0)

        h = _rmsmod(x, norm_g[l, 1], modl, 3, 4, tok)
        z = _mm_ws(h, w_in, (l,), (0, c_gt), F32, tm_in, 512, "in_proj")
        zg = _mm_ws(h, w_in, (l,), (c_gt, N_BRANCH * dm), BF16, tm_in, 512, "in_proj_gate")

        oa_c, ak_n, av_n = _attn_a(z, ctx_rb, bc, lc, c_aq, c_ak, c_av, a_q_norm[l], a_k_norm[l], a_sink[l])
        ctx_a = (cache_a_k[:, l].reshape(bl, past, a_kv * A_HEAD_DIM), cache_a_v[:, l].reshape(bl, past, a_kv * A_HEAD_DIM))
        oa_l = _attn_a(z, lat_rb, bl, ll, c_aq, c_ak, c_av, a_q_norm[l], a_k_norm[l], a_sink[l], ctx_a)
        new_ak.append(ak_n.reshape(bc, lc, a_kv, A_HEAD_DIM))
        new_av.append(av_n.reshape(bc, lc, a_kv, A_HEAD_DIM))

        hy_args = (hy_w_in[l], hy_b_in[l], hy_w_hid[l], hy_b_hid[l], hy_w_out[l], hy_freq[l])
        filt_c = _hy_filter(lc, *hy_args, tabs_c)
        filt_l = _hy_filter(ll, *hy_args, tabs_l)
        ob_c = _hyena(z, ctx_rb, bc, lc, c_hy, mix, hy_conv_w[l], hy_conv_b[l], filt_c, hy_bias[l], tabs_cb)
        ob_l = _hyena(z, lat_rb, bl, ll, c_hy, mix, hy_conv_w[l], hy_conv_b[l], filt_l, hy_bias[l], tabs_lb)

        s5w = _s5_weights(s5_lam_re[l], s5_lam_im[l], s5_log_step[l], s5_b_re[l], s5_b_im[l], s5_c_re[l], s5_c_im[l])
        su = z[:, c_su:c_su + mix]
        h0_c = jnp.zeros((2, bc, 2 * ns), F32)
        st = state_ssm[:, l]
        h0_l = jnp.transpose(st, (1, 0, 4, 2, 3)).reshape(2, bl, 2 * ns)
        oc_c, hf = _mixer_c(su[:n_ctx], bc, lc, s5w, s5_d[l], glu_w_b[l], s5_glu_b[l], h0_c)
        oc_l, _ = _mixer_c(su[n_ctx:], bl, ll, s5w, s5_d[l], glu_w_b[l], s5_glu_b[l], h0_l)
        new_ss.append(jnp.transpose(hf.reshape(2, bc, 2, groups, n_state), (1, 0, 3, 4, 2)))

        lam_init = 0.8 - 0.6 * math.exp(-0.3 * l)
        d_args = (d_q_norm[l], d_k_norm[l], d_lambda[l], d_subln[l], lam_init)
        od_c, dk_n, dv_n = _attn_d(z, ctx_rb, bc, lc, c_dq, c_dk, c_dv, *d_args)
        ctx_d = (cache_d_k[:, l].reshape(bl, past, D_HEADS * d_width), cache_d_v[:, l].reshape(bl, past, D_HEADS * d_width))
        od_l = _attn_d(z, lat_rb, bl, ll, c_dq, c_dk, c_dv, *d_args, ctx_d)
        new_dk.append(dk_n.reshape(bc, lc, D_HEADS, d_width))
        new_dv.append(dv_n.reshape(bc, lc, D_HEADS, d_width))

        s = _branch((oa_c, ob_c, oc_c, od_c), (oa_l, ob_l, oc_l, od_l), w_br_b, l, zg, tok, tm, 512)
        x = _mm_resid(s, w_out_b, (l,), x, modl, 5, 1.0, tok, tm, 512, "out_proj")

        if l + 1 < depth:
            x = ffn(x, 1, 6)
        else:
            y_prompt, y_sample = ffn(x, 1, 6, ((0, n_ctx), (n_ctx, n_lat)))

    y_prompt = y_prompt.reshape(bc, lc, dm)
    y_sample = y_sample.reshape(bl, ll, dm)
    return (y_prompt, y_sample, jnp.stack(new_ak, axis=1), jnp.stack(new_av, axis=1), jnp.stack(new_dk, axis=1),
            jnp.stack(new_dv, axis=1), jnp.stack(new_ss, axis=1))
```
